```python
import math
import jax, jax.numpy as jnp
from jax import lax
import numpy as np

D_MODEL = 1024
BATCH = 4
SEQ = 8192
DEPTH = 1

GDN_HEADS = 4
GDN_DK = 128
GDN_DV = 128
CONV_K = 4
CHUNK = 64
NSA_HEADS = 8
NSA_KV_HEADS = 2
NSA_GROUP = NSA_HEADS // NSA_KV_HEADS
NSA_DH = 64
CMP_LEN = 32
CMP_STRIDE = 16
CMP_HIDDEN = 128
SEL_BLOCK = 64
SEL_TOP_N = 16
WINDOW = 512
Q_BLOCK = 128
FORCE_BONUS = 1e4
N_BUCKETS = 32
MAX_DISTANCE = 1024
EPS = 1e-6
NEG = -1e30

GDN_QK_W = GDN_HEADS * GDN_DK
GDN_W = GDN_HEADS * GDN_DV
GDN_CONV_W = 2 * GDN_QK_W + GDN_W
NSA_W = NSA_HEADS * NSA_DH
KV_W = NSA_KV_HEADS * NSA_DH
MIX_W = GDN_W + NSA_W
PROJ_SIZES = (GDN_QK_W, GDN_QK_W, GDN_W, GDN_W, GDN_HEADS, GDN_HEADS,
              NSA_W, KV_W, KV_W, KV_W, KV_W, KV_W, KV_W, 3 * NSA_HEADS, NSA_W)
PROJ_W = sum(PROJ_SIZES)

kernel_name = "hymba_gdn_nsa_hybrid"


def rmsnorm(x, w):
    xf = x.astype(jnp.float32)
    y = xf * lax.rsqrt(jnp.mean(xf * xf, axis=-1, keepdims=True) + EPS)
    return (y * w.astype(jnp.float32)).astype(x.dtype)


def l2norm(t):
    return t * lax.rsqrt(jnp.sum(t * t, axis=-1, keepdims=True) + EPS)


def causal_conv(x, w):
    c = x.shape[-1]
    return lax.conv_general_dilated(x, w.astype(x.dtype)[:, None, :], window_strides=(1,),
                                    padding=[(CONV_K - 1, 0)],
                                    dimension_numbers=('NWC', 'WIO', 'NWC'),
                                    feature_group_count=c)


def rel_bucket(dist):
    n = jnp.maximum(dist, 0)
    max_exact = N_BUCKETS // 2
    large = max_exact + (jnp.log(jnp.maximum(n, max_exact).astype(jnp.float32) / max_exact)
                         / math.log(MAX_DISTANCE / max_exact) * (N_BUCKETS - max_exact)).astype(jnp.int32)
    return jnp.where(n < max_exact, n, jnp.minimum(large, N_BUCKETS - 1))


def gated_delta_rule(q, k, v, beta, g):
    B, S, H, dk = q.shape
    dv = v.shape[-1]
    N = S // CHUNK

    def chunk(t):
        return t.reshape((B, N, CHUNK, H) + t.shape[3:]).swapaxes(2, 3)

    q, k, v, beta, g = chunk(q), chunk(k), chunk(v), chunk(beta), chunk(g)
    g = jnp.cumsum(g, axis=-1)
    idx = jnp.arange(CHUNK)
    tril = idx[:, None] >= idx[None, :]
    tril_strict = idx[:, None] > idx[None, :]
    diff = g[..., :, None] - g[..., None, :]
    decay = jnp.where(tril, jnp.exp(jnp.where(tril, diff, 0.0)), 0.0)
    kb = k * beta[..., None]
    L = jnp.where(tril_strict, jnp.einsum('bnhcd,bnhed->bnhce', kb, k) * decay, 0.0)
    A = L + jnp.eye(CHUNK, dtype=L.dtype)
    rhs = jnp.concatenate([v * beta[..., None], kb * jnp.exp(g)[..., None]], axis=-1)
    sol = lax.linalg.triangular_solve(A, rhs, left_side=True, lower=True, unit_diagonal=True)
    u, w = sol[..., :dv], sol[..., dv:]
    attn_qk = jnp.where(tril, jnp.einsum('bnhcd,bnhed->bnhce', q, k) * decay, 0.0)
    g_last = g[..., -1]
    k_dec = k * jnp.exp(g_last[..., None] - g)[..., None]
    q_dec = q * jnp.exp(g)[..., None]

    def step(state, inp):
        qd, kd, u_, w_, a_, gl = inp
        v_new = u_ - jnp.einsum('bhcd,bhde->bhce', w_, state)
        o = jnp.einsum('bhcd,bhde->bhce', qd, state) + jnp.einsum('bhce,bhef->bhcf', a_, v_new)
        state = state * jnp.exp(gl)[..., None, None] + jnp.einsum('bhcd,bhce->bhde', kd, v_new)
        return state, o

    xs = tuple(jnp.moveaxis(t, 1, 0) for t in (q_dec, k_dec, u, w, attn_qk, g_last))
    s0 = jnp.zeros((B, H, dk, dv), jnp.float32)
    _, o = lax.scan(step, s0, xs)
    return jnp.moveaxis(o, 0, 1).swapaxes(2, 3).reshape(B, S, H, dv)


def compress(raw, pe, w1, w2):
    S = raw.shape[1]
    n_cmp = (S - CMP_LEN) // CMP_STRIDE + 1
    idx = CMP_STRIDE * np.arange(n_cmp)[:, None] + np.arange(CMP_LEN)[None, :]
    blocks = raw[:, idx] + pe[None, None, :, None, :]
    h = jax.nn.silu(jnp.einsum('bnlhd,ldf->bnhf', blocks, w1))
    return jnp.einsum('bnhf,fd->bhnd', h, w2)


def nsa_mixer(nq, kc, vc, ks, vs, kw, vw, ng, pe_k, pe_v, kw1, kw2, vw1, vw2, rel_bias):
    f32 = jnp.float32
    B, S, _ = nq.shape
    Hk, G, dh = NSA_KV_HEADS, NSA_GROUP, NSA_DH
    n_sel = S // SEL_BLOCK
    n_top = min(SEL_TOP_N, n_sel)
    n_cmp = (S - CMP_LEN) // CMP_STRIDE + 1
    q = nq.astype(f32).reshape(B, S, Hk, G, dh).transpose(0, 2, 3, 1, 4) * dh ** -0.5
    gates = jax.nn.sigmoid(ng.astype(f32)).reshape(B, S, 3, Hk, G).transpose(0, 3, 4, 1, 2)

    def heads(t):
        return t.astype(f32).reshape(B, S, Hk, dh)

    k_cmp = compress(heads(kc), pe_k.astype(f32), kw1.astype(f32), kw2.astype(f32))
    v_cmp = compress(heads(vc), pe_v.astype(f32), vw1.astype(f32), vw2.astype(f32))
    k_slc = heads(ks).transpose(0, 2, 1, 3).reshape(B, Hk, n_sel, SEL_BLOCK, dh)
    v_slc = heads(vs).transpose(0, 2, 1, 3).reshape(B, Hk, n_sel, SEL_BLOCK, dh)
    pad = ((0, 0), (0, 0), (WINDOW, 0), (0, 0))
    k_win = jnp.pad(heads(kw).transpose(0, 2, 1, 3), pad)
    v_win = jnp.pad(heads(vw).transpose(0, 2, 1, 3), pad)
    rel_tab = rel_bias.astype(f32).reshape(N_BUCKETS, Hk, G)

    cs = CMP_STRIDE * np.arange(n_cmp)
    ss = SEL_BLOCK * np.arange(n_sel)
    overlap = jnp.asarray(((cs[:, None] < ss[None, :] + SEL_BLOCK) &
                           (cs[:, None] + CMP_LEN > ss[None, :])).astype(np.float32))
    cmp_end = jnp.asarray((cs + CMP_LEN - 1).astype(np.int32))
    b_ix = jnp.arange(B)[:, None, None, None]
    h_ix = jnp.arange(Hk)[None, :, None, None]
    h_ix5 = jnp.arange(Hk)[None, :, None, None, None]
    g_ix5 = jnp.arange(G)[None, None, :, None, None]
    j = jnp.arange(n_sel)

    def block(qb):
        q0 = qb * Q_BLOCK
        t = q0 + jnp.arange(Q_BLOCK)
        qblk = lax.dynamic_slice_in_dim(q, q0, Q_BLOCK, axis=3)
        gblk = lax.dynamic_slice_in_dim(gates, q0, Q_BLOCK, axis=3)
        mask_c = cmp_end[None, :] <= t[:, None]
        bias_c = rel_tab[rel_bucket(t[:, None] - cmp_end[None, :])].transpose(2, 3, 0, 1)
        s_c = jnp.einsum('bhgtd,bhnd->bhgtn', qblk, k_cmp) + bias_c
        p_c = jax.nn.softmax(jnp.where(mask_c, s_c, NEG), axis=-1) * mask_c.astype(f32)
        o_c = jnp.einsum('bhgtn,bhnd->bhgtd', p_c, v_cmp)
        imp = jnp.einsum('bhgtn,nj->bhtj', p_c, overlap)
        cb = t // SEL_BLOCK
        valid = j[None, :] <= cb[:, None]
        forced = (j[None, :] == 0) | (j[None, :] == cb[:, None]) | (j[None, :] == cb[:, None] - 1)
        score = jnp.where(valid, imp + jnp.where(forced, FORCE_BONUS, 0.0), NEG)
        _, sel = lax.top_k(score, n_top)
        kb = k_slc[b_ix, h_ix, sel].reshape(B, Hk, Q_BLOCK, n_top * SEL_BLOCK, dh)
        vb = v_slc[b_ix, h_ix, sel].reshape(B, Hk, Q_BLOCK, n_top * SEL_BLOCK, dh)
        pos = (sel[..., None] * SEL_BLOCK + jnp.arange(SEL_BLOCK)).reshape(B, Hk, Q_BLOCK, n_top * SEL_BLOCK)
        dist_s = t[None, None, :, None] - pos
        bias_s = rel_tab[rel_bucket(dist_s)[:, :, None], h_ix5, g_ix5]
        s_s = jnp.einsum('bhgtd,bhtkd->bhgtk', qblk, kb) + bias_s
        p_s = jax.nn.softmax(jnp.where((dist_s >= 0)[:, :, None], s_s, NEG), axis=-1)
        o_s = jnp.einsum('bhgtk,bhtkd->bhgtd', p_s, vb)
        kwb = lax.dynamic_slice_in_dim(k_win, q0, WINDOW + Q_BLOCK, axis=2)
        vwb = lax.dynamic_slice_in_dim(v_win, q0, WINDOW + Q_BLOCK, axis=2)
        kpos = q0 - WINDOW + jnp.arange(WINDOW + Q_BLOCK)
        dist_w = t[:, None] - kpos[None, :]
        mask_w = (dist_w >= 0) & (dist_w < WINDOW) & (kpos[None, :] >= 0)
        bias_w = rel_tab[rel_bucket(dist_w)].transpose(2, 3, 0, 1)
        s_w = jnp.einsum('bhgtd,bhkd->bhgtk', qblk, kwb) + bias_w
        p_w = jax.nn.softmax(jnp.where(mask_w, s_w, NEG), axis=-1)
        o_w = jnp.einsum('bhgtk,bhkd->bhgtd', p_w, vwb)
        return gblk[..., 0:1] * o_c + gblk[..., 1:2] * o_s + gblk[..., 2:3] * o_w

    outs = lax.map(block, jnp.arange(S // Q_BLOCK))
    return outs.transpose(1, 0, 4, 2, 3, 5).reshape(B, S, NSA_W)


def setup_inputs(seed: int = 0) -> dict:
    key = jax.random.key(seed)
    ks = jax.random.split(key, 18)
    nrm = jax.random.normal
    D = D_MODEL
    dt = jnp.exp(jax.random.uniform(ks[5], (DEPTH, GDN_HEADS)) * (math.log(0.1) - math.log(0.001)) + math.log(0.001))
    return {
        "x": nrm(ks[0], (BATCH, SEQ, D), jnp.float32),
        "norm_w": 1.0 + 0.02 * nrm(ks[1], (DEPTH, D)),
        "w_in": nrm(ks[2], (DEPTH, D, PROJ_W)) * D ** -0.5,
        "conv_w": nrm(ks[3], (DEPTH, CONV_K, GDN_CONV_W)) * CONV_K ** -0.5,
        "a_log": jnp.log(jax.random.uniform(ks[4], (DEPTH, GDN_HEADS), minval=1.0, maxval=16.0)),
        "dt_bias": dt + jnp.log(-jnp.expm1(-dt)),
        "gdn_norm_w": 1.0 + 0.02 * nrm(ks[6], (DEPTH, GDN_DV)),
        "cmp_pe_k": 0.1 * nrm(ks[7], (DEPTH, CMP_LEN, NSA_DH)),
        "cmp_pe_v": 0.1 * nrm(ks[8], (DEPTH, CMP_LEN, NSA_DH)),
        "cmp_k_w1": nrm(ks[9], (DEPTH, CMP_LEN, NSA_DH, CMP_HIDDEN)) * (CMP_LEN * NSA_DH) ** -0.5,
        "cmp_k_w2": nrm(ks[10], (DEPTH, CMP_HIDDEN, NSA_DH)) * CMP_HIDDEN ** -0.5,
        "cmp_v_w1": nrm(ks[11], (DEPTH, CMP_LEN, NSA_DH, CMP_HIDDEN)) * (CMP_LEN * NSA_DH) ** -0.5,
        "cmp_v_w2": nrm(ks[12], (DEPTH, CMP_HIDDEN, NSA_DH)) * CMP_HIDDEN ** -0.5,
        "w_out": nrm(ks[13], (DEPTH, MIX_W, D)) * MIX_W ** -0.5,
        "rel_bias": 0.5 * nrm(ks[14], (N_BUCKETS, NSA_HEADS)),
        "final_norm_w": 1.0 + 0.02 * nrm(ks[15], (D,)),
    }


def reference(x, norm_w, w_in, conv_w, a_log, dt_bias, gdn_norm_w, cmp_pe_k, cmp_pe_v,
              cmp_k_w1, cmp_k_w2, cmp_v_w1, cmp_v_w2, w_out, rel_bias, final_norm_w):
    f32 = jnp.float32
    B, S, _ = x.shape
    split_points = [int(p) for p in np.cumsum(PROJ_SIZES)[:-1]]
    h = x
    for layer in range(DEPTH):
        u = rmsnorm(h, norm_w[layer])
        proj = u @ w_in[layer]
        (gq, gk, gv, gz, gb, ga, nq, kc, vc, ks, vs, kw, vw, ng, nz) = jnp.split(proj, split_points, axis=-1)
        qkv = jax.nn.silu(causal_conv(jnp.concatenate([gq, gk, gv], axis=-1), conv_w[layer])).astype(f32)
        q, k, v = jnp.split(qkv, [GDN_QK_W, 2 * GDN_QK_W], axis=-1)
        q = l2norm(q.reshape(B, S, GDN_HEADS, GDN_DK)) * GDN_DK ** -0.5
        k = l2norm(k.reshape(B, S, GDN_HEADS, GDN_DK))
        v = v.reshape(B, S, GDN_HEADS, GDN_DV)
        beta = jax.nn.sigmoid(gb.astype(f32))
        g = -jnp.exp(a_log[layer].astype(f32)) * jax.nn.softplus(ga.astype(f32) + dt_bias[layer].astype(f32))
        o_g = gated_delta_rule(q, k, v, beta, g)
        o_g = o_g * lax.rsqrt(jnp.mean(o_g * o_g, axis=-1, keepdims=True) + EPS) * gdn_norm_w[layer].astype(f32)
        o_g = o_g.reshape(B, S, GDN_W) * jax.nn.silu(gz.astype(f32))
        o_n = nsa_mixer(nq, kc, vc, ks, vs, kw, vw, ng, cmp_pe_k[layer], cmp_pe_v[layer],
                        cmp_k_w1[layer], cmp_k_w2[layer], cmp_v_w1[layer], cmp_v_w2[layer], rel_bias)
        o_n = o_n * jax.nn.silu(nz.astype(f32))
        mix = jnp.concatenate([o_g, o_n], axis=-1).astype(x.dtype)
        h = h + mix @ w_out[layer]
    return rmsnorm(h, final_norm_w)
```

```python
import functools
import math

import jax
import jax.numpy as jnp
import numpy as np
from jax import lax
from jax.experimental import pallas as pl
from jax.experimental.pallas import tpu as pltpu

D_MODEL = 1024
GDN_HEADS = 4
GDN_DK = 128
GDN_DV = 128
CONV_K = 4
CHUNK = 64
NSA_HEADS = 8
NSA_KV_HEADS = 2
NSA_GROUP = NSA_HEADS // NSA_KV_HEADS
NSA_DH = 64
CMP_LEN = 32
CMP_STRIDE = 16
CMP_HIDDEN = 128
SEL_BLOCK = 64
SEL_TOP_N = 16
WINDOW = 512
FORCE_BONUS = 1e4
N_BUCKETS = 32
MAX_DISTANCE = 1024
EPS = 1e-6
NEG = -1e30

GDN_QK_W = GDN_HEADS * GDN_DK
GDN_W = GDN_HEADS * GDN_DV
GDN_CONV_W = 2 * GDN_QK_W + GDN_W
NSA_W = NSA_HEADS * NSA_DH
KV_W = NSA_KV_HEADS * NSA_DH
PROJ_SIZES = (GDN_QK_W, GDN_QK_W, GDN_W, GDN_W, GDN_HEADS, GDN_HEADS,
              NSA_W, KV_W, KV_W, KV_W, KV_W, KV_W, KV_W, 3 * NSA_HEADS, NSA_W)

LANES = 128
TQ = 128
QL = NSA_GROUP * TQ
GQ_W = NSA_GROUP * NSA_DH
N_NEAR = 8
TAB_FAR = N_NEAR
TAB_WIN_EDGE = N_NEAR + 1
TAB_CMP = N_NEAR + 2
N_TAB = N_NEAR + 3
CMP_NEAR = 64
CMP_NEAR_BACK = 56
VMEM_LIMIT = 56 * 1024 * 1024

_F32 = jnp.float32
_BF16 = jnp.bfloat16


def _dot(a, b):
    return jnp.dot(a, b, preferred_element_type=_F32)


def _dot_nt(a, b):
    return lax.dot_general(a, b, (((1,), (1,)), ((), ())), preferred_element_type=_F32)


def _dot_tn(a, b):
    return lax.dot_general(a, b, (((0,), (0,)), ((), ())), preferred_element_type=_F32)


def _sigmoid(x):
    return 1.0 / (1.0 + jnp.exp(-x))


def _silu(x):
    return x * _sigmoid(x)


def _bias_tab_kernel(rel_ref, o_ref):
    hk = pl.program_id(0)
    e = pl.program_id(1)
    is_cmp = e == TAB_CMP
    base = jnp.where(e < N_NEAR, e * TQ,
                     jnp.where(e == TAB_FAR, 1 << 20,
                               jnp.where(e == TAB_WIN_EDGE, WINDOW,
                                         CMP_STRIDE * CMP_NEAR_BACK - (CMP_LEN - 1))))
    row_scale = jnp.where(is_cmp, CMP_STRIDE, 1)
    upper = jnp.where(e == TAB_WIN_EDGE, WINDOW, 1 << 30)
    row = lax.broadcasted_iota(jnp.int32, (TQ, TQ), 0)
    tok = lax.broadcasted_iota(jnp.int32, (TQ, TQ), 1)
    dist = base + tok - row_scale * row
    n = jnp.maximum(dist, 0)
    max_exact = N_BUCKETS // 2
    large = max_exact + (jnp.log(jnp.maximum(n, max_exact).astype(_F32) / max_exact)
                         / math.log(MAX_DISTANCE / max_exact) * (N_BUCKETS - max_exact)).astype(jnp.int32)
    bucket = jnp.where(n < max_exact, n, jnp.minimum(large, N_BUCKETS - 1))
    ok = (dist >= 0) & (dist < upper)
    for g in range(NSA_GROUP):
        head = hk * NSA_GROUP + g
        val = jnp.zeros((TQ, TQ), _F32)
        for b in range(N_BUCKETS):
            val = jnp.where(bucket == b, rel_ref[b, head], val)
        o_ref[0, 0, :, g * TQ:(g + 1) * TQ] = jnp.where(ok, val, NEG)


def _bias_tables(rel_bias):
    return pl.pallas_call(
        _bias_tab_kernel,
        out_shape=jax.ShapeDtypeStruct((NSA_KV_HEADS, N_TAB, TQ, QL), _F32),
        grid=(NSA_KV_HEADS, N_TAB),
        in_specs=[pl.BlockSpec(memory_space=pltpu.SMEM)],
        out_specs=pl.BlockSpec((1, 1, TQ, QL), lambda h, e: (h, e, 0, 0)),
        name="bias_tables",
    )(rel_bias.astype(_F32))


_C_CONV = (0, GDN_CONV_W)
_C_GZ = (_C_CONV[1], _C_CONV[1] + GDN_W)
_C_SMALL = (_C_GZ[1], _C_GZ[1] + LANES)
_C_NQ = (_C_SMALL[1], _C_SMALL[1] + NSA_W)
_C_KC = (_C_NQ[1], _C_NQ[1] + KV_W)
_C_VC = (_C_KC[1], _C_KC[1] + KV_W)
_C_KS = (_C_VC[1], _C_VC[1] + KV_W)
_C_KW = (_C_KS[1], _C_KS[1] + KV_W)
_C_VS = (_C_KW[1], _C_KW[1] + KV_W)
_C_VW = (_C_VS[1], _C_VS[1] + KV_W)
_C_NZ = (_C_VW[1], _C_VW[1] + NSA_W)
_W_COLS = _C_NZ[1]


def _regroup_w_in(w):
    gq, gk, gv, gz, gb, ga, nq, kc, vc, ks, vs, kw, vw, ng, nz = jnp.split(
        w, [int(p) for p in np.cumsum(PROJ_SIZES)[:-1]], axis=-1)
    small = jnp.concatenate([gb, ga, ng], axis=-1)
    small = jnp.pad(small, ((0, 0), (0, LANES - small.shape[-1])))
    return jnp.concatenate([gq, gk, gv, gz, small, nq, kc, vc, ks, kw, vs, vw, nz], axis=-1).astype(_BF16)


def _in_proj_kernel(x_ref, nw_ref, w_ref, conv_ref, gz_ref, sm_ref, smt_ref, nq_ref, kc_ref, vc_ref,
                    ks_ref, kw_ref, vst_ref, vwt_ref, nz_ref):
    x = x_ref[...]
    u = x * lax.rsqrt(jnp.mean(x * x, axis=-1, keepdims=True) + EPS) * nw_ref[...]
    u = u.astype(_BF16)

    def proj(c):
        return _dot(u, w_ref[:, c[0]:c[1]])

    conv_ref[...] = proj(_C_CONV)
    gz_ref[...] = proj(_C_GZ)
    sm = proj(_C_SMALL)
    sm_ref[...] = sm
    smt_ref[0] = sm.T
    nq_ref[...] = proj(_C_NQ).astype(_BF16)
    kc_ref[...] = proj(_C_KC).astype(_BF16)
    vc_ref[...] = proj(_C_VC).astype(_BF16)
    ks_ref[...] = proj(_C_KS).astype(_BF16)
    kw_ref[...] = proj(_C_KW).astype(_BF16)
    vst_ref[0] = proj(_C_VS).T.astype(_BF16)
    vwt_ref[0] = proj(_C_VW).T.astype(_BF16)
    nz_ref[...] = proj(_C_NZ)


def _in_proj(x2d, norm_w, w_cat, batch, seq, tm):
    n = x2d.shape[0]
    tiles_per_seq = seq // tm
    row = lambda w: pl.BlockSpec((tm, w), lambda i: (i, 0))
    tr = pl.BlockSpec((1, LANES, tm), lambda i: (i // tiles_per_seq, 0, i % tiles_per_seq))
    sds = jax.ShapeDtypeStruct
    return pl.pallas_call(
        _in_proj_kernel,
        out_shape=(
            sds((n, GDN_CONV_W), _F32), sds((n, GDN_W), _F32), sds((n, LANES), _F32),
            sds((batch, LANES, seq), _F32), sds((n, NSA_W), _BF16), sds((n, KV_W), _BF16),
            sds((n, KV_W), _BF16), sds((n, KV_W), _BF16), sds((n, KV_W), _BF16),
            sds((batch, KV_W, seq), _BF16), sds((batch, KV_W, seq), _BF16), sds((n, NSA_W), _F32)),
        grid=(n // tm,),
        in_specs=[row(D_MODEL), pl.BlockSpec((1, D_MODEL), lambda i: (0, 0)),
                  pl.BlockSpec((D_MODEL, _W_COLS), lambda i: (0, 0))],
        out_specs=(row(GDN_CONV_W), row(GDN_W), row(LANES), tr, row(NSA_W), row(KV_W), row(KV_W),
                   row(KV_W), row(KV_W), tr, tr, row(NSA_W)),
        compiler_params=pltpu.CompilerParams(dimension_semantics=("arbitrary",),
                                             vmem_limit_bytes=VMEM_LIMIT),
        name="in_proj",
    )(x2d, norm_w.reshape(1, D_MODEL).astype(_F32), w_cat)


_GDN_PAIR = 2 * CHUNK


def _segment_cumsum(g, axis):
    pos = lax.broadcasted_iota(jnp.int32, g.shape, axis) % CHUNK
    k = 1
    while k < CHUNK:
        g = g + jnp.where(pos >= k, pltpu.roll(g, k, axis), 0.0)
        k *= 2
    return g


def _unit_lower_inverse(low):
    r = lax.broadcasted_iota(jnp.int32, (CHUNK, CHUNK), 0)
    c = lax.broadcasted_iota(jnp.int32, (CHUNK, CHUNK), 1)
    eye = (r == c).astype(_F32)
    m = -low
    x = eye + m
    span = 2
    while span < CHUNK:
        m = _dot(m, m)
        x = x + _dot(x, m)
        span *= 2
    return x


def _gdn_kernel(x_ref, cw_ref, sm_ref, smt_ref, prow_ref, pcol_ref, gz_ref, gnw_ref, o_ref,
                xs_ref, q_ref, k_ref, v_ref, col_ref, grow_ref, state_ref, *, tile):
    t_idx = pl.program_id(1)

    @pl.when(t_idx == 0)
    def _():
        xs_ref[0:8, :] = jnp.zeros((8, GDN_CONV_W), _F32)
        state_ref[...] = jnp.zeros_like(state_ref)

    @pl.when(t_idx != 0)
    def _():
        xs_ref[0:8, :] = xs_ref[tile:tile + 8, :]

    xs_ref[8:tile + 8, :] = x_ref[0]

    y = jnp.zeros((tile, GDN_CONV_W), _F32)
    for tap in range(CONV_K):
        off = 8 - (CONV_K - 1) + tap
        y = y + cw_ref[tap:tap + 1, :] * xs_ref[off:off + tile, :]
    y = _silu(y)

    for h in range(GDN_HEADS):
        lo = h * GDN_DK
        qh = y[:, lo:lo + GDN_DK]
        kh = y[:, GDN_QK_W + lo:GDN_QK_W + lo + GDN_DK]
        q_ref[:, lo:lo + GDN_DK] = qh * lax.rsqrt(jnp.sum(qh * qh, axis=-1, keepdims=True) + EPS) * GDN_DK ** -0.5
        k_ref[:, lo:lo + GDN_DK] = kh * lax.rsqrt(jnp.sum(kh * kh, axis=-1, keepdims=True) + EPS)
    v_ref[...] = y[:, 2 * GDN_QK_W:]

    sm = sm_ref[0]
    lane = lax.broadcasted_iota(jnp.int32, sm.shape, 1)
    zc = sm + prow_ref[1:2, :]
    gcol = -jnp.exp(prow_ref[0:1, :]) * (jnp.maximum(zc, 0.0) + jnp.log1p(jnp.exp(-jnp.abs(zc))))
    gcol = _segment_cumsum(gcol, 0)
    col_ref[...] = jnp.where(lane < GDN_HEADS, _sigmoid(sm), gcol)
    zr = smt_ref[0, 0:8, :] + pcol_ref[1]
    grow = -jnp.exp(pcol_ref[0]) * (jnp.maximum(zr, 0.0) + jnp.log1p(jnp.exp(-jnp.abs(zr))))
    grow_ref[...] = _segment_cumsum(grow, 1)

    r = lax.broadcasted_iota(jnp.int32, (CHUNK, CHUNK), 0)
    c = lax.broadcasted_iota(jnp.int32, (CHUNK, CHUNK), 1)
    tril = r >= c
    strict = r > c
    gnw = gnw_ref[...]

    def pair_body(pi, carry):
        base = pl.multiple_of(pi * _GDN_PAIR, _GDN_PAIR)
        grow2 = grow_ref[:, pl.ds(base, _GDN_PAIR)]
        for half in range(2):
            rows = pl.ds(pl.multiple_of(base + half * CHUNK, CHUNK), CHUNK)
            colv = col_ref[rows, :]
            for h in range(GDN_HEADS):
                lo = h * GDN_DK
                q = q_ref[rows, lo:lo + GDN_DK]
                k = k_ref[rows, lo:lo + GDN_DK]
                v = v_ref[rows, lo:lo + GDN_DV]
                beta = colv[:, h:h + 1]
                gc = colv[:, GDN_HEADS + h:GDN_HEADS + h + 1]
                gr = grow2[GDN_HEADS + h:GDN_HEADS + h + 1, half * CHUNK:(half + 1) * CHUNK]
                decay = jnp.where(tril, jnp.exp(jnp.where(tril, gc - gr, 0.0)), 0.0)
                kb = k * beta
                low = jnp.where(strict, _dot_nt(kb, k) * decay, 0.0)
                tinv = _unit_lower_inverse(low)
                eg = jnp.exp(gc)
                sol = _dot(tinv, jnp.concatenate([v * beta, kb * eg], axis=1))
                u = sol[:, :GDN_DV]
                w = sol[:, GDN_DV:]
                attn = jnp.where(tril, _dot_nt(q, k) * decay, 0.0)
                g_last = gc[CHUNK - 1:CHUNK, :]
                k_dec = k * jnp.exp(g_last - gc)
                q_dec = q * eg
                st = state_ref[h]
                v_new = u - _dot(w, st)
                o = _dot(q_dec, st) + _dot(attn, v_new)
                state_ref[h] = st * jnp.exp(g_last) + _dot_tn(k_dec, v_new)
                o = o * lax.rsqrt(jnp.mean(o * o, axis=-1, keepdims=True) + EPS) * gnw
                o_ref[0, rows, lo:lo + GDN_DV] = o * _silu(gz_ref[0, rows, lo:lo + GDN_DV])
        return carry

    lax.fori_loop(0, tile // _GDN_PAIR, pair_body, 0)


def _gdn(conv_in, conv_w, sm, smt, a_log, dt_bias, gz, gdn_norm_w, tile):
    batch, seq, _ = conv_in.shape
    prow = jnp.zeros((2, LANES), _F32)
    prow = prow.at[0, GDN_HEADS:2 * GDN_HEADS].set(a_log.astype(_F32))
    prow = prow.at[1, GDN_HEADS:2 * GDN_HEADS].set(dt_bias.astype(_F32))
    pcol = jnp.broadcast_to(prow[:, :8, None], (2, 8, tile))
    kern = functools.partial(_gdn_kernel, tile=tile)
    return pl.pallas_call(
        kern,
        out_shape=jax.ShapeDtypeStruct((batch, seq, GDN_W), _F32),
        grid=(batch, seq // tile),
        in_specs=[
            pl.BlockSpec((1, tile, GDN_CONV_W), lambda b, t: (b, t, 0)),
            pl.BlockSpec((CONV_K, GDN_CONV_W), lambda b, t: (0, 0)),
            pl.BlockSpec((1, tile, LANES), lambda b, t: (b, t, 0)),
            pl.BlockSpec((1, LANES, tile), lambda b, t: (b, 0, t)),
            pl.BlockSpec((2, LANES), lambda b, t: (0, 0)),
            pl.BlockSpec((2, 8, tile), lambda b, t: (0, 0, 0)),
            pl.BlockSpec((1, tile, GDN_W), lambda b, t: (b, t, 0)),
            pl.BlockSpec((1, GDN_DV), lambda b, t: (0, 0)),
        ],
        out_specs=pl.BlockSpec((1, tile, GDN_W), lambda b, t: (b, t, 0)),
        scratch_shapes=[
            pltpu.VMEM((tile + 8, GDN_CONV_W), _F32),
            pltpu.VMEM((tile, GDN_QK_W), _F32),
            pltpu.VMEM((tile, GDN_QK_W), _F32),
            pltpu.VMEM((tile, GDN_W), _F32),
            pltpu.VMEM((tile, LANES), _F32),
            pltpu.VMEM((8, tile), _F32),
            pltpu.VMEM((GDN_HEADS, GDN_DK, GDN_DV), _F32),
        ],
        compiler_params=pltpu.CompilerParams(dimension_semantics=("arbitrary", "arbitrary"),
                                             vmem_limit_bytes=VMEM_LIMIT),
        name="gdn",
    )(conv_in, conv_w.astype(_F32), sm, smt, prow, pcol, gz, gdn_norm_w.reshape(1, GDN_DV).astype(_F32))


_GROUP_W = CMP_STRIDE * KV_W
_HID2 = NSA_KV_HEADS * CMP_HIDDEN


def _compress_weights(pe, w1, w2):
    eye = jnp.eye(NSA_KV_HEADS, dtype=_F32)
    halves = []
    pes = []
    for part in range(CMP_LEN // CMP_STRIDE):
        w1p = w1[part * CMP_STRIDE:(part + 1) * CMP_STRIDE].astype(_F32)
        big = jnp.einsum('ldf,hg->lhdgf', w1p, eye).reshape(_GROUP_W, _HID2)
        halves.append(big.astype(_BF16))
        pep = pe[part * CMP_STRIDE:(part + 1) * CMP_STRIDE].astype(_F32)
        pes.append(jnp.broadcast_to(pep[:, None, :], (CMP_STRIDE, NSA_KV_HEADS, NSA_DH)).reshape(1, _GROUP_W))
    w2big = jnp.einsum('fd,hg->hfgd', w2.astype(_F32), eye).reshape(_HID2, KV_W).astype(_BF16)
    pe2 = jnp.broadcast_to(jnp.stack(pes, 0), (2, 8, _GROUP_W)).astype(_BF16)
    return halves[0], halves[1], w2big, pe2


def _compress_kernel(kc_ref, vc_ref, kwa_ref, kwb_ref, kw2_ref, kpe_ref, vwa_ref, vwb_ref, vw2_ref, vpe_ref,
                     ko_ref, vo_ref):
    def mlp(g_ref, wa_ref, wb_ref, w2_ref, pe_ref):
        g = g_ref[0]
        ncp = g.shape[0]
        p0 = _dot(g, wa_ref[...])
        p1 = _dot(g, wb_ref[...])
        const = _dot(pe_ref[0], wa_ref[...]) + _dot(pe_ref[1], wb_ref[...])
        pre = p0 + pltpu.roll(p1, ncp - 1, 0) + const[0:1, :]
        return _dot(_silu(pre).astype(_BF16), w2_ref[...])

    ko_ref[0] = mlp(kc_ref, kwa_ref, kwb_ref, kw2_ref, kpe_ref).astype(_BF16)
    vo_ref[0] = mlp(vc_ref, vwa_ref, vwb_ref, vw2_ref, vpe_ref).T.astype(_BF16)


def _compress(kc, vc, kweights, vweights):
    batch, ncp, _ = kc.shape
    full = lambda a: pl.BlockSpec(a.shape, lambda b: (0,) * a.ndim)
    per_b = pl.BlockSpec((1, ncp, _GROUP_W), lambda b: (b, 0, 0))
    return pl.pallas_call(
        _compress_kernel,
        out_shape=(jax.ShapeDtypeStruct((batch, ncp, KV_W), _BF16),
                   jax.ShapeDtypeStruct((batch, KV_W, ncp), _BF16)),
        grid=(batch,),
        in_specs=[per_b, per_b] + [full(a) for a in kweights] + [full(a) for a in vweights],
        out_specs=(pl.BlockSpec((1, ncp, KV_W), lambda b: (b, 0, 0)),
                   pl.BlockSpec((1, KV_W, ncp), lambda b: (b, 0, 0))),
        compiler_params=pltpu.CompilerParams(dimension_semantics=("arbitrary",),
                                             vmem_limit_bytes=VMEM_LIMIT),
        name="compress",
    )(kc, vc, *kweights, *vweights)


def _overlap_t(n_sel, ncp, n_cmp):
    cs = CMP_STRIDE * np.arange(ncp)
    ss = SEL_BLOCK * np.arange(n_sel)
    ov = ((cs[None, :] < ss[:, None] + SEL_BLOCK) & (cs[None, :] + CMP_LEN > ss[:, None])
          & (np.arange(ncp)[None, :] < n_cmp))
    return jnp.asarray(ov.astype(np.float32), dtype=_BF16)


def _nsa_kernel(q_ref, ks_ref, kw_ref, vst_ref, vwt_ref, kc_ref, vct_ref, tab_ref, ovt_ref, smt_ref, nz_ref,
                o_ref, cb_ref, sc_ref, mb_ref):
    hk = pl.program_id(1)
    a = pl.program_id(2)
    ncp = kc_ref.shape[1]
    n_sel = ovt_ref.shape[0]
    hk_rows = pl.ds(pl.multiple_of(hk * NSA_DH, NSA_DH), NSA_DH)

    qf = q_ref[0].astype(_F32) * NSA_DH ** -0.5
    qt = qf.T
    qt4 = jnp.concatenate([qt[g * NSA_DH:(g + 1) * NSA_DH, :] for g in range(NSA_GROUP)], axis=1)
    qrow = lax.broadcasted_iota(jnp.int32, (KV_W, QL), 0)
    qt_b = jnp.where(qrow // NSA_DH == hk, jnp.concatenate([qt4] * NSA_KV_HEADS, axis=0), 0.0).astype(_BF16)

    far_row = tab_ref[0, TAB_FAR, 0:1, :]

    win_start = pl.multiple_of(jnp.maximum(8 * a - CMP_NEAR_BACK, 0), 8)
    tab_start = pl.multiple_of(win_start - (8 * a - CMP_NEAR_BACK), 8)
    crow = lax.broadcasted_iota(jnp.int32, (ncp, QL), 0)
    cb_ref[...] = jnp.where(crow < win_start, far_row, NEG)
    cb_ref[pl.ds(win_start, CMP_NEAR), :] = tab_ref[0, TAB_CMP, pl.ds(tab_start, CMP_NEAR), :]
    s_c = _dot(kc_ref[0], qt_b) + cb_ref[...]
    m_c = jnp.max(s_c, axis=0, keepdims=True)
    e_c = jnp.exp(s_c - m_c)
    l_c = jnp.sum(e_c, axis=0, keepdims=True)
    p_c = jnp.where(m_c > 0.5 * NEG, e_c * (1.0 / l_c), 0.0)
    o_c = _dot(vct_ref[0, hk_rows, :], p_c.astype(_BF16))

    psum = p_c[:, 0:TQ]
    for g in range(1, NSA_GROUP):
        psum = psum + p_c[:, g * TQ:(g + 1) * TQ]
    hi = psum.astype(_BF16)
    r1 = psum - hi.astype(_F32)
    mid = r1.astype(_BF16)
    lo = (r1 - mid.astype(_F32)).astype(_BF16)
    ovt = ovt_ref[...]
    imp = _dot(ovt, hi) + _dot(ovt, mid) + _dot(ovt, lo)
    jrow = lax.broadcasted_iota(jnp.int32, (n_sel, TQ), 0)
    cur = (a * TQ + lax.broadcasted_iota(jnp.int32, (n_sel, TQ), 1)) // SEL_BLOCK
    forced = (jrow == 0) | (jrow == cur) | (jrow == cur - 1)
    score = jnp.where(jrow <= cur, imp + jnp.where(forced, FORCE_BONUS, 0.0), NEG)
    sc_ref[...] = score

    def rank_body(jp, cnt):
        srow = sc_ref[pl.ds(jp, 1), :]
        tie = jnp.where(jrow > jp, 1.0, 0.0)
        return cnt + jnp.where(srow > score, 1.0, jnp.where(srow >= score, tie, 0.0))

    cnt = lax.fori_loop(0, (TQ // SEL_BLOCK) * (a + 1), rank_body, jnp.zeros((n_sel, TQ), _F32))
    mb = jnp.where(cnt < SEL_TOP_N, 0.0, NEG)
    mb_ref[...] = jnp.concatenate([mb] * NSA_GROUP, axis=1)

    def tile_update(k_ref, vt_ref, kt, bias, carry):
        m, l, acc = carry
        keys = pl.ds(pl.multiple_of(kt * TQ, TQ), TQ)
        s = _dot(k_ref[0, keys, :], qt_b) + bias
        m_new = jnp.maximum(m, jnp.max(s, axis=0, keepdims=True))
        alpha = jnp.exp(m - m_new)
        p = jnp.exp(s - m_new)
        l = alpha * l + jnp.sum(p, axis=0, keepdims=True)
        acc = alpha * acc + _dot(vt_ref[0, hk_rows, keys], p.astype(_BF16))
        return m_new, l, acc

    init = (jnp.full((1, QL), NEG, _F32), jnp.zeros((1, QL), _F32), jnp.zeros((NSA_DH, QL), _F32))

    def sel_body(kt, carry):
        e = jnp.minimum(a - kt, TAB_FAR)
        blk = (TQ // SEL_BLOCK) * kt
        rows = [jnp.broadcast_to(mb_ref[pl.ds(blk + i, 1), :], (SEL_BLOCK, QL)) for i in range(TQ // SEL_BLOCK)]
        bias = tab_ref[0, e] + jnp.concatenate(rows, axis=0)
        return tile_update(ks_ref, vst_ref, kt, bias, carry)

    _, l_s, acc_s = lax.fori_loop(0, a + 1, sel_body, init)

    def win_body(kt, carry):
        d = a - kt
        e = jnp.where(d == WINDOW // TQ, TAB_WIN_EDGE, d)
        return tile_update(kw_ref, vwt_ref, kt, tab_ref[0, e], carry)

    _, l_w, acc_w = lax.fori_loop(jnp.maximum(a - WINDOW // TQ, 0), a + 1, win_body, init)

    def gate(branch):
        grp = smt_ref[0, 8 + 8 * branch:16 + 8 * branch, :]
        sel = jnp.where(hk == 0, grp[0:NSA_GROUP, :], grp[NSA_GROUP:2 * NSA_GROUP, :])
        return _sigmoid(jnp.concatenate([sel[g:g + 1, :] for g in range(NSA_GROUP)], axis=1))

    o_t = gate(0) * o_c + gate(1) * (acc_s * (1.0 / l_s)) + gate(2) * (acc_w * (1.0 / l_w))
    stacked = jnp.concatenate([o_t[:, g * TQ:(g + 1) * TQ] for g in range(NSA_GROUP)], axis=0)
    o_ref[0] = stacked.T * _silu(nz_ref[0])


def _nsa(nq, ks, kw, vst, vwt, kcmp, vcmpt, tab, ovt, smt, nz):
    batch, seq, _ = nq.shape
    ncp = kcmp.shape[1]
    n_sel = seq // SEL_BLOCK
    per_b = lambda shape: pl.BlockSpec((1,) + shape, lambda b, h, a: (b, 0, 0))
    return pl.pallas_call(
        _nsa_kernel,
        out_shape=jax.ShapeDtypeStruct((batch, seq, NSA_W), _F32),
        grid=(batch, NSA_KV_HEADS, seq // TQ),
        in_specs=[
            pl.BlockSpec((1, TQ, GQ_W), lambda b, h, a: (b, a, h)),
            per_b((seq, KV_W)), per_b((seq, KV_W)), per_b((KV_W, seq)), per_b((KV_W, seq)),
            per_b((ncp, KV_W)), per_b((KV_W, ncp)),
            pl.BlockSpec((1, N_TAB, TQ, QL), lambda b, h, a: (h, 0, 0, 0)),
            pl.BlockSpec((n_sel, ncp), lambda b, h, a: (0, 0)),
            pl.BlockSpec((1, LANES, TQ), lambda b, h, a: (b, 0, a)),
            pl.BlockSpec((1, TQ, GQ_W), lambda b, h, a: (b, a, h)),
        ],
        out_specs=pl.BlockSpec((1, TQ, GQ_W), lambda b, h, a: (b, a, h)),
        scratch_shapes=[pltpu.VMEM((ncp, QL), _F32), pltpu.VMEM((n_sel, TQ), _F32),
                        pltpu.VMEM((n_sel, QL), _F32)],
        compiler_params=pltpu.CompilerParams(dimension_semantics=("arbitrary", "arbitrary", "arbitrary"),
                                             vmem_limit_bytes=VMEM_LIMIT),
        name="nsa",
    )(nq, ks, kw, vst, vwt, kcmp, vcmpt, tab, ovt, smt, nz)


def _out_proj_kernel(og_ref, on_ref, x_ref, w_ref, fw_ref, o_ref):
    h = (x_ref[...] + _dot(og_ref[...].astype(_BF16), w_ref[0:GDN_W, :])
         + _dot(on_ref[...].astype(_BF16), w_ref[GDN_W:, :]))
    o_ref[...] = h * lax.rsqrt(jnp.mean(h * h, axis=-1, keepdims=True) + EPS) * fw_ref[...]


def _out_proj(o_g, o_n, x2d, w_out, final_norm_w, tm):
    n = x2d.shape[0]
    row = lambda w: pl.BlockSpec((tm, w), lambda i: (i, 0))
    return pl.pallas_call(
        _out_proj_kernel,
        out_shape=jax.ShapeDtypeStruct((n, D_MODEL), _F32),
        grid=(n // tm,),
        in_specs=[row(GDN_W), row(NSA_W), row(D_MODEL),
                  pl.BlockSpec((GDN_W + NSA_W, D_MODEL), lambda i: (0, 0)),
                  pl.BlockSpec((1, D_MODEL), lambda i: (0, 0))],
        out_specs=row(D_MODEL),
        compiler_params=pltpu.CompilerParams(dimension_semantics=("arbitrary",),
                                             vmem_limit_bytes=VMEM_LIMIT),
        name="out_proj",
    )(o_g, o_n, x2d, w_out.astype(_BF16), final_norm_w.reshape(1, D_MODEL).astype(_F32))


def kernel(x, norm_w, w_in, conv_w, a_log, dt_bias, gdn_norm_w, cmp_pe_k, cmp_pe_v, cmp_k_w1, cmp_k_w2,
           cmp_v_w1, cmp_v_w2, w_out, rel_bias, final_norm_w):
    batch, seq, _ = x.shape
    assert w_in.shape[0] == 1, "single-layer problem"
    assert seq % (8 * TQ) == 0
    n = batch * seq
    ncp = seq // CMP_STRIDE
    n_cmp = (seq - CMP_LEN) // CMP_STRIDE + 1
    tm = min(512, seq)

    x2d = x.reshape(n, D_MODEL)
    tab = _bias_tables(rel_bias)
    (conv_in, gz, sm, smt, nq, kc, vc, ks, kw, vst, vwt, nz) = _in_proj(
        x2d, norm_w[0], _regroup_w_in(w_in[0]), batch, seq, tm)

    o_g = _gdn(conv_in.reshape(batch, seq, GDN_CONV_W), conv_w[0], sm.reshape(batch, seq, LANES), smt,
               a_log[0], dt_bias[0], gz.reshape(batch, seq, GDN_W), gdn_norm_w[0], tm)

    kcmp, vcmpt = _compress(kc.reshape(batch, ncp, _GROUP_W), vc.reshape(batch, ncp, _GROUP_W),
                            _compress_weights(cmp_pe_k[0], cmp_k_w1[0], cmp_k_w2[0]),
                            _compress_weights(cmp_pe_v[0], cmp_v_w1[0], cmp_v_w2[0]))

    o_n = _nsa(nq.reshape(batch, seq, NSA_W), ks.reshape(batch, seq, KV_W), kw.reshape(batch, seq, KV_W),
               vst, vwt, kcmp, vcmpt, tab, _overlap_t(seq // SEL_BLOCK, ncp, n_cmp), smt,
               nz.reshape(batch, seq, NSA_W))

    out = _out_proj(o_g.reshape(n, GDN_W), o_n.reshape(n, NSA_W), x2d, w_out[0], final_norm_w, tm)
    return out.reshape(batch, seq, D_MODEL)
```

```python
import functools
import math

import jax
import jax.numpy as jnp
import numpy as np
from jax import lax
from jax.experimental import pallas as pl
from jax.experimental.pallas import tpu as pltpu

D_MODEL = 1024
GDN_HEADS = 4
GDN_DK = 128
GDN_DV = 128
CONV_K = 4
CHUNK = 64
NSA_HEADS = 8
NSA_KV_HEADS = 2
NSA_GROUP = NSA_HEADS // NSA_KV_HEADS
NSA_DH = 64
CMP_LEN = 32
CMP_STRIDE = 16
CMP_HIDDEN = 128
SEL_BLOCK = 64
SEL_TOP_N = 16
WINDOW = 512
FORCE_BONUS = 1e4
N_BUCKETS = 32
MAX_DISTANCE = 1024
EPS = 1e-6
NEG = -1e30

GDN_QK_W = GDN_HEADS * GDN_DK
GDN_W = GDN_HEADS * GDN_DV
GDN_CONV_W = 2 * GDN_QK_W + GDN_W
NSA_W = NSA_HEADS * NSA_DH
KV_W = NSA_KV_HEADS * NSA_DH
PROJ_SIZES = (GDN_QK_W, GDN_QK_W, GDN_W, GDN_W, GDN_HEADS, GDN_HEADS,
              NSA_W, KV_W, KV_W, KV_W, KV_W, KV_W, KV_W, 3 * NSA_HEADS, NSA_W)

LANES = 128
TQ = 128
QL = NSA_GROUP * TQ
GQ_W = NSA_GROUP * NSA_DH
N_NEAR = 8
TAB_ZERO = N_NEAR
TAB_MASKED = N_NEAR + 1
TAB_WIN_EDGE = N_NEAR + 2
TAB_CMP = N_NEAR + 3
N_TAB = N_NEAR + 4
GROUP_TILES = 4
WIN_TILES = WINDOW // TQ + 1
VT_ROWS = 80
LOG2E = math.log2(math.e)
CMP_NEAR = 64
CMP_NEAR_BACK = 56
VMEM_LIMIT = 56 * 1024 * 1024

_F32 = jnp.float32
_BF16 = jnp.bfloat16


def _dot(a, b):
    return jnp.dot(a, b, preferred_element_type=_F32)


def _dot_nt(a, b):
    return lax.dot_general(a, b, (((1,), (1,)), ((), ())), preferred_element_type=_F32)


def _dot_tn(a, b):
    return lax.dot_general(a, b, (((0,), (0,)), ((), ())), preferred_element_type=_F32)


def _sigmoid(x):
    return 1.0 / (1.0 + jnp.exp(-x))


def _silu(x):
    return x * _sigmoid(x)


def _bias_tab_kernel(rel_ref, o_ref, far_ref):
    hk = pl.program_id(0)
    e = pl.program_id(1)
    is_cmp = e == TAB_CMP
    is_edge = e == TAB_WIN_EDGE
    base = jnp.where(e < N_NEAR, e * TQ,
                     jnp.where(is_edge, WINDOW, CMP_STRIDE * CMP_NEAR_BACK - (CMP_LEN - 1)))
    row_scale = jnp.where(is_cmp, CMP_STRIDE, 1)
    upper = jnp.where(is_edge, WINDOW, 1 << 30)
    row = lax.broadcasted_iota(jnp.int32, (TQ, TQ), 0)
    tok = lax.broadcasted_iota(jnp.int32, (TQ, TQ), 1)
    dist = base + tok - row_scale * row
    n = jnp.maximum(dist, 0)
    max_exact = N_BUCKETS // 2
    large = max_exact + (jnp.log(jnp.maximum(n, max_exact).astype(_F32) / max_exact)
                         / math.log(MAX_DISTANCE / max_exact) * (N_BUCKETS - max_exact)).astype(jnp.int32)
    bucket = jnp.where(n < max_exact, n, jnp.minimum(large, N_BUCKETS - 1))
    ok = (dist >= 0) & (dist < upper)
    for g in range(NSA_GROUP):
        head = hk * NSA_GROUP + g
        lanes = slice(g * TQ, (g + 1) * TQ)
        far = jnp.full((8, TQ), rel_ref[N_BUCKETS - 1, head] * LOG2E, _F32)
        far_hi = far.astype(_BF16).astype(_F32)
        far_lo = (far - far_hi).astype(_BF16).astype(_F32)
        val = jnp.zeros((TQ, TQ), _F32)
        for b in range(N_BUCKETS):
            val = jnp.where(bucket == b, rel_ref[b, head], val)
        val = val * LOG2E - jnp.where(is_cmp, 0.0, far_hi[0:1, :] + far_lo[0:1, :])
        val = jnp.where(ok, val, NEG)
        val = jnp.where(e == TAB_ZERO, 0.0, jnp.where(e == TAB_MASKED, NEG, val))
        o_ref[0, 0, :, lanes] = val

        sub = lax.broadcasted_iota(jnp.int32, (8, TQ), 0)
        far_ref[0, :, lanes] = jnp.where(sub == 0, far_hi, jnp.where(sub == 1, far_lo,
                                                                     jnp.where(sub == 2, far, 0.0)))


def _bias_tables(rel_bias):
    return pl.pallas_call(
        _bias_tab_kernel,
        out_shape=(jax.ShapeDtypeStruct((NSA_KV_HEADS, N_TAB, TQ, QL), _F32),
                   jax.ShapeDtypeStruct((NSA_KV_HEADS, 8, QL), _F32)),
        grid=(NSA_KV_HEADS, N_TAB),
        in_specs=[pl.BlockSpec(memory_space=pltpu.SMEM)],
        out_specs=(pl.BlockSpec((1, 1, TQ, QL), lambda h, e: (h, e, 0, 0)),
                   pl.BlockSpec((1, 8, QL), lambda h, e: (h, 0, 0))),
        name="bias_tables",
    )(rel_bias.astype(_F32))


_C_CONV = (0, GDN_CONV_W)
_C_GZ = (_C_CONV[1], _C_CONV[1] + GDN_W)
_C_SMALL = (_C_GZ[1], _C_GZ[1] + LANES)
_C_NQ = (_C_SMALL[1], _C_SMALL[1] + NSA_W)
_C_KC = (_C_NQ[1], _C_NQ[1] + KV_W)
_C_VC = (_C_KC[1], _C_KC[1] + KV_W)
_C_KS = (_C_VC[1], _C_VC[1] + KV_W)
_C_KW = (_C_KS[1], _C_KS[1] + KV_W)
_C_VS = (_C_KW[1], _C_KW[1] + KV_W)
_C_VW = (_C_VS[1], _C_VS[1] + KV_W)
_C_NZ = (_C_VW[1], _C_VW[1] + NSA_W)
_W_COLS = _C_NZ[1]


def _regroup_w_in(w):
    gq, gk, gv, gz, gb, ga, nq, kc, vc, ks, vs, kw, vw, ng, nz = jnp.split(
        w, [int(p) for p in np.cumsum(PROJ_SIZES)[:-1]], axis=-1)
    small = jnp.concatenate([gb, ga, ng], axis=-1)
    small = jnp.pad(small, ((0, 0), (0, LANES - small.shape[-1])))
    return jnp.concatenate([gq, gk, gv, gz, small, nq, kc, vc, ks, kw, vs, vw, nz], axis=-1).astype(_BF16)


def _in_proj_kernel(x_ref, nw_ref, w_ref, conv_ref, gz_ref, sm_ref, smt_ref, nq_ref, kc_ref, vc_ref,
                    ks_ref, kw_ref, vst_ref, vwt_ref, nz_ref):
    x = x_ref[...]
    u = x * lax.rsqrt(jnp.mean(x * x, axis=-1, keepdims=True) + EPS) * nw_ref[...]
    u = u.astype(_BF16)

    def proj(c):
        return _dot(u, w_ref[:, c[0]:c[1]])

    conv_ref[...] = proj(_C_CONV)
    gz_ref[...] = proj(_C_GZ)
    sm = proj(_C_SMALL)
    sm_ref[...] = sm
    smt_ref[0] = sm.T
    nq_ref[...] = proj(_C_NQ).astype(_BF16)
    kc_ref[...] = proj(_C_KC).astype(_BF16)
    vc_ref[...] = proj(_C_VC).astype(_BF16)
    ks_ref[...] = proj(_C_KS).astype(_BF16)
    kw_ref[...] = proj(_C_KW).astype(_BF16)
    tm = x.shape[0]
    ones_pad = (lax.broadcasted_iota(jnp.int32, (VT_ROWS - NSA_DH, tm), 0) == 0).astype(_F32)

    def values_t(c):
        vt = proj(c).T
        parts = []
        for h in range(NSA_KV_HEADS):
            parts += [vt[h * NSA_DH:(h + 1) * NSA_DH, :], ones_pad]
        return jnp.concatenate(parts, axis=0).astype(_BF16)

    vst_ref[0] = values_t(_C_VS)
    vwt_ref[0] = values_t(_C_VW)
    nz_ref[...] = proj(_C_NZ)


def _in_proj(x2d, norm_w, w_cat, batch, seq, tm):
    n = x2d.shape[0]
    tiles_per_seq = seq // tm
    row = lambda w: pl.BlockSpec((tm, w), lambda i: (i, 0))
    tr_map = lambda i: (i // tiles_per_seq, 0, i % tiles_per_seq)
    tr = pl.BlockSpec((1, LANES, tm), tr_map)
    trv = pl.BlockSpec((1, NSA_KV_HEADS * VT_ROWS, tm), tr_map)
    sds = jax.ShapeDtypeStruct
    return pl.pallas_call(
        _in_proj_kernel,
        out_shape=(
            sds((n, GDN_CONV_W), _F32), sds((n, GDN_W), _F32), sds((n, LANES), _F32),
            sds((batch, LANES, seq), _F32), sds((n, NSA_W), _BF16), sds((n, KV_W), _BF16),
            sds((n, KV_W), _BF16), sds((n, KV_W), _BF16), sds((n, KV_W), _BF16),
            sds((batch, NSA_KV_HEADS * VT_ROWS, seq), _BF16), sds((batch, NSA_KV_HEADS * VT_ROWS, seq), _BF16),
            sds((n, NSA_W), _F32)),
        grid=(n // tm,),
        in_specs=[row(D_MODEL), pl.BlockSpec((1, D_MODEL), lambda i: (0, 0)),
                  pl.BlockSpec((D_MODEL, _W_COLS), lambda i: (0, 0))],
        out_specs=(row(GDN_CONV_W), row(GDN_W), row(LANES), tr, row(NSA_W), row(KV_W), row(KV_W),
                   row(KV_W), row(KV_W), trv, trv, row(NSA_W)),
        compiler_params=pltpu.CompilerParams(dimension_semantics=("arbitrary",),
                                             vmem_limit_bytes=VMEM_LIMIT),
        name="in_proj",
    )(x2d, norm_w.reshape(1, D_MODEL).astype(_F32), w_cat)


_GDN_PAIR = 2 * CHUNK


def _segment_cumsum(g, axis):
    pos = lax.broadcasted_iota(jnp.int32, g.shape, axis) % CHUNK
    k = 1
    while k < CHUNK:
        g = g + jnp.where(pos >= k, pltpu.roll(g, k, axis), 0.0)
        k *= 2
    return g


def _unit_lower_inverse(low):
    r = lax.broadcasted_iota(jnp.int32, (CHUNK, CHUNK), 0)
    c = lax.broadcasted_iota(jnp.int32, (CHUNK, CHUNK), 1)
    eye = (r == c).astype(_F32)
    m = -low
    x = eye + m
    span = 2
    while span < CHUNK:
        m = _dot(m, m)
        x = x + _dot(x, m)
        span *= 2
    return x


def _gdn_kernel(x_ref, cw_ref, sm_ref, smt_ref, prow_ref, pcol_ref, gz_ref, gnw_ref, o_ref,
                xs_ref, q_ref, k_ref, v_ref, col_ref, grow_ref, state_ref, *, tile):
    t_idx = pl.program_id(1)

    @pl.when(t_idx == 0)
    def _():
        xs_ref[0:8, :] = jnp.zeros((8, GDN_CONV_W), _F32)
        state_ref[...] = jnp.zeros_like(state_ref)

    @pl.when(t_idx != 0)
    def _():
        xs_ref[0:8, :] = xs_ref[tile:tile + 8, :]

    xs_ref[8:tile + 8, :] = x_ref[0]

    y = jnp.zeros((tile, GDN_CONV_W), _F32)
    for tap in range(CONV_K):
        off = 8 - (CONV_K - 1) + tap
        y = y + cw_ref[tap:tap + 1, :] * xs_ref[off:off + tile, :]
    y = _silu(y)

    for h in range(GDN_HEADS):
        lo = h * GDN_DK
        qh = y[:, lo:lo + GDN_DK]
        kh = y[:, GDN_QK_W + lo:GDN_QK_W + lo + GDN_DK]
        q_ref[:, lo:lo + GDN_DK] = qh * lax.rsqrt(jnp.sum(qh * qh, axis=-1, keepdims=True) + EPS) * GDN_DK ** -0.5
        k_ref[:, lo:lo + GDN_DK] = kh * lax.rsqrt(jnp.sum(kh * kh, axis=-1, keepdims=True) + EPS)
    v_ref[...] = y[:, 2 * GDN_QK_W:]

    sm = sm_ref[0]
    lane = lax.broadcasted_iota(jnp.int32, sm.shape, 1)
    zc = sm + prow_ref[1:2, :]
    gcol = -jnp.exp(prow_ref[0:1, :]) * (jnp.maximum(zc, 0.0) + jnp.log1p(jnp.exp(-jnp.abs(zc))))
    gcol = _segment_cumsum(gcol, 0)
    col_ref[...] = jnp.where(lane < GDN_HEADS, _sigmoid(sm), gcol)
    zr = smt_ref[0, 0:8, :] + pcol_ref[1]
    grow = -jnp.exp(pcol_ref[0]) * (jnp.maximum(zr, 0.0) + jnp.log1p(jnp.exp(-jnp.abs(zr))))
    grow_ref[...] = _segment_cumsum(grow, 1)

    r = lax.broadcasted_iota(jnp.int32, (CHUNK, CHUNK), 0)
    c = lax.broadcasted_iota(jnp.int32, (CHUNK, CHUNK), 1)
    tril = r >= c
    strict = r > c
    gnw = gnw_ref[...]

    def pair_body(pi, carry):
        base = pl.multiple_of(pi * _GDN_PAIR, _GDN_PAIR)
        grow2 = grow_ref[:, pl.ds(base, _GDN_PAIR)]
        for half in range(2):
            rows = pl.ds(pl.multiple_of(base + half * CHUNK, CHUNK), CHUNK)
            colv = col_ref[rows, :]
            for h in range(GDN_HEADS):
                lo = h * GDN_DK
                q = q_ref[rows, lo:lo + GDN_DK]
                k = k_ref[rows, lo:lo + GDN_DK]
                v = v_ref[rows, lo:lo + GDN_DV]
                beta = colv[:, h:h + 1]
                gc = colv[:, GDN_HEADS + h:GDN_HEADS + h + 1]
                gr = grow2[GDN_HEADS + h:GDN_HEADS + h + 1, half * CHUNK:(half + 1) * CHUNK]
                decay = jnp.where(tril, jnp.exp(jnp.where(tril, gc - gr, 0.0)), 0.0)
                kb = k * beta
                low = jnp.where(strict, _dot_nt(kb, k) * decay, 0.0)
                tinv = _unit_lower_inverse(low)
                eg = jnp.exp(gc)
                sol = _dot(tinv, jnp.concatenate([v * beta, kb * eg], axis=1))
                u = sol[:, :GDN_DV]
                w = sol[:, GDN_DV:]
                attn = jnp.where(tril, _dot_nt(q, k) * decay, 0.0)
                g_last = gc[CHUNK - 1:CHUNK, :]
                k_dec = k * jnp.exp(g_last - gc)
                q_dec = q * eg
                st = state_ref[h]
                v_new = u - _dot(w, st)
                o = _dot(q_dec, st) + _dot(attn, v_new)
                state_ref[h] = st * jnp.exp(g_last) + _dot_tn(k_dec, v_new)
                o = o * lax.rsqrt(jnp.mean(o * o, axis=-1, keepdims=True) + EPS) * gnw
                o_ref[0, rows, lo:lo + GDN_DV] = o * _silu(gz_ref[0, rows, lo:lo + GDN_DV])
        return carry

    lax.fori_loop(0, tile // _GDN_PAIR, pair_body, 0)


def _gdn(conv_in, conv_w, sm, smt, a_log, dt_bias, gz, gdn_norm_w, tile):
    batch, seq, _ = conv_in.shape
    prow = jnp.zeros((2, LANES), _F32)
    prow = prow.at[0, GDN_HEADS:2 * GDN_HEADS].set(a_log.astype(_F32))
    prow = prow.at[1, GDN_HEADS:2 * GDN_HEADS].set(dt_bias.astype(_F32))
    pcol = jnp.broadcast_to(prow[:, :8, None], (2, 8, tile))
    kern = functools.partial(_gdn_kernel, tile=tile)
    return pl.pallas_call(
        kern,
        out_shape=jax.ShapeDtypeStruct((batch, seq, GDN_W), _F32),
        grid=(batch, seq // tile),
        in_specs=[
            pl.BlockSpec((1, tile, GDN_CONV_W), lambda b, t: (b, t, 0)),
            pl.BlockSpec((CONV_K, GDN_CONV_W), lambda b, t: (0, 0)),
            pl.BlockSpec((1, tile, LANES), lambda b, t: (b, t, 0)),
            pl.BlockSpec((1, LANES, tile), lambda b, t: (b, 0, t)),
            pl.BlockSpec((2, LANES), lambda b, t: (0, 0)),
            pl.BlockSpec((2, 8, tile), lambda b, t: (0, 0, 0)),
            pl.BlockSpec((1, tile, GDN_W), lambda b, t: (b, t, 0)),
            pl.BlockSpec((1, GDN_DV), lambda b, t: (0, 0)),
        ],
        out_specs=pl.BlockSpec((1, tile, GDN_W), lambda b, t: (b, t, 0)),
        scratch_shapes=[
            pltpu.VMEM((tile + 8, GDN_CONV_W), _F32),
            pltpu.VMEM((tile, GDN_QK_W), _F32),
            pltpu.VMEM((tile, GDN_QK_W), _F32),
            pltpu.VMEM((tile, GDN_W), _F32),
            pltpu.VMEM((tile, LANES), _F32),
            pltpu.VMEM((8, tile), _F32),
            pltpu.VMEM((GDN_HEADS, GDN_DK, GDN_DV), _F32),
        ],
        compiler_params=pltpu.CompilerParams(dimension_semantics=("arbitrary", "arbitrary"),
                                             vmem_limit_bytes=VMEM_LIMIT),
        name="gdn",
    )(conv_in, conv_w.astype(_F32), sm, smt, prow, pcol, gz, gdn_norm_w.reshape(1, GDN_DV).astype(_F32))


_GROUP_W = CMP_STRIDE * KV_W
_HID2 = NSA_KV_HEADS * CMP_HIDDEN


def _compress_weights(pe, w1, w2):
    eye = jnp.eye(NSA_KV_HEADS, dtype=_F32)
    halves = []
    pes = []
    for part in range(CMP_LEN // CMP_STRIDE):
        w1p = w1[part * CMP_STRIDE:(part + 1) * CMP_STRIDE].astype(_F32)
        big = jnp.einsum('ldf,hg->lhdgf', w1p, eye).reshape(_GROUP_W, _HID2)
        halves.append(big.astype(_BF16))
        pep = pe[part * CMP_STRIDE:(part + 1) * CMP_STRIDE].astype(_F32)
        pes.append(jnp.broadcast_to(pep[:, None, :], (CMP_STRIDE, NSA_KV_HEADS, NSA_DH)).reshape(1, _GROUP_W))
    w2big = jnp.einsum('fd,hg->hfgd', w2.astype(_F32), eye).reshape(_HID2, KV_W).astype(_BF16)
    pe2 = jnp.broadcast_to(jnp.stack(pes, 0), (2, 8, _GROUP_W)).astype(_BF16)
    return halves[0], halves[1], w2big, pe2


def _compress_kernel(kc_ref, vc_ref, kwa_ref, kwb_ref, kw2_ref, kpe_ref, vwa_ref, vwb_ref, vw2_ref, vpe_ref,
                     ko_ref, vo_ref):
    def mlp(g_ref, wa_ref, wb_ref, w2_ref, pe_ref):
        g = g_ref[0]
        ncp = g.shape[0]
        p0 = _dot(g, wa_ref[...])
        p1 = _dot(g, wb_ref[...])
        const = _dot(pe_ref[0], wa_ref[...]) + _dot(pe_ref[1], wb_ref[...])
        pre = p0 + pltpu.roll(p1, ncp - 1, 0) + const[0:1, :]
        return _dot(_silu(pre).astype(_BF16), w2_ref[...])

    ko_ref[0] = mlp(kc_ref, kwa_ref, kwb_ref, kw2_ref, kpe_ref).astype(_BF16)
    vo_ref[0] = mlp(vc_ref, vwa_ref, vwb_ref, vw2_ref, vpe_ref).T.astype(_BF16)


def _compress(kc, vc, kweights, vweights):
    batch, ncp, _ = kc.shape
    full = lambda a: pl.BlockSpec(a.shape, lambda b: (0,) * a.ndim)
    per_b = pl.BlockSpec((1, ncp, _GROUP_W), lambda b: (b, 0, 0))
    return pl.pallas_call(
        _compress_kernel,
        out_shape=(jax.ShapeDtypeStruct((batch, ncp, KV_W), _BF16),
                   jax.ShapeDtypeStruct((batch, KV_W, ncp), _BF16)),
        grid=(batch,),
        in_specs=[per_b, per_b] + [full(a) for a in kweights] + [full(a) for a in vweights],
        out_specs=(pl.BlockSpec((1, ncp, KV_W), lambda b: (b, 0, 0)),
                   pl.BlockSpec((1, KV_W, ncp), lambda b: (b, 0, 0))),
        compiler_params=pltpu.CompilerParams(dimension_semantics=("arbitrary",),
                                             vmem_limit_bytes=VMEM_LIMIT),
        name="compress",
    )(kc, vc, *kweights, *vweights)


def _overlap_t(n_sel, ncp, n_cmp):
    cs = CMP_STRIDE * np.arange(ncp)
    ss = SEL_BLOCK * np.arange(n_sel)
    ov = ((cs[None, :] < ss[:, None] + SEL_BLOCK) & (cs[None, :] + CMP_LEN > ss[:, None])
          & (np.arange(ncp)[None, :] < n_cmp))
    return jnp.asarray(ov.astype(np.float32), dtype=_BF16)


def _key_onehots(n_tiles, with_blocks):
    key = np.arange(n_tiles * TQ)[:, None]
    col = np.arange(LANES)[None, :]
    ek = np.broadcast_to((col == 8) | (col == 9), (n_tiles * TQ, LANES))
    if with_blocks:
        ek = ek | (col == key // SEL_BLOCK)
    return jnp.asarray(ek.astype(np.float32), dtype=_BF16)


def _nsa_kernel(q_ref, ks_ref, kw_ref, vst_ref, vwt_ref, kc_ref, vct_ref, tab_ref, far_ref, ovt_ref, eks_ref,
                ekw_ref, smt_ref, nz_ref, o_ref, cb_ref, sc_ref, mb_ref):
    hk = pl.program_id(1)
    a = pl.program_id(2)
    ncp = kc_ref.shape[1]
    n_sel = ovt_ref.shape[0]
    hk_rows = pl.ds(pl.multiple_of(hk * NSA_DH, NSA_DH), NSA_DH)
    hk_vrows = pl.ds(pl.multiple_of(hk * VT_ROWS, VT_ROWS), VT_ROWS)

    qf = q_ref[0].astype(_F32) * (NSA_DH ** -0.5 * LOG2E)
    qt = qf.T
    qt4 = jnp.concatenate([qt[g * NSA_DH:(g + 1) * NSA_DH, :] for g in range(NSA_GROUP)], axis=1)
    qrow = lax.broadcasted_iota(jnp.int32, (KV_W, QL), 0)
    qt_b = jnp.where(qrow // NSA_DH == hk, jnp.concatenate([qt4] * NSA_KV_HEADS, axis=0), 0.0).astype(_BF16)

    far_rows = far_ref[0]
    far_row = far_rows[2:3, :]
    zero_rows = jnp.zeros((LANES - 16, QL), _BF16)

    def augmented_queries(mask_rows):
        extra = jnp.concatenate([mask_rows, far_rows], axis=0).astype(_BF16)
        return jnp.concatenate([qt_b, extra, zero_rows], axis=0)

    win_start = pl.multiple_of(jnp.maximum(8 * a - CMP_NEAR_BACK, 0), 8)
    tab_start = pl.multiple_of(win_start - (8 * a - CMP_NEAR_BACK), 8)
    crow = lax.broadcasted_iota(jnp.int32, (ncp, QL), 0)
    cb_ref[...] = jnp.where(crow < win_start, far_row, NEG)
    cb_ref[pl.ds(win_start, CMP_NEAR), :] = tab_ref[0, TAB_CMP, pl.ds(tab_start, CMP_NEAR), :]
    s_c = _dot(kc_ref[0], qt_b) + cb_ref[...]
    m_c = jnp.max(s_c, axis=0, keepdims=True)
    e_c = jnp.exp2(s_c - m_c)
    l_c = jnp.sum(e_c, axis=0, keepdims=True)
    p_c = jnp.where(m_c > 0.5 * NEG, e_c * (1.0 / l_c), 0.0)
    o_c = _dot(vct_ref[0, hk_rows, :], p_c.astype(_BF16))

    psum = p_c[:, 0:TQ]
    for g in range(1, NSA_GROUP):
        psum = psum + p_c[:, g * TQ:(g + 1) * TQ]
    hi = psum.astype(_BF16)
    r1 = psum - hi.astype(_F32)
    mid = r1.astype(_BF16)
    lo = (r1 - mid.astype(_F32)).astype(_BF16)
    ovt = ovt_ref[...]
    imp = _dot(ovt, hi) + _dot(ovt, mid) + _dot(ovt, lo)
    jrow = lax.broadcasted_iota(jnp.int32, (n_sel, TQ), 0)
    cur = (a * TQ + lax.broadcasted_iota(jnp.int32, (n_sel, TQ), 1)) // SEL_BLOCK
    forced = (jrow == 0) | (jrow == cur) | (jrow == cur - 1)
    score = jnp.where(jrow <= cur, imp + jnp.where(forced, FORCE_BONUS, 0.0), NEG)
    sc_ref[...] = score

    def rank_body(jp, cnt):
        srow = sc_ref[pl.ds(jp, 1), :]
        tie = jnp.where(jrow > jp, 1.0, 0.0)
        return cnt + jnp.where(srow > score, 1.0, jnp.where(srow >= score, tie, 0.0))

    cnt = lax.fori_loop(0, (TQ // SEL_BLOCK) * (a + 1), rank_body, jnp.zeros((n_sel, TQ), _F32))
    mb = jnp.where(cnt < SEL_TOP_N, 0.0, NEG)
    mb_ref[...] = jnp.concatenate([mb] * NSA_GROUP, axis=1)

    def group_update(k_ref, vt_ref, ek, q_aug, key0, n_tiles, table_entries, carry):
        m, acc = carry
        keys = pl.ds(pl.multiple_of(key0, TQ), n_tiles * TQ)
        s = _dot(jnp.concatenate([k_ref[0, keys, :], ek], axis=1), q_aug)
        if table_entries is not None:
            s = jnp.concatenate([s[i * TQ:(i + 1) * TQ, :] + tab_ref[0, table_entries[i]]
                                 for i in range(n_tiles)], axis=0)
        m_new = jnp.maximum(m, jnp.max(s, axis=0, keepdims=True))
        alpha = jnp.exp2(m - m_new)
        p = jnp.exp2((s - m_new).astype(_BF16))
        acc = alpha * acc + _dot(vt_ref[0, hk_vrows, keys], p)
        return m_new, acc

    init = (jnp.full((1, QL), NEG, _F32), jnp.zeros((VT_ROWS, QL), _F32))
    eks = eks_ref[...]
    blocks_per_group = GROUP_TILES * (TQ // SEL_BLOCK)

    def sel_group(gi, carry, near):
        q_aug = augmented_queries(mb_ref[pl.ds(pl.multiple_of(gi * blocks_per_group, 8), blocks_per_group), :])
        entries = None
        if near:
            entries = []
            for i in range(GROUP_TILES):
                d = a - (gi * GROUP_TILES + i)
                entries.append(jnp.where(d < 0, TAB_MASKED, jnp.where(d >= N_NEAR, TAB_ZERO, d)))
        return group_update(ks_ref, vst_ref, eks, q_aug, gi * (GROUP_TILES * TQ), GROUP_TILES, entries, carry)

    last_group = a // GROUP_TILES
    n_far = jnp.maximum(last_group - (N_NEAR // GROUP_TILES), 0)
    carry = lax.fori_loop(0, n_far, functools.partial(sel_group, near=False), init)
    _, acc_s = lax.fori_loop(n_far, last_group + 1, functools.partial(sel_group, near=True), carry)

    win_t0 = jnp.maximum(a - (WIN_TILES - 1), 0)
    win_entries = []
    for i in range(WIN_TILES):
        d = a - (win_t0 + i)
        win_entries.append(jnp.where(d < 0, TAB_MASKED, jnp.where(d == WIN_TILES - 1, TAB_WIN_EDGE, d)))
    q_aug_w = augmented_queries(jnp.zeros((8, QL), _F32))
    _, acc_w = group_update(kw_ref, vwt_ref, ekw_ref[...], q_aug_w, win_t0 * TQ, WIN_TILES, win_entries, init)

    def gate(branch):
        grp = smt_ref[0, 8 + 8 * branch:16 + 8 * branch, :]
        sel = jnp.where(hk == 0, grp[0:NSA_GROUP, :], grp[NSA_GROUP:2 * NSA_GROUP, :])
        return _sigmoid(jnp.concatenate([sel[g:g + 1, :] for g in range(NSA_GROUP)], axis=1))

    def normalized(acc):
        return acc[0:NSA_DH, :] * (1.0 / acc[NSA_DH:NSA_DH + 1, :])

    o_t = gate(0) * o_c + gate(1) * normalized(acc_s) + gate(2) * normalized(acc_w)
    stacked = jnp.concatenate([o_t[:, g * TQ:(g + 1) * TQ] for g in range(NSA_GROUP)], axis=0)
    o_ref[0] = stacked.T * _silu(nz_ref[0])


def _nsa(nq, ks, kw, vst, vwt, kcmp, vcmpt, tab, far, ovt, smt, nz):
    batch, seq, _ = nq.shape
    ncp = kcmp.shape[1]
    n_sel = seq // SEL_BLOCK
    per_b = lambda shape: pl.BlockSpec((1,) + shape, lambda b, h, a: (b, 0, 0))
    const = lambda shape: pl.BlockSpec(shape, lambda b, h, a: (0,) * len(shape))
    vt_shape = (NSA_KV_HEADS * VT_ROWS, seq)
    return pl.pallas_call(
        _nsa_kernel,
        out_shape=jax.ShapeDtypeStruct((batch, seq, NSA_W), _F32),
        grid=(batch, NSA_KV_HEADS, seq // TQ),
        in_specs=[
            pl.BlockSpec((1, TQ, GQ_W), lambda b, h, a: (b, a, h)),
            per_b((seq, KV_W)), per_b((seq, KV_W)), per_b(vt_shape), per_b(vt_shape),
            per_b((ncp, KV_W)), per_b((KV_W, ncp)),
            pl.BlockSpec((1, N_TAB, TQ, QL), lambda b, h, a: (h, 0, 0, 0)),
            pl.BlockSpec((1, 8, QL), lambda b, h, a: (h, 0, 0)),
            const((n_sel, ncp)), const((GROUP_TILES * TQ, LANES)), const((WIN_TILES * TQ, LANES)),
            pl.BlockSpec((1, LANES, TQ), lambda b, h, a: (b, 0, a)),
            pl.BlockSpec((1, TQ, GQ_W), lambda b, h, a: (b, a, h)),
        ],
        out_specs=pl.BlockSpec((1, TQ, GQ_W), lambda b, h, a: (b, a, h)),
        scratch_shapes=[pltpu.VMEM((ncp, QL), _F32), pltpu.VMEM((n_sel, TQ), _F32),
                        pltpu.VMEM((n_sel, QL), _F32)],
        compiler_params=pltpu.CompilerParams(dimension_semantics=("arbitrary", "arbitrary", "arbitrary"),
                                             vmem_limit_bytes=VMEM_LIMIT),
        name="nsa",
    )(nq, ks, kw, vst, vwt, kcmp, vcmpt, tab, far, ovt, _key_onehots(GROUP_TILES, True),
      _key_onehots(WIN_TILES, False), smt, nz)


def _out_proj_kernel(og_ref, on_ref, x_ref, w_ref, fw_ref, o_ref):
    h = (x_ref[...] + _dot(og_ref[...].astype(_BF16), w_ref[0:GDN_W, :])
         + _dot(on_ref[...].astype(_BF16), w_ref[GDN_W:, :]))
    o_ref[...] = h * lax.rsqrt(jnp.mean(h * h, axis=-1, keepdims=True) + EPS) * fw_ref[...]


def _out_proj(o_g, o_n, x2d, w_out, final_norm_w, tm):
    n = x2d.shape[0]
    row = lambda w: pl.BlockSpec((tm, w), lambda i: (i, 0))
    return pl.pallas_call(
        _out_proj_kernel,
        out_shape=jax.ShapeDtypeStruct((n, D_MODEL), _F32),
        grid=(n // tm,),
        in_specs=[row(GDN_W), row(NSA_W), row(D_MODEL),
                  pl.BlockSpec((GDN_W + NSA_W, D_MODEL), lambda i: (0, 0)),
                  pl.BlockSpec((1, D_MODEL), lambda i: (0, 0))],
        out_specs=row(D_MODEL),
        compiler_params=pltpu.CompilerParams(dimension_semantics=("arbitrary",),
                                             vmem_limit_bytes=VMEM_LIMIT),
        name="out_proj",
    )(o_g, o_n, x2d, w_out.astype(_BF16), final_norm_w.reshape(1, D_MODEL).astype(_F32))


def kernel(x, norm_w, w_in, conv_w, a_log, dt_bias, gdn_norm_w, cmp_pe_k, cmp_pe_v, cmp_k_w1, cmp_k_w2,
           cmp_v_w1, cmp_v_w2, w_out, rel_bias, final_norm_w):
    batch, seq, _ = x.shape
    assert w_in.shape[0] == 1, "single-layer problem"
    assert seq % (8 * TQ) == 0
    n = batch * seq
    ncp = seq // CMP_STRIDE
    n_cmp = (seq - CMP_LEN) // CMP_STRIDE + 1
    tm = min(512, seq)

    x2d = x.reshape(n, D_MODEL)
    tab, far = _bias_tables(rel_bias)
    (conv_in, gz, sm, smt, nq, kc, vc, ks, kw, vst, vwt, nz) = _in_proj(
        x2d, norm_w[0], _regroup_w_in(w_in[0]), batch, seq, tm)

    o_g = _gdn(conv_in.reshape(batch, seq, GDN_CONV_W), conv_w[0], sm.reshape(batch, seq, LANES), smt,
               a_log[0], dt_bias[0], gz.reshape(batch, seq, GDN_W), gdn_norm_w[0], tm)

    kcmp, vcmpt = _compress(kc.reshape(batch, ncp, _GROUP_W), vc.reshape(batch, ncp, _GROUP_W),
                            _compress_weights(cmp_pe_k[0], cmp_k_w1[0], cmp_k_w2[0]),
                            _compress_weights(cmp_pe_v[0], cmp_v_w1[0], cmp_v_w2[0]))

    o_n = _nsa(nq.reshape(batch, seq, NSA_W), ks.reshape(batch, seq, KV_W), kw.reshape(batch, seq, KV_W),
               vst, vwt, kcmp, vcmpt, tab, far, _overlap_t(seq // SEL_BLOCK, ncp, n_cmp), smt,
               nz.reshape(batch, seq, NSA_W))

    out = _out_proj(o_g.reshape(n, GDN_W), o_n.reshape(n, NSA_W), x2d, w_out[0], final_norm_w, tm)
    return out.reshape(batch, seq, D_MODEL)
```

```python
import functools
import math

import jax
import jax.numpy as jnp
import numpy as np
from jax import lax
from jax.experimental import pallas as pl
from jax.experimental.pallas import tpu as pltpu

D_MODEL = 1024
GDN_HEADS = 4
GDN_DK = 128
GDN_DV = 128
CONV_K = 4
CHUNK = 64
NSA_HEADS = 8
NSA_KV_HEADS = 2
NSA_GROUP = NSA_HEADS // NSA_KV_HEADS
NSA_DH = 64
CMP_LEN = 32
CMP_STRIDE = 16
CMP_HIDDEN = 128
SEL_BLOCK = 64
SEL_TOP_N = 16
WINDOW = 512
FORCE_BONUS = 1e4
N_BUCKETS = 32
MAX_DISTANCE = 1024
EPS = 1e-6
NEG = -1e30

GDN_QK_W = GDN_HEADS * GDN_DK
GDN_W = GDN_HEADS * GDN_DV
GDN_CONV_W = 2 * GDN_QK_W + GDN_W
NSA_W = NSA_HEADS * NSA_DH
KV_W = NSA_KV_HEADS * NSA_DH
PROJ_SIZES = (GDN_QK_W, GDN_QK_W, GDN_W, GDN_W, GDN_HEADS, GDN_HEADS,
              NSA_W, KV_W, KV_W, KV_W, KV_W, KV_W, KV_W, 3 * NSA_HEADS, NSA_W)

LANES = 128
TQ = 128
QL = NSA_GROUP * TQ
GQ_W = NSA_GROUP * NSA_DH
N_NEAR = 8
TAB_ZERO = N_NEAR
TAB_MASKED = N_NEAR + 1
TAB_WIN_EDGE = N_NEAR + 2
TAB_CMP = N_NEAR + 3
N_TAB = N_NEAR + 4
GROUP_TILES = 4
WIN_TILES = WINDOW // TQ + 1
VT_ROWS = 80
LOG2E = math.log2(math.e)
CMP_NEAR = 64
CMP_NEAR_BACK = 56
VMEM_LIMIT = 56 * 1024 * 1024

_F32 = jnp.float32
_BF16 = jnp.bfloat16


def _dot(a, b):
    return jnp.dot(a, b, preferred_element_type=_F32)


def _dot_nt(a, b):
    return lax.dot_general(a, b, (((1,), (1,)), ((), ())), preferred_element_type=_F32)


def _dot_tn(a, b):
    return lax.dot_general(a, b, (((0,), (0,)), ((), ())), preferred_element_type=_F32)


def _sigmoid(x):
    return 1.0 / (1.0 + jnp.exp(-x))


def _silu(x):
    return x * _sigmoid(x)


def _bias_tab_kernel(rel_ref, o_ref, far_ref):
    hk = pl.program_id(0)
    e = pl.program_id(1)
    is_cmp = e == TAB_CMP
    is_edge = e == TAB_WIN_EDGE
    base = jnp.where(e < N_NEAR, e * TQ,
                     jnp.where(is_edge, WINDOW, CMP_STRIDE * CMP_NEAR_BACK - (CMP_LEN - 1)))
    row_scale = jnp.where(is_cmp, CMP_STRIDE, 1)
    upper = jnp.where(is_edge, WINDOW, 1 << 30)
    row = lax.broadcasted_iota(jnp.int32, (TQ, TQ), 0)
    tok = lax.broadcasted_iota(jnp.int32, (TQ, TQ), 1)
    dist = base + tok - row_scale * row
    n = jnp.maximum(dist, 0)
    max_exact = N_BUCKETS // 2
    large = max_exact + (jnp.log(jnp.maximum(n, max_exact).astype(_F32) / max_exact)
                         / math.log(MAX_DISTANCE / max_exact) * (N_BUCKETS - max_exact)).astype(jnp.int32)
    bucket = jnp.where(n < max_exact, n, jnp.minimum(large, N_BUCKETS - 1))
    ok = (dist >= 0) & (dist < upper)
    for g in range(NSA_GROUP):
        head = hk * NSA_GROUP + g
        lanes = slice(g * TQ, (g + 1) * TQ)
        far = jnp.full((8, TQ), rel_ref[N_BUCKETS - 1, head] * LOG2E, _F32)
        far_hi = far.astype(_BF16).astype(_F32)
        far_lo = (far - far_hi).astype(_BF16).astype(_F32)
        val = jnp.zeros((TQ, TQ), _F32)
        for b in range(N_BUCKETS):
            val = jnp.where(bucket == b, rel_ref[b, head], val)
        val = val * LOG2E - jnp.where(is_cmp, 0.0, far_hi[0:1, :] + far_lo[0:1, :])
        val = jnp.where(ok, val, NEG)
        val = jnp.where(e == TAB_ZERO, 0.0, jnp.where(e == TAB_MASKED, NEG, val))
        o_ref[0, 0, :, lanes] = val

        sub = lax.broadcasted_iota(jnp.int32, (8, TQ), 0)
        far_ref[0, :, lanes] = jnp.where(sub == 0, far_hi, jnp.where(sub == 1, far_lo,
                                                                     jnp.where(sub == 2, far, 0.0)))


def _bias_tables(rel_bias):
    return pl.pallas_call(
        _bias_tab_kernel,
        out_shape=(jax.ShapeDtypeStruct((NSA_KV_HEADS, N_TAB, TQ, QL), _F32),
                   jax.ShapeDtypeStruct((NSA_KV_HEADS, 8, QL), _F32)),
        grid=(NSA_KV_HEADS, N_TAB),
        in_specs=[pl.BlockSpec(memory_space=pltpu.SMEM)],
        out_specs=(pl.BlockSpec((1, 1, TQ, QL), lambda h, e: (h, e, 0, 0)),
                   pl.BlockSpec((1, 8, QL), lambda h, e: (h, 0, 0))),
        name="bias_tables",
    )(rel_bias.astype(_F32))


_C_CONV = (0, GDN_CONV_W)
_C_GZ = (_C_CONV[1], _C_CONV[1] + GDN_W)
_C_SMALL = (_C_GZ[1], _C_GZ[1] + LANES)
_C_NQ = (_C_SMALL[1], _C_SMALL[1] + NSA_W)
_C_KC = (_C_NQ[1], _C_NQ[1] + KV_W)
_C_VC = (_C_KC[1], _C_KC[1] + KV_W)
_C_KS = (_C_VC[1], _C_VC[1] + KV_W)
_C_KW = (_C_KS[1], _C_KS[1] + KV_W)
_C_VS = (_C_KW[1], _C_KW[1] + KV_W)
_C_VW = (_C_VS[1], _C_VS[1] + KV_W)
_C_NZ = (_C_VW[1], _C_VW[1] + NSA_W)
_W_COLS = _C_NZ[1]


def _regroup_w_in(w):
    gq, gk, gv, gz, gb, ga, nq, kc, vc, ks, vs, kw, vw, ng, nz = jnp.split(
        w, [int(p) for p in np.cumsum(PROJ_SIZES)[:-1]], axis=-1)
    small = jnp.concatenate([gb, ga, ng], axis=-1)
    small = jnp.pad(small, ((0, 0), (0, LANES - small.shape[-1])))
    return jnp.concatenate([gq, gk, gv, gz, small, nq, kc, vc, ks, kw, vs, vw, nz], axis=-1).astype(_BF16)


def _in_proj_kernel(x_ref, nw_ref, w_ref, conv_ref, gz_ref, sm_ref, smt_ref, nq_ref, kc_ref, vc_ref,
                    ks_ref, kw_ref, vst_ref, vwt_ref, nz_ref):
    x = x_ref[...]
    u = x * lax.rsqrt(jnp.mean(x * x, axis=-1, keepdims=True) + EPS) * nw_ref[...]
    u = u.astype(_BF16)

    def proj(c):
        return _dot(u, w_ref[:, c[0]:c[1]])

    conv_ref[...] = proj(_C_CONV)
    gz_ref[...] = proj(_C_GZ)
    sm = proj(_C_SMALL)
    sm_ref[...] = sm
    smt_ref[0] = sm.T
    nq_ref[...] = proj(_C_NQ).astype(_BF16)
    kc_ref[...] = proj(_C_KC).astype(_BF16)
    vc_ref[...] = proj(_C_VC).astype(_BF16)
    ks_ref[...] = proj(_C_KS).astype(_BF16)
    kw_ref[...] = proj(_C_KW).astype(_BF16)
    tm = x.shape[0]
    ones_pad = (lax.broadcasted_iota(jnp.int32, (VT_ROWS - NSA_DH, tm), 0) == 0).astype(_F32)

    def values_t(c):
        vt = proj(c).T
        parts = []
        for h in range(NSA_KV_HEADS):
            parts += [vt[h * NSA_DH:(h + 1) * NSA_DH, :], ones_pad]
        return jnp.concatenate(parts, axis=0).astype(_BF16)

    vst_ref[0] = values_t(_C_VS)
    vwt_ref[0] = values_t(_C_VW)
    nz_ref[...] = proj(_C_NZ)


def _in_proj(x2d, norm_w, w_cat, batch, seq, tm):
    n = x2d.shape[0]
    tiles_per_seq = seq // tm
    row = lambda w: pl.BlockSpec((tm, w), lambda i: (i, 0))
    tr_map = lambda i: (i // tiles_per_seq, 0, i % tiles_per_seq)
    tr = pl.BlockSpec((1, LANES, tm), tr_map)
    trv = pl.BlockSpec((1, NSA_KV_HEADS * VT_ROWS, tm), tr_map)
    sds = jax.ShapeDtypeStruct
    return pl.pallas_call(
        _in_proj_kernel,
        out_shape=(
            sds((n, GDN_CONV_W), _F32), sds((n, GDN_W), _F32), sds((n, LANES), _F32),
            sds((batch, LANES, seq), _F32), sds((n, NSA_W), _BF16), sds((n, KV_W), _BF16),
            sds((n, KV_W), _BF16), sds((n, KV_W), _BF16), sds((n, KV_W), _BF16),
            sds((batch, NSA_KV_HEADS * VT_ROWS, seq), _BF16), sds((batch, NSA_KV_HEADS * VT_ROWS, seq), _BF16),
            sds((n, NSA_W), _F32)),
        grid=(n // tm,),
        in_specs=[row(D_MODEL), pl.BlockSpec((1, D_MODEL), lambda i: (0, 0)),
                  pl.BlockSpec((D_MODEL, _W_COLS), lambda i: (0, 0))],
        out_specs=(row(GDN_CONV_W), row(GDN_W), row(LANES), tr, row(NSA_W), row(KV_W), row(KV_W),
                   row(KV_W), row(KV_W), trv, trv, row(NSA_W)),
        compiler_params=pltpu.CompilerParams(dimension_semantics=("arbitrary",),
                                             vmem_limit_bytes=VMEM_LIMIT),
        name="in_proj",
    )(x2d, norm_w.reshape(1, D_MODEL).astype(_F32), w_cat)


_GDN_PAIR = 2 * CHUNK
_STACK = GDN_HEADS * CHUNK
_PREP_UNROLL = 4


def _segment_cumsum(g, axis):
    pos = lax.broadcasted_iota(jnp.int32, g.shape, axis) % CHUNK
    k = 1
    while k < CHUNK:
        g = g + jnp.where(pos >= k, pltpu.roll(g, k, axis), 0.0)
        k *= 2
    return g


def _stacked_unit_lower_inverses(lows):
    r = lax.broadcasted_iota(jnp.int32, (_STACK, _STACK), 0)
    c = lax.broadcasted_iota(jnp.int32, (_STACK, _STACK), 1)
    eye = (r == c).astype(_F32)
    ms = [-low for low in lows]
    xs = [eye + m for m in ms]
    ms = [_dot(m, m) for m in ms]
    span = 4
    while span < CHUNK:
        boths = [_dot(jnp.concatenate([x, m], axis=0), m) for x, m in zip(xs, ms)]
        xs = [x + both[:_STACK] for x, both in zip(xs, boths)]
        ms = [both[_STACK:] for both in boths]
        span *= 2
    return [x + _dot(x, m) for x, m in zip(xs, ms)]


def _gdn_kernel(x_ref, cw_ref, sm_ref, smt_ref, prow_ref, pcol_ref, gz_ref, gnw_ref, o_ref,
                xs_ref, q_ref, k_ref, v_ref, col_ref, grow_ref, state_ref,
                u_ref, w_ref, qd_ref, kd_ref, a_ref, egl_ref, oraw_ref, *, tile):
    t_idx = pl.program_id(1)

    @pl.when(t_idx == 0)
    def _():
        xs_ref[0:8, :] = jnp.zeros((8, GDN_CONV_W), _F32)
        state_ref[...] = jnp.zeros_like(state_ref)

    @pl.when(t_idx != 0)
    def _():
        xs_ref[0:8, :] = xs_ref[tile:tile + 8, :]

    xs_ref[8:tile + 8, :] = x_ref[0]

    y = jnp.zeros((tile, GDN_CONV_W), _F32)
    for tap in range(CONV_K):
        off = 8 - (CONV_K - 1) + tap
        y = y + cw_ref[tap:tap + 1, :] * xs_ref[off:off + tile, :]
    y = _silu(y)

    for h in range(GDN_HEADS):
        lo = h * GDN_DK
        qh = y[:, lo:lo + GDN_DK]
        kh = y[:, GDN_QK_W + lo:GDN_QK_W + lo + GDN_DK]
        q_ref[:, lo:lo + GDN_DK] = qh * lax.rsqrt(jnp.sum(qh * qh, axis=-1, keepdims=True) + EPS) * GDN_DK ** -0.5
        k_ref[:, lo:lo + GDN_DK] = kh * lax.rsqrt(jnp.sum(kh * kh, axis=-1, keepdims=True) + EPS)
    v_ref[...] = y[:, 2 * GDN_QK_W:]

    sm = sm_ref[0]
    lane = lax.broadcasted_iota(jnp.int32, sm.shape, 1)
    zc = sm + prow_ref[1:2, :]
    gcol = -jnp.exp(prow_ref[0:1, :]) * (jnp.maximum(zc, 0.0) + jnp.log1p(jnp.exp(-jnp.abs(zc))))
    gcol = _segment_cumsum(gcol, 0)
    col_ref[...] = jnp.where(lane < GDN_HEADS, _sigmoid(sm), gcol)
    zr = smt_ref[0, 0:8, :] + pcol_ref[1]
    grow = -jnp.exp(pcol_ref[0]) * (jnp.maximum(zr, 0.0) + jnp.log1p(jnp.exp(-jnp.abs(zr))))
    grow_ref[...] = _segment_cumsum(grow, 1)

    r = lax.broadcasted_iota(jnp.int32, (_STACK, _STACK), 0)
    c = lax.broadcasted_iota(jnp.int32, (_STACK, _STACK), 1)
    same_head = (r // CHUNK) == (c // CHUNK)
    tril = same_head & (r >= c)
    strict = same_head & (r > c)
    heads = range(GDN_HEADS)

    def prep_gram(ci, half):
        rows = pl.ds(pl.multiple_of(ci * CHUNK, CHUNK), CHUNK)
        slab = pl.ds(pl.multiple_of((ci // 2) * _GDN_PAIR, _GDN_PAIR), _GDN_PAIR)
        colv = col_ref[rows, :]
        grow2 = grow_ref[:, slab]

        def stack(ref, width):
            return jnp.concatenate([ref[rows, h * width:(h + 1) * width] for h in heads], axis=0)

        q = stack(q_ref, GDN_DK)
        k = stack(k_ref, GDN_DK)
        v = stack(v_ref, GDN_DV)
        beta = jnp.concatenate([colv[:, h:h + 1] for h in heads], axis=0)
        gc = jnp.concatenate([colv[:, GDN_HEADS + h:GDN_HEADS + h + 1] for h in heads], axis=0)
        gr = jnp.concatenate([grow2[GDN_HEADS + h:GDN_HEADS + h + 1, half * CHUNK:(half + 1) * CHUNK]
                              for h in heads], axis=1)
        last = [colv[CHUNK - 1:CHUNK, GDN_HEADS + h:GDN_HEADS + h + 1] for h in heads]
        g_last = jnp.concatenate([jnp.broadcast_to(x, (CHUNK, 1)) for x in last], axis=0)
        decay = jnp.where(tril, jnp.exp(jnp.where(tril, gc - gr, 0.0)), 0.0)
        kb = k * beta
        eg = jnp.exp(gc)
        low = jnp.where(strict, _dot_nt(kb, k) * decay, 0.0)
        a_ref[ci] = jnp.where(tril, _dot_nt(q, k) * decay, 0.0)
        kd_ref[ci] = k * jnp.exp(g_last - gc)
        qd_ref[ci] = q * eg
        egl_ref[ci] = jnp.concatenate([jnp.broadcast_to(jnp.exp(x), (1, GDN_DV)) for x in last]
                                      + [jnp.zeros((8 - GDN_HEADS, GDN_DV), _F32)], axis=0)
        return low, jnp.concatenate([v * beta, kb * eg], axis=1)

    def prep_body(pi, carry):
        cis = [pi * _PREP_UNROLL + j for j in range(_PREP_UNROLL)]
        lows, rhss = zip(*[prep_gram(ci, j % 2) for j, ci in enumerate(cis)])
        tinvs = _stacked_unit_lower_inverses(lows)
        sols = [_dot(tinv, rhs) for tinv, rhs in zip(tinvs, rhss)]
        for ci, sol in zip(cis, sols):
            u_ref[ci] = sol[:, :GDN_DV]
            w_ref[ci] = sol[:, GDN_DV:]
        return carry

    n_chunks = tile // CHUNK
    lax.fori_loop(0, n_chunks // _PREP_UNROLL, prep_body, 0)

    states = [state_ref[h] for h in heads]
    for ci in range(n_chunks):
        u, w, qd, kd, egl = u_ref[ci], w_ref[ci], qd_ref[ci], kd_ref[ci], egl_ref[ci]
        v_news, inters = [], []
        for h in heads:
            rs = slice(h * CHUNK, (h + 1) * CHUNK)
            wq = _dot(jnp.concatenate([w[rs], qd[rs]], axis=0), states[h])
            v_new = u[rs] - wq[:CHUNK]
            states[h] = states[h] * egl[h:h + 1, :] + _dot_tn(kd[rs], v_new)
            v_news.append(v_new)
            inters.append(wq[CHUNK:])
        o = jnp.concatenate(inters, axis=0) + _dot(a_ref[ci], jnp.concatenate(v_news, axis=0))
        for h in heads:
            oraw_ref[ci * CHUNK:(ci + 1) * CHUNK, h * GDN_DV:(h + 1) * GDN_DV] = o[h * CHUNK:(h + 1) * CHUNK]
    for h in heads:
        state_ref[h] = states[h]

    gnw = gnw_ref[...]
    for h in heads:
        lanes = slice(h * GDN_DV, (h + 1) * GDN_DV)
        o = oraw_ref[:, lanes]
        o = o * lax.rsqrt(jnp.mean(o * o, axis=-1, keepdims=True) + EPS) * gnw
        o_ref[0, :, lanes] = o * _silu(gz_ref[0, :, lanes])


def _gdn(conv_in, conv_w, sm, smt, a_log, dt_bias, gz, gdn_norm_w, tile):
    batch, seq, _ = conv_in.shape
    prow = jnp.zeros((2, LANES), _F32)
    prow = prow.at[0, GDN_HEADS:2 * GDN_HEADS].set(a_log.astype(_F32))
    prow = prow.at[1, GDN_HEADS:2 * GDN_HEADS].set(dt_bias.astype(_F32))
    pcol = jnp.broadcast_to(prow[:, :8, None], (2, 8, tile))
    n_chunks = tile // CHUNK
    assert n_chunks % _PREP_UNROLL == 0 and _PREP_UNROLL % 2 == 0
    kern = functools.partial(_gdn_kernel, tile=tile)
    return pl.pallas_call(
        kern,
        out_shape=jax.ShapeDtypeStruct((batch, seq, GDN_W), _F32),
        grid=(batch, seq // tile),
        in_specs=[
            pl.BlockSpec((1, tile, GDN_CONV_W), lambda b, t: (b, t, 0)),
            pl.BlockSpec((CONV_K, GDN_CONV_W), lambda b, t: (0, 0)),
            pl.BlockSpec((1, tile, LANES), lambda b, t: (b, t, 0)),
            pl.BlockSpec((1, LANES, tile), lambda b, t: (b, 0, t)),
            pl.BlockSpec((2, LANES), lambda b, t: (0, 0)),
            pl.BlockSpec((2, 8, tile), lambda b, t: (0, 0, 0)),
            pl.BlockSpec((1, tile, GDN_W), lambda b, t: (b, t, 0)),
            pl.BlockSpec((1, GDN_DV), lambda b, t: (0, 0)),
        ],
        out_specs=pl.BlockSpec((1, tile, GDN_W), lambda b, t: (b, t, 0)),
        scratch_shapes=[
            pltpu.VMEM((tile + 8, GDN_CONV_W), _F32),
            pltpu.VMEM((tile, GDN_QK_W), _F32),
            pltpu.VMEM((tile, GDN_QK_W), _F32),
            pltpu.VMEM((tile, GDN_W), _F32),
            pltpu.VMEM((tile, LANES), _F32),
            pltpu.VMEM((8, tile), _F32),
            pltpu.VMEM((GDN_HEADS, GDN_DK, GDN_DV), _F32),
            pltpu.VMEM((n_chunks, _STACK, GDN_DV), _F32),
            pltpu.VMEM((n_chunks, _STACK, GDN_DV), _F32),
            pltpu.VMEM((n_chunks, _STACK, GDN_DK), _F32),
            pltpu.VMEM((n_chunks, _STACK, GDN_DK), _F32),
            pltpu.VMEM((n_chunks, _STACK, _STACK), _F32),
            pltpu.VMEM((n_chunks, 8, GDN_DV), _F32),
            pltpu.VMEM((tile, GDN_W), _F32),
        ],
        compiler_params=pltpu.CompilerParams(dimension_semantics=("arbitrary", "arbitrary"),
                                             vmem_limit_bytes=VMEM_LIMIT),
        name="gdn",
    )(conv_in, conv_w.astype(_F32), sm, smt, prow, pcol, gz, gdn_norm_w.reshape(1, GDN_DV).astype(_F32))


_GROUP_W = CMP_STRIDE * KV_W
_HID2 = NSA_KV_HEADS * CMP_HIDDEN


def _compress_weights(pe, w1, w2):
    eye = jnp.eye(NSA_KV_HEADS, dtype=_F32)
    halves = []
    pes = []
    for part in range(CMP_LEN // CMP_STRIDE):
        w1p = w1[part * CMP_STRIDE:(part + 1) * CMP_STRIDE].astype(_F32)
        big = jnp.einsum('ldf,hg->lhdgf', w1p, eye).reshape(_GROUP_W, _HID2)
        halves.append(big.astype(_BF16))
        pep = pe[part * CMP_STRIDE:(part + 1) * CMP_STRIDE].astype(_F32)
        pes.append(jnp.broadcast_to(pep[:, None, :], (CMP_STRIDE, NSA_KV_HEADS, NSA_DH)).reshape(1, _GROUP_W))
    w2big = jnp.einsum('fd,hg->hfgd', w2.astype(_F32), eye).reshape(_HID2, KV_W).astype(_BF16)
    pe2 = jnp.broadcast_to(jnp.stack(pes, 0), (2, 8, _GROUP_W)).astype(_BF16)
    return halves[0], halves[1], w2big, pe2


def _compress_kernel(kc_ref, vc_ref, kwa_ref, kwb_ref, kw2_ref, kpe_ref, vwa_ref, vwb_ref, vw2_ref, vpe_ref,
                     ko_ref, vo_ref):
    def mlp(g_ref, wa_ref, wb_ref, w2_ref, pe_ref):
        g = g_ref[0]
        ncp = g.shape[0]
        p0 = _dot(g, wa_ref[...])
        p1 = _dot(g, wb_ref[...])
        const = _dot(pe_ref[0], wa_ref[...]) + _dot(pe_ref[1], wb_ref[...])
        pre = p0 + pltpu.roll(p1, ncp - 1, 0) + const[0:1, :]
        return _dot(_silu(pre).astype(_BF16), w2_ref[...])

    ko_ref[0] = mlp(kc_ref, kwa_ref, kwb_ref, kw2_ref, kpe_ref).astype(_BF16)
    vo_ref[0] = mlp(vc_ref, vwa_ref, vwb_ref, vw2_ref, vpe_ref).T.astype(_BF16)


def _compress(kc, vc, kweights, vweights):
    batch, ncp, _ = kc.shape
    full = lambda a: pl.BlockSpec(a.shape, lambda b: (0,) * a.ndim)
    per_b = pl.BlockSpec((1, ncp, _GROUP_W), lambda b: (b, 0, 0))
    return pl.pallas_call(
        _compress_kernel,
        out_shape=(jax.ShapeDtypeStruct((batch, ncp, KV_W), _BF16),
                   jax.ShapeDtypeStruct((batch, KV_W, ncp), _BF16)),
        grid=(batch,),
        in_specs=[per_b, per_b] + [full(a) for a in kweights] + [full(a) for a in vweights],
        out_specs=(pl.BlockSpec((1, ncp, KV_W), lambda b: (b, 0, 0)),
                   pl.BlockSpec((1, KV_W, ncp), lambda b: (b, 0, 0))),
        compiler_params=pltpu.CompilerParams(dimension_semantics=("arbitrary",),
                                             vmem_limit_bytes=VMEM_LIMIT),
        name="compress",
    )(kc, vc, *kweights, *vweights)


def _overlap_t(n_sel, ncp, n_cmp):
    cs = CMP_STRIDE * np.arange(ncp)
    ss = SEL_BLOCK * np.arange(n_sel)
    ov = ((cs[None, :] < ss[:, None] + SEL_BLOCK) & (cs[None, :] + CMP_LEN > ss[:, None])
          & (np.arange(ncp)[None, :] < n_cmp))
    return jnp.asarray(ov.astype(np.float32), dtype=_BF16)


def _key_onehots(n_tiles, with_blocks):
    key = np.arange(n_tiles * TQ)[:, None]
    col = np.arange(LANES)[None, :]
    ek = np.broadcast_to((col == 8) | (col == 9), (n_tiles * TQ, LANES))
    if with_blocks:
        ek = ek | (col == key // SEL_BLOCK)
    return jnp.asarray(ek.astype(np.float32), dtype=_BF16)


def _nsa_kernel(q_ref, ks_ref, kw_ref, vst_ref, vwt_ref, kc_ref, vct_ref, tab_ref, far_ref, ovt_ref, eks_ref,
                ekw_ref, smt_ref, nz_ref, o_ref, cb_ref, sc_ref, mb_ref, ss_ref):
    hk = pl.program_id(1)
    a = pl.program_id(2)
    ncp = kc_ref.shape[1]
    n_sel = ovt_ref.shape[0]
    hk_rows = pl.ds(pl.multiple_of(hk * NSA_DH, NSA_DH), NSA_DH)
    hk_vrows = pl.ds(pl.multiple_of(hk * VT_ROWS, VT_ROWS), VT_ROWS)

    qf = q_ref[0].astype(_F32) * (NSA_DH ** -0.5 * LOG2E)
    qt = qf.T
    qt4 = jnp.concatenate([qt[g * NSA_DH:(g + 1) * NSA_DH, :] for g in range(NSA_GROUP)], axis=1)
    qrow = lax.broadcasted_iota(jnp.int32, (KV_W, QL), 0)
    qt_b = jnp.where(qrow // NSA_DH == hk, jnp.concatenate([qt4] * NSA_KV_HEADS, axis=0), 0.0).astype(_BF16)

    far_rows = far_ref[0]
    far_row = far_rows[2:3, :]
    zero_rows = jnp.zeros((LANES - 16, QL), _BF16)

    def augmented_queries(mask_rows):
        extra = jnp.concatenate([mask_rows, far_rows], axis=0).astype(_BF16)
        return jnp.concatenate([qt_b, extra, zero_rows], axis=0)

    win_start = pl.multiple_of(jnp.maximum(8 * a - CMP_NEAR_BACK, 0), 8)
    tab_start = pl.multiple_of(win_start - (8 * a - CMP_NEAR_BACK), 8)
    crow = lax.broadcasted_iota(jnp.int32, (ncp, QL), 0)
    cb_ref[...] = jnp.where(crow < win_start, far_row, NEG)
    cb_ref[pl.ds(win_start, CMP_NEAR), :] = tab_ref[0, TAB_CMP, pl.ds(tab_start, CMP_NEAR), :]
    s_c = _dot(kc_ref[0], qt_b) + cb_ref[...]
    m_c = jnp.max(s_c, axis=0, keepdims=True)
    e_c = jnp.exp2(s_c - m_c)
    l_c = jnp.sum(e_c, axis=0, keepdims=True)
    p_c = jnp.where(m_c > 0.5 * NEG, e_c * (1.0 / l_c), 0.0)
    o_c = _dot(vct_ref[0, hk_rows, :], p_c.astype(_BF16))

    psum = p_c[:, 0:TQ]
    for g in range(1, NSA_GROUP):
        psum = psum + p_c[:, g * TQ:(g + 1) * TQ]
    hi = psum.astype(_BF16)
    r1 = psum - hi.astype(_F32)
    mid = r1.astype(_BF16)
    lo = (r1 - mid.astype(_F32)).astype(_BF16)
    ovt = ovt_ref[...]
    imp = _dot(ovt, hi) + _dot(ovt, mid) + _dot(ovt, lo)
    jrow = lax.broadcasted_iota(jnp.int32, (n_sel, TQ), 0)
    cur = (a * TQ + lax.broadcasted_iota(jnp.int32, (n_sel, TQ), 1)) // SEL_BLOCK
    forced = (jrow == 0) | (jrow == cur) | (jrow == cur - 1)
    score = jnp.where(jrow <= cur, imp + jnp.where(forced, FORCE_BONUS, 0.0), NEG)
    sc_ref[...] = score

    def rank_body(jp, cnt):
        srow = sc_ref[pl.ds(jp, 1), :]
        tie = jnp.where(jrow > jp, 1.0, 0.0)
        return cnt + jnp.where(srow > score, 1.0, jnp.where(srow >= score, tie, 0.0))

    cnt = lax.fori_loop(0, (TQ // SEL_BLOCK) * (a + 1), rank_body, jnp.zeros((n_sel, TQ), _F32))
    mb = jnp.where(cnt < SEL_TOP_N, 0.0, NEG)
    mb_ref[...] = jnp.concatenate([mb] * NSA_GROUP, axis=1)

    def group_scores(k_ref, ek, q_aug, key0, n_tiles):
        keys = pl.ds(pl.multiple_of(key0, TQ), n_tiles * TQ)
        return _dot(jnp.concatenate([k_ref[0, keys, :], ek], axis=1), q_aug)

    def group_update(vt_ref, s, key0, n_tiles, table_entries, carry):
        m, acc = carry
        keys = pl.ds(pl.multiple_of(key0, TQ), n_tiles * TQ)
        s = jnp.concatenate([s[i * TQ:(i + 1) * TQ, :] + tab_ref[0, table_entries[i]]
                             for i in range(n_tiles)], axis=0)
        m_new = jnp.maximum(m, jnp.max(s, axis=0, keepdims=True))
        alpha = jnp.exp2(m - m_new)
        p = jnp.exp2((s - m_new).astype(_BF16))
        acc = alpha * acc + _dot(vt_ref[0, hk_vrows, keys], p)
        return m_new, acc

    init = (jnp.full((1, QL), NEG, _F32), jnp.zeros((VT_ROWS, QL), _F32))
    eks = eks_ref[...]
    blocks_per_group = GROUP_TILES * (TQ // SEL_BLOCK)
    group_keys = GROUP_TILES * TQ
    last_group = a // GROUP_TILES

    def sel_scores(gi):
        gi = jnp.minimum(gi, last_group)
        q_aug = augmented_queries(mb_ref[pl.ds(pl.multiple_of(gi * blocks_per_group, 8), blocks_per_group), :])
        return group_scores(ks_ref, eks, q_aug, gi * group_keys, GROUP_TILES)

    def sel_update(s, gi, carry):
        entries = []
        for i in range(GROUP_TILES):
            d = a - (gi * GROUP_TILES + i)
            entries.append(jnp.where(d < 0, TAB_MASKED, jnp.where(d >= N_NEAR, TAB_ZERO, d)))
        return group_update(vst_ref, s, gi * group_keys, GROUP_TILES, entries, carry)

    ss_ref[...] = sel_scores(0)

    def sel_pair(pi, carry):
        g0 = 2 * pi
        s0 = ss_ref[...]
        s1 = sel_scores(g0 + 1)
        carry = sel_update(s0, g0, carry)

        def second_group(carry):
            ss_ref[...] = sel_scores(g0 + 2)
            return sel_update(s1, g0 + 1, carry)

        return lax.cond(g0 + 1 <= last_group, second_group, lambda c: c, carry)

    _, acc_s = lax.fori_loop(0, last_group // 2 + 1, sel_pair, init)

    win_t0 = jnp.maximum(a - (WIN_TILES - 1), 0)
    win_entries = []
    for i in range(WIN_TILES):
        d = a - (win_t0 + i)
        win_entries.append(jnp.where(d < 0, TAB_MASKED, jnp.where(d == WIN_TILES - 1, TAB_WIN_EDGE, d)))
    q_aug_w = augmented_queries(jnp.zeros((8, QL), _F32))
    s_w = group_scores(kw_ref, ekw_ref[...], q_aug_w, win_t0 * TQ, WIN_TILES)
    _, acc_w = group_update(vwt_ref, s_w, win_t0 * TQ, WIN_TILES, win_entries, init)

    def gate(branch):
        grp = smt_ref[0, 8 + 8 * branch:16 + 8 * branch, :]
        sel = jnp.where(hk == 0, grp[0:NSA_GROUP, :], grp[NSA_GROUP:2 * NSA_GROUP, :])
        return _sigmoid(jnp.concatenate([sel[g:g + 1, :] for g in range(NSA_GROUP)], axis=1))

    def normalized(acc):
        return acc[0:NSA_DH, :] * (1.0 / acc[NSA_DH:NSA_DH + 1, :])

    o_t = gate(0) * o_c + gate(1) * normalized(acc_s) + gate(2) * normalized(acc_w)
    stacked = jnp.concatenate([o_t[:, g * TQ:(g + 1) * TQ] for g in range(NSA_GROUP)], axis=0)
    o_ref[0] = stacked.T * _silu(nz_ref[0])


def _nsa(nq, ks, kw, vst, vwt, kcmp, vcmpt, tab, far, ovt, smt, nz):
    batch, seq, _ = nq.shape
    ncp = kcmp.shape[1]
    n_sel = seq // SEL_BLOCK
    per_b = lambda shape: pl.BlockSpec((1,) + shape, lambda b, h, a: (b, 0, 0))
    const = lambda shape: pl.BlockSpec(shape, lambda b, h, a: (0,) * len(shape))
    vt_shape = (NSA_KV_HEADS * VT_ROWS, seq)
    return pl.pallas_call(
        _nsa_kernel,
        out_shape=jax.ShapeDtypeStruct((batch, seq, NSA_W), _F32),
        grid=(batch, NSA_KV_HEADS, seq // TQ),
        in_specs=[
            pl.BlockSpec((1, TQ, GQ_W), lambda b, h, a: (b, a, h)),
            per_b((seq, KV_W)), per_b((seq, KV_W)), per_b(vt_shape), per_b(vt_shape),
            per_b((ncp, KV_W)), per_b((KV_W, ncp)),
            pl.BlockSpec((1, N_TAB, TQ, QL), lambda b, h, a: (h, 0, 0, 0)),
            pl.BlockSpec((1, 8, QL), lambda b, h, a: (h, 0, 0)),
            const((n_sel, ncp)), const((GROUP_TILES * TQ, LANES)), const((WIN_TILES * TQ, LANES)),
            pl.BlockSpec((1, LANES, TQ), lambda b, h, a: (b, 0, a)),
            pl.BlockSpec((1, TQ, GQ_W), lambda b, h, a: (b, a, h)),
        ],
        out_specs=pl.BlockSpec((1, TQ, GQ_W), lambda b, h, a: (b, a, h)),
        scratch_shapes=[pltpu.VMEM((ncp, QL), _F32), pltpu.VMEM((n_sel, TQ), _F32),
                        pltpu.VMEM((n_sel, QL), _F32), pltpu.VMEM((GROUP_TILES * TQ, QL), _F32)],
        compiler_params=pltpu.CompilerParams(dimension_semantics=("arbitrary", "arbitrary", "arbitrary"),
                                             vmem_limit_bytes=VMEM_LIMIT),
        name="nsa",
    )(nq, ks, kw, vst, vwt, kcmp, vcmpt, tab, far, ovt, _key_onehots(GROUP_TILES, True),
      _key_onehots(WIN_TILES, False), smt, nz)


def _out_proj_kernel(og_ref, on_ref, x_ref, w_ref, fw_ref, o_ref):
    h = (x_ref[...] + _dot(og_ref[...].astype(_BF16), w_ref[0:GDN_W, :])
         + _dot(on_ref[...].astype(_BF16), w_ref[GDN_W:, :]))
    o_ref[...] = h * lax.rsqrt(jnp.mean(h * h, axis=-1, keepdims=True) + EPS) * fw_ref[...]


def _out_proj(o_g, o_n, x2d, w_out, final_norm_w, tm):
    n = x2d.shape[0]
    row = lambda w: pl.BlockSpec((tm, w), lambda i: (i, 0))
    return pl.pallas_call(
        _out_proj_kernel,
        out_shape=jax.ShapeDtypeStruct((n, D_MODEL), _F32),
        grid=(n // tm,),
        in_specs=[row(GDN_W), row(NSA_W), row(D_MODEL),
                  pl.BlockSpec((GDN_W + NSA_W, D_MODEL), lambda i: (0, 0)),
                  pl.BlockSpec((1, D_MODEL), lambda i: (0, 0))],
        out_specs=row(D_MODEL),
        compiler_params=pltpu.CompilerParams(dimension_semantics=("arbitrary",),
                                             vmem_limit_bytes=VMEM_LIMIT),
        name="out_proj",
    )(o_g, o_n, x2d, w_out.astype(_BF16), final_norm_w.reshape(1, D_MODEL).astype(_F32))


def kernel(x, norm_w, w_in, conv_w, a_log, dt_bias, gdn_norm_w, cmp_pe_k, cmp_pe_v, cmp_k_w1, cmp_k_w2,
           cmp_v_w1, cmp_v_w2, w_out, rel_bias, final_norm_w):
    batch, seq, _ = x.shape
    assert w_in.shape[0] == 1, "single-layer problem"
    assert seq % (8 * TQ) == 0
    n = batch * seq
    ncp = seq // CMP_STRIDE
    n_cmp = (seq - CMP_LEN) // CMP_STRIDE + 1
    tm = min(512, seq)

    x2d = x.reshape(n, D_MODEL)
    tab, far = _bias_tables(rel_bias)
    (conv_in, gz, sm, smt, nq, kc, vc, ks, kw, vst, vwt, nz) = _in_proj(
        x2d, norm_w[0], _regroup_w_in(w_in[0]), batch, seq, tm)

    o_g = _gdn(conv_in.reshape(batch, seq, GDN_CONV_W), conv_w[0], sm.reshape(batch, seq, LANES), smt,
               a_log[0], dt_bias[0], gz.reshape(batch, seq, GDN_W), gdn_norm_w[0], tm)

    kcmp, vcmpt = _compress(kc.reshape(batch, ncp, _GROUP_W), vc.reshape(batch, ncp, _GROUP_W),
                            _compress_weights(cmp_pe_k[0], cmp_k_w1[0], cmp_k_w2[0]),
                            _compress_weights(cmp_pe_v[0], cmp_v_w1[0], cmp_v_w2[0]))

    o_n = _nsa(nq.reshape(batch, seq, NSA_W), ks.reshape(batch, seq, KV_W), kw.reshape(batch, seq, KV_W),
               vst, vwt, kcmp, vcmpt, tab, far, _overlap_t(seq // SEL_BLOCK, ncp, n_cmp), smt,
               nz.reshape(batch, seq, NSA_W))

    out = _out_proj(o_g.reshape(n, GDN_W), o_n.reshape(n, NSA_W), x2d, w_out[0], final_norm_w, tm)
    return out.reshape(batch, seq, D_MODEL)
```

```python
import functools
import math

import jax
import jax.numpy as jnp
import numpy as np
from jax import lax
from jax.experimental import pallas as pl
from jax.experimental.pallas import tpu as pltpu

D_MODEL = 1024
GDN_HEADS = 4
GDN_DK = 128
GDN_DV = 128
CONV_K = 4
CHUNK = 64
NSA_HEADS = 8
NSA_KV_HEADS = 2
NSA_GROUP = NSA_HEADS // NSA_KV_HEADS
NSA_DH = 64
CMP_LEN = 32
CMP_STRIDE = 16
CMP_HIDDEN = 128
SEL_BLOCK = 64
SEL_TOP_N = 16
WINDOW = 512
FORCE_BONUS = 1e4
N_BUCKETS = 32
MAX_DISTANCE = 1024
EPS = 1e-6
NEG = -1e30

GDN_QK_W = GDN_HEADS * GDN_DK
GDN_W = GDN_HEADS * GDN_DV
GDN_CONV_W = 2 * GDN_QK_W + GDN_W
NSA_W = NSA_HEADS * NSA_DH
KV_W = NSA_KV_HEADS * NSA_DH
PROJ_SIZES = (GDN_QK_W, GDN_QK_W, GDN_W, GDN_W, GDN_HEADS, GDN_HEADS,
              NSA_W, KV_W, KV_W, KV_W, KV_W, KV_W, KV_W, 3 * NSA_HEADS, NSA_W)

LANES = 128
TQ = 128
QL = NSA_GROUP * TQ
GQ_W = NSA_GROUP * NSA_DH
N_NEAR = 8
TAB_ZERO = N_NEAR
TAB_MASKED = N_NEAR + 1
TAB_WIN_EDGE = N_NEAR + 2
TAB_CMP = N_NEAR + 3
N_TAB = N_NEAR + 4
GROUP_TILES = 4
WIN_TILES = WINDOW // TQ + 1
RANK_CHUNK = 16
VT_ROWS = 80
LOG2E = math.log2(math.e)
CMP_NEAR = 64
CMP_NEAR_BACK = 56
VMEM_LIMIT = 56 * 1024 * 1024

_F32 = jnp.float32
_BF16 = jnp.bfloat16


def _dot(a, b):
    return jnp.dot(a, b, preferred_element_type=_F32)


def _dot_nt(a, b):
    return lax.dot_general(a, b, (((1,), (1,)), ((), ())), preferred_element_type=_F32)


def _dot_tn(a, b):
    return lax.dot_general(a, b, (((0,), (0,)), ((), ())), preferred_element_type=_F32)


def _sigmoid(x):
    return 1.0 / (1.0 + jnp.exp(-x))


def _silu(x):
    return x * _sigmoid(x)


def _bias_tab_kernel(rel_ref, o_ref, far_ref):
    hk = pl.program_id(0)
    e = pl.program_id(1)
    is_cmp = e == TAB_CMP
    is_edge = e == TAB_WIN_EDGE
    base = jnp.where(e < N_NEAR, e * TQ,
                     jnp.where(is_edge, WINDOW, CMP_STRIDE * CMP_NEAR_BACK - (CMP_LEN - 1)))
    row_scale = jnp.where(is_cmp, CMP_STRIDE, 1)
    upper = jnp.where(is_edge, WINDOW, 1 << 30)
    row = lax.broadcasted_iota(jnp.int32, (TQ, TQ), 0)
    tok = lax.broadcasted_iota(jnp.int32, (TQ, TQ), 1)
    dist = base + tok - row_scale * row
    n = jnp.maximum(dist, 0)
    max_exact = N_BUCKETS // 2
    large = max_exact + (jnp.log(jnp.maximum(n, max_exact).astype(_F32) / max_exact)
                         / math.log(MAX_DISTANCE / max_exact) * (N_BUCKETS - max_exact)).astype(jnp.int32)
    bucket = jnp.where(n < max_exact, n, jnp.minimum(large, N_BUCKETS - 1))
    ok = (dist >= 0) & (dist < upper)
    for g in range(NSA_GROUP):
        head = hk * NSA_GROUP + g
        lanes = slice(g * TQ, (g + 1) * TQ)
        far = jnp.full((8, TQ), rel_ref[N_BUCKETS - 1, head] * LOG2E, _F32)
        far_hi = far.astype(_BF16).astype(_F32)
        far_lo = (far - far_hi).astype(_BF16).astype(_F32)
        val = jnp.zeros((TQ, TQ), _F32)
        for b in range(N_BUCKETS):
            val = jnp.where(bucket == b, rel_ref[b, head], val)
        val = val * LOG2E - jnp.where(is_cmp, 0.0, far_hi[0:1, :] + far_lo[0:1, :])
        val = jnp.where(ok, val, NEG)
        val = jnp.where(e == TAB_ZERO, 0.0, jnp.where(e == TAB_MASKED, NEG, val))
        o_ref[0, 0, :, lanes] = val

        sub = lax.broadcasted_iota(jnp.int32, (8, TQ), 0)
        far_ref[0, :, lanes] = jnp.where(sub == 0, far_hi, jnp.where(sub == 1, far_lo,
                                                                     jnp.where(sub == 2, far, 0.0)))


def _bias_tables(rel_bias):
    return pl.pallas_call(
        _bias_tab_kernel,
        out_shape=(jax.ShapeDtypeStruct((NSA_KV_HEADS, N_TAB, TQ, QL), _F32),
                   jax.ShapeDtypeStruct((NSA_KV_HEADS, 8, QL), _F32)),
        grid=(NSA_KV_HEADS, N_TAB),
        in_specs=[pl.BlockSpec(memory_space=pltpu.SMEM)],
        out_specs=(pl.BlockSpec((1, 1, TQ, QL), lambda h, e: (h, e, 0, 0)),
                   pl.BlockSpec((1, 8, QL), lambda h, e: (h, 0, 0))),
        name="bias_tables",
    )(rel_bias.astype(_F32))


_C_CONV = (0, GDN_CONV_W)
_C_GZ = (_C_CONV[1], _C_CONV[1] + GDN_W)
_C_SMALL = (_C_GZ[1], _C_GZ[1] + LANES)
_C_NQ = (_C_SMALL[1], _C_SMALL[1] + NSA_W)
_C_KC = (_C_NQ[1], _C_NQ[1] + KV_W)
_C_VC = (_C_KC[1], _C_KC[1] + KV_W)
_C_KS = (_C_VC[1], _C_VC[1] + KV_W)
_C_KW = (_C_KS[1], _C_KS[1] + KV_W)
_C_VS = (_C_KW[1], _C_KW[1] + KV_W)
_C_VW = (_C_VS[1], _C_VS[1] + KV_W)
_C_NZ = (_C_VW[1], _C_VW[1] + NSA_W)
_W_COLS = _C_NZ[1]


def _regroup_w_in(w):
    gq, gk, gv, gz, gb, ga, nq, kc, vc, ks, vs, kw, vw, ng, nz = jnp.split(
        w, [int(p) for p in np.cumsum(PROJ_SIZES)[:-1]], axis=-1)
    small = jnp.concatenate([gb, ga, ng], axis=-1)
    small = jnp.pad(small, ((0, 0), (0, LANES - small.shape[-1])))
    return jnp.concatenate([gq, gk, gv, gz, small, nq, kc, vc, ks, kw, vs, vw, nz], axis=-1).astype(_BF16)


def _in_proj_kernel(x_ref, nw_ref, w_ref, conv_ref, gz_ref, sm_ref, smt_ref, nq_ref, kc_ref, vc_ref,
                    ks_ref, kw_ref, vst_ref, vwt_ref, nz_ref):
    x = x_ref[...]
    u = x * lax.rsqrt(jnp.mean(x * x, axis=-1, keepdims=True) + EPS) * nw_ref[...]
    u = u.astype(_BF16)

    def proj(c):
        return _dot(u, w_ref[:, c[0]:c[1]])

    conv_ref[...] = proj(_C_CONV)
    gz_ref[...] = proj(_C_GZ)
    sm = proj(_C_SMALL)
    sm_ref[...] = sm
    smt_ref[0] = sm.T
    nq_ref[...] = proj(_C_NQ).astype(_BF16)
    kc_ref[...] = proj(_C_KC).astype(_BF16)
    vc_ref[...] = proj(_C_VC).astype(_BF16)
    ks_ref[...] = proj(_C_KS).astype(_BF16)
    kw_ref[...] = proj(_C_KW).astype(_BF16)
    tm = x.shape[0]
    ones_pad = (lax.broadcasted_iota(jnp.int32, (VT_ROWS - NSA_DH, tm), 0) == 0).astype(_F32)

    def values_t(c):
        vt = proj(c).T
        parts = []
        for h in range(NSA_KV_HEADS):
            parts += [vt[h * NSA_DH:(h + 1) * NSA_DH, :], ones_pad]
        return jnp.concatenate(parts, axis=0).astype(_BF16)

    vst_ref[0] = values_t(_C_VS)
    vwt_ref[0] = values_t(_C_VW)
    nz_ref[...] = proj(_C_NZ)


def _in_proj(x2d, norm_w, w_cat, batch, seq, tm):
    n = x2d.shape[0]
    tiles_per_seq = seq // tm
    row = lambda w: pl.BlockSpec((tm, w), lambda i: (i, 0))
    tr_map = lambda i: (i // tiles_per_seq, 0, i % tiles_per_seq)
    tr = pl.BlockSpec((1, LANES, tm), tr_map)
    trv = pl.BlockSpec((1, NSA_KV_HEADS * VT_ROWS, tm), tr_map)
    sds = jax.ShapeDtypeStruct
    return pl.pallas_call(
        _in_proj_kernel,
        out_shape=(
            sds((n, GDN_CONV_W), _F32), sds((n, GDN_W), _F32), sds((n, LANES), _F32),
            sds((batch, LANES, seq), _F32), sds((n, NSA_W), _BF16), sds((n, KV_W), _BF16),
            sds((n, KV_W), _BF16), sds((n, KV_W), _BF16), sds((n, KV_W), _BF16),
            sds((batch, NSA_KV_HEADS * VT_ROWS, seq), _BF16), sds((batch, NSA_KV_HEADS * VT_ROWS, seq), _BF16),
            sds((n, NSA_W), _F32)),
        grid=(n // tm,),
        in_specs=[row(D_MODEL), pl.BlockSpec((1, D_MODEL), lambda i: (0, 0)),
                  pl.BlockSpec((D_MODEL, _W_COLS), lambda i: (0, 0))],
        out_specs=(row(GDN_CONV_W), row(GDN_W), row(LANES), tr, row(NSA_W), row(KV_W), row(KV_W),
                   row(KV_W), row(KV_W), trv, trv, row(NSA_W)),
        compiler_params=pltpu.CompilerParams(dimension_semantics=("arbitrary",),
                                             vmem_limit_bytes=VMEM_LIMIT),
        name="in_proj",
    )(x2d, norm_w.reshape(1, D_MODEL).astype(_F32), w_cat)


_GDN_PAIR = 2 * CHUNK
_STACK = GDN_HEADS * CHUNK
_PREP_UNROLL = 4


def _segment_cumsum(g, axis):
    pos = lax.broadcasted_iota(jnp.int32, g.shape, axis) % CHUNK
    k = 1
    while k < CHUNK:
        g = g + jnp.where(pos >= k, pltpu.roll(g, k, axis), 0.0)
        k *= 2
    return g


def _stacked_unit_lower_inverses(lows):
    r = lax.broadcasted_iota(jnp.int32, (_STACK, _STACK), 0)
    c = lax.broadcasted_iota(jnp.int32, (_STACK, _STACK), 1)
    eye = (r == c).astype(_F32)
    ms = [-low for low in lows]
    xs = [eye + m for m in ms]
    ms = [_dot(m, m) for m in ms]
    span = 4
    while span < CHUNK:
        boths = [_dot(jnp.concatenate([x, m], axis=0), m) for x, m in zip(xs, ms)]
        xs = [x + both[:_STACK] for x, both in zip(xs, boths)]
        ms = [both[_STACK:] for both in boths]
        span *= 2
    return [x + _dot(x, m) for x, m in zip(xs, ms)]


def _gdn_kernel(x_ref, cw_ref, sm_ref, smt_ref, prow_ref, pcol_ref, gz_ref, gnw_ref, o_ref,
                xs_ref, q_ref, k_ref, v_ref, col_ref, grow_ref, state_ref,
                u_ref, w_ref, qd_ref, kd_ref, a_ref, egl_ref, oraw_ref, *, tile):
    t_idx = pl.program_id(1)

    @pl.when(t_idx == 0)
    def _():
        xs_ref[0:8, :] = jnp.zeros((8, GDN_CONV_W), _F32)
        state_ref[...] = jnp.zeros_like(state_ref)

    @pl.when(t_idx != 0)
    def _():
        xs_ref[0:8, :] = xs_ref[tile:tile + 8, :]

    xs_ref[8:tile + 8, :] = x_ref[0]

    y = jnp.zeros((tile, GDN_CONV_W), _F32)
    for tap in range(CONV_K):
        off = 8 - (CONV_K - 1) + tap
        y = y + cw_ref[tap:tap + 1, :] * xs_ref[off:off + tile, :]
    y = _silu(y)

    for h in range(GDN_HEADS):
        lo = h * GDN_DK
        qh = y[:, lo:lo + GDN_DK]
        kh = y[:, GDN_QK_W + lo:GDN_QK_W + lo + GDN_DK]
        q_ref[:, lo:lo + GDN_DK] = qh * lax.rsqrt(jnp.sum(qh * qh, axis=-1, keepdims=True) + EPS) * GDN_DK ** -0.5
        k_ref[:, lo:lo + GDN_DK] = kh * lax.rsqrt(jnp.sum(kh * kh, axis=-1, keepdims=True) + EPS)
    v_ref[...] = y[:, 2 * GDN_QK_W:]

    sm = sm_ref[0]
    lane = lax.broadcasted_iota(jnp.int32, sm.shape, 1)
    zc = sm + prow_ref[1:2, :]
    gcol = -jnp.exp(prow_ref[0:1, :]) * (jnp.maximum(zc, 0.0) + jnp.log1p(jnp.exp(-jnp.abs(zc))))
    gcol = _segment_cumsum(gcol, 0)
    col_ref[...] = jnp.where(lane < GDN_HEADS, _sigmoid(sm), gcol)
    zr = smt_ref[0, 0:8, :] + pcol_ref[1]
    grow = -jnp.exp(pcol_ref[0]) * (jnp.maximum(zr, 0.0) + jnp.log1p(jnp.exp(-jnp.abs(zr))))
    grow_ref[...] = _segment_cumsum(grow, 1)

    r = lax.broadcasted_iota(jnp.int32, (_STACK, _STACK), 0)
    c = lax.broadcasted_iota(jnp.int32, (_STACK, _STACK), 1)
    same_head = (r // CHUNK) == (c // CHUNK)
    tril = same_head & (r >= c)
    strict = same_head & (r > c)
    heads = range(GDN_HEADS)

    def prep_gram(ci, half):
        rows = pl.ds(pl.multiple_of(ci * CHUNK, CHUNK), CHUNK)
        slab = pl.ds(pl.multiple_of((ci // 2) * _GDN_PAIR, _GDN_PAIR), _GDN_PAIR)
        colv = col_ref[rows, :]
        grow2 = grow_ref[:, slab]

        def stack(ref, width):
            return jnp.concatenate([ref[rows, h * width:(h + 1) * width] for h in heads], axis=0)

        q = stack(q_ref, GDN_DK)
        k = stack(k_ref, GDN_DK)
        v = stack(v_ref, GDN_DV)
        beta = jnp.concatenate([colv[:, h:h + 1] for h in heads], axis=0)
        gc = jnp.concatenate([colv[:, GDN_HEADS + h:GDN_HEADS + h + 1] for h in heads], axis=0)
        gr = jnp.concatenate([grow2[GDN_HEADS + h:GDN_HEADS + h + 1, half * CHUNK:(half + 1) * CHUNK]
                              for h in heads], axis=1)
        last = [colv[CHUNK - 1:CHUNK, GDN_HEADS + h:GDN_HEADS + h + 1] for h in heads]
        g_last = jnp.concatenate([jnp.broadcast_to(x, (CHUNK, 1)) for x in last], axis=0)
        decay = jnp.where(tril, jnp.exp(jnp.where(tril, gc - gr, 0.0)), 0.0)
        kb = k * beta
        eg = jnp.exp(gc)
        low = jnp.where(strict, _dot_nt(kb, k) * decay, 0.0)
        a_ref[ci] = jnp.where(tril, _dot_nt(q, k) * decay, 0.0)
        kd_ref[ci] = k * jnp.exp(g_last - gc)
        qd_ref[ci] = q * eg
        egl_ref[ci] = jnp.concatenate([jnp.broadcast_to(jnp.exp(x), (1, GDN_DV)) for x in last]
                                      + [jnp.zeros((8 - GDN_HEADS, GDN_DV), _F32)], axis=0)
        return low, jnp.concatenate([v * beta, kb * eg], axis=1)

    def prep_body(pi, carry):
        cis = [pi * _PREP_UNROLL + j for j in range(_PREP_UNROLL)]
        lows, rhss = zip(*[prep_gram(ci, j % 2) for j, ci in enumerate(cis)])
        tinvs = _stacked_unit_lower_inverses(lows)
        sols = [_dot(tinv, rhs) for tinv, rhs in zip(tinvs, rhss)]
        for ci, sol in zip(cis, sols):
            u_ref[ci] = sol[:, :GDN_DV]
            w_ref[ci] = sol[:, GDN_DV:]
        return carry

    n_chunks = tile // CHUNK
    lax.fori_loop(0, n_chunks // _PREP_UNROLL, prep_body, 0)

    states = [state_ref[h] for h in heads]
    for ci in range(n_chunks):
        u, w, qd, kd, egl = u_ref[ci], w_ref[ci], qd_ref[ci], kd_ref[ci], egl_ref[ci]
        v_news, inters = [], []
        for h in heads:
            rs = slice(h * CHUNK, (h + 1) * CHUNK)
            wq = _dot(jnp.concatenate([w[rs], qd[rs]], axis=0), states[h])
            v_new = u[rs] - wq[:CHUNK]
            states[h] = states[h] * egl[h:h + 1, :] + _dot_tn(kd[rs], v_new)
            v_news.append(v_new)
            inters.append(wq[CHUNK:])
        o = jnp.concatenate(inters, axis=0) + _dot(a_ref[ci], jnp.concatenate(v_news, axis=0))
        for h in heads:
            oraw_ref[ci * CHUNK:(ci + 1) * CHUNK, h * GDN_DV:(h + 1) * GDN_DV] = o[h * CHUNK:(h + 1) * CHUNK]
    for h in heads:
        state_ref[h] = states[h]

    gnw = gnw_ref[...]
    for h in heads:
        lanes = slice(h * GDN_DV, (h + 1) * GDN_DV)
        o = oraw_ref[:, lanes]
        o = o * lax.rsqrt(jnp.mean(o * o, axis=-1, keepdims=True) + EPS) * gnw
        o_ref[0, :, lanes] = o * _silu(gz_ref[0, :, lanes])


def _gdn(conv_in, conv_w, sm, smt, a_log, dt_bias, gz, gdn_norm_w, tile):
    batch, seq, _ = conv_in.shape
    prow = jnp.zeros((2, LANES), _F32)
    prow = prow.at[0, GDN_HEADS:2 * GDN_HEADS].set(a_log.astype(_F32))
    prow = prow.at[1, GDN_HEADS:2 * GDN_HEADS].set(dt_bias.astype(_F32))
    pcol = jnp.broadcast_to(prow[:, :8, None], (2, 8, tile))
    n_chunks = tile // CHUNK
    assert n_chunks % _PREP_UNROLL == 0 and _PREP_UNROLL % 2 == 0
    kern = functools.partial(_gdn_kernel, tile=tile)
    return pl.pallas_call(
        kern,
        out_shape=jax.ShapeDtypeStruct((batch, seq, GDN_W), _F32),
        grid=(batch, seq // tile),
        in_specs=[
            pl.BlockSpec((1, tile, GDN_CONV_W), lambda b, t: (b, t, 0)),
            pl.BlockSpec((CONV_K, GDN_CONV_W), lambda b, t: (0, 0)),
            pl.BlockSpec((1, tile, LANES), lambda b, t: (b, t, 0)),
            pl.BlockSpec((1, LANES, tile), lambda b, t: (b, 0, t)),
            pl.BlockSpec((2, LANES), lambda b, t: (0, 0)),
            pl.BlockSpec((2, 8, tile), lambda b, t: (0, 0, 0)),
            pl.BlockSpec((1, tile, GDN_W), lambda b, t: (b, t, 0)),
            pl.BlockSpec((1, GDN_DV), lambda b, t: (0, 0)),
        ],
        out_specs=pl.BlockSpec((1, tile, GDN_W), lambda b, t: (b, t, 0)),
        scratch_shapes=[
            pltpu.VMEM((tile + 8, GDN_CONV_W), _F32),
            pltpu.VMEM((tile, GDN_QK_W), _F32),
            pltpu.VMEM((tile, GDN_QK_W), _F32),
            pltpu.VMEM((tile, GDN_W), _F32),
            pltpu.VMEM((tile, LANES), _F32),
            pltpu.VMEM((8, tile), _F32),
            pltpu.VMEM((GDN_HEADS, GDN_DK, GDN_DV), _F32),
            pltpu.VMEM((n_chunks, _STACK, GDN_DV), _F32),
            pltpu.VMEM((n_chunks, _STACK, GDN_DV), _F32),
            pltpu.VMEM((n_chunks, _STACK, GDN_DK), _F32),
            pltpu.VMEM((n_chunks, _STACK, GDN_DK), _F32),
            pltpu.VMEM((n_chunks, _STACK, _STACK), _F32),
            pltpu.VMEM((n_chunks, 8, GDN_DV), _F32),
            pltpu.VMEM((tile, GDN_W), _F32),
        ],
        compiler_params=pltpu.CompilerParams(dimension_semantics=("arbitrary", "arbitrary"),
                                             vmem_limit_bytes=VMEM_LIMIT),
        name="gdn",
    )(conv_in, conv_w.astype(_F32), sm, smt, prow, pcol, gz, gdn_norm_w.reshape(1, GDN_DV).astype(_F32))


_GROUP_W = CMP_STRIDE * KV_W
_HID2 = NSA_KV_HEADS * CMP_HIDDEN


def _compress_weights(pe, w1, w2):
    eye = jnp.eye(NSA_KV_HEADS, dtype=_F32)
    halves = []
    pes = []
    for part in range(CMP_LEN // CMP_STRIDE):
        w1p = w1[part * CMP_STRIDE:(part + 1) * CMP_STRIDE].astype(_F32)
        big = jnp.einsum('ldf,hg->lhdgf', w1p, eye).reshape(_GROUP_W, _HID2)
        halves.append(big.astype(_BF16))
        pep = pe[part * CMP_STRIDE:(part + 1) * CMP_STRIDE].astype(_F32)
        pes.append(jnp.broadcast_to(pep[:, None, :], (CMP_STRIDE, NSA_KV_HEADS, NSA_DH)).reshape(1, _GROUP_W))
    w2big = jnp.einsum('fd,hg->hfgd', w2.astype(_F32), eye).reshape(_HID2, KV_W).astype(_BF16)
    pe2 = jnp.broadcast_to(jnp.stack(pes, 0), (2, 8, _GROUP_W)).astype(_BF16)
    return halves[0], halves[1], w2big, pe2


def _compress_kernel(kc_ref, vc_ref, kwa_ref, kwb_ref, kw2_ref, kpe_ref, vwa_ref, vwb_ref, vw2_ref, vpe_ref,
                     ko_ref, vo_ref):
    def mlp(g_ref, wa_ref, wb_ref, w2_ref, pe_ref):
        g = g_ref[0]
        ncp = g.shape[0]
        p0 = _dot(g, wa_ref[...])
        p1 = _dot(g, wb_ref[...])
        const = _dot(pe_ref[0], wa_ref[...]) + _dot(pe_ref[1], wb_ref[...])
        pre = p0 + pltpu.roll(p1, ncp - 1, 0) + const[0:1, :]
        return _dot(_silu(pre).astype(_BF16), w2_ref[...])

    ko_ref[0] = mlp(kc_ref, kwa_ref, kwb_ref, kw2_ref, kpe_ref).astype(_BF16)
    vo_ref[0] = mlp(vc_ref, vwa_ref, vwb_ref, vw2_ref, vpe_ref).T.astype(_BF16)


def _compress(kc, vc, kweights, vweights):
    batch, ncp, _ = kc.shape
    full = lambda a: pl.BlockSpec(a.shape, lambda b: (0,) * a.ndim)
    per_b = pl.BlockSpec((1, ncp, _GROUP_W), lambda b: (b, 0, 0))
    return pl.pallas_call(
        _compress_kernel,
        out_shape=(jax.ShapeDtypeStruct((batch, ncp, KV_W), _BF16),
                   jax.ShapeDtypeStruct((batch, KV_W, ncp), _BF16)),
        grid=(batch,),
        in_specs=[per_b, per_b] + [full(a) for a in kweights] + [full(a) for a in vweights],
        out_specs=(pl.BlockSpec((1, ncp, KV_W), lambda b: (b, 0, 0)),
                   pl.BlockSpec((1, KV_W, ncp), lambda b: (b, 0, 0))),
        compiler_params=pltpu.CompilerParams(dimension_semantics=("arbitrary",),
                                             vmem_limit_bytes=VMEM_LIMIT),
        name="compress",
    )(kc, vc, *kweights, *vweights)


def _overlap_t(n_sel, ncp, n_cmp):
    cs = CMP_STRIDE * np.arange(ncp)
    ss = SEL_BLOCK * np.arange(n_sel)
    ov = ((cs[None, :] < ss[:, None] + SEL_BLOCK) & (cs[None, :] + CMP_LEN > ss[:, None])
          & (np.arange(ncp)[None, :] < n_cmp))
    return jnp.asarray(ov.astype(np.float32), dtype=_BF16)


def _key_onehots(n_tiles, with_blocks):
    key = np.arange(n_tiles * TQ)[:, None]
    col = np.arange(LANES)[None, :]
    ek = np.broadcast_to((col == 8) | (col == 9), (n_tiles * TQ, LANES))
    if with_blocks:
        ek = ek | (col == key // SEL_BLOCK)
    return jnp.asarray(ek.astype(np.float32), dtype=_BF16)


def _nsa_kernel(q_ref, ks_ref, kw_ref, vst_ref, vwt_ref, kc_ref, vct_ref, tab_ref, far_ref, ovt_ref, eks_ref,
                ekw_ref, smt_ref, nz_ref, o_ref, cb_ref, sc_ref, cnt_ref, mb_ref, ss_ref, st_ref, ps_ref, ocw_ref):
    hk = pl.program_id(1)
    a = pl.program_id(2)
    ncp = kc_ref.shape[1]
    n_sel = ovt_ref.shape[0]
    hk_rows = pl.ds(pl.multiple_of(hk * NSA_DH, NSA_DH), NSA_DH)
    hk_vrows = pl.ds(pl.multiple_of(hk * VT_ROWS, VT_ROWS), VT_ROWS)

    qf = q_ref[0].astype(_F32) * (NSA_DH ** -0.5 * LOG2E)
    qt = qf.T
    qt4 = jnp.concatenate([qt[g * NSA_DH:(g + 1) * NSA_DH, :] for g in range(NSA_GROUP)], axis=1)
    qrow = lax.broadcasted_iota(jnp.int32, (KV_W, QL), 0)
    qt_b = jnp.where(qrow // NSA_DH == hk, jnp.concatenate([qt4] * NSA_KV_HEADS, axis=0), 0.0).astype(_BF16)

    far_rows = far_ref[0]
    far_row = far_rows[2:3, :]
    zero_rows = jnp.zeros((LANES - 16, QL), _BF16)

    def augmented_queries(mask_rows):
        extra = jnp.concatenate([mask_rows, far_rows], axis=0).astype(_BF16)
        return jnp.concatenate([qt_b, extra, zero_rows], axis=0)

    def group_scores(k_ref, ek, q_aug, key0, n_tiles):
        keys = pl.ds(pl.multiple_of(key0, TQ), n_tiles * TQ)
        return _dot(jnp.concatenate([k_ref[0, keys, :], ek], axis=1), q_aug)

    def gate(branch):
        grp = smt_ref[0, 8 + 8 * branch:16 + 8 * branch, :]
        sel = jnp.where(hk == 0, grp[0:NSA_GROUP, :], grp[NSA_GROUP:2 * NSA_GROUP, :])
        return _sigmoid(jnp.concatenate([sel[g:g + 1, :] for g in range(NSA_GROUP)], axis=1))

    def normalized(acc):
        return acc[0:NSA_DH, :] * (1.0 / acc[NSA_DH:NSA_DH + 1, :])

    win_t0 = jnp.maximum(a - (WIN_TILES - 1), 0)
    win_keys = pl.ds(pl.multiple_of(win_t0 * TQ, TQ), WIN_TILES * TQ)
    s_w = group_scores(kw_ref, ekw_ref[...], augmented_queries(jnp.zeros((8, QL), _F32)), win_t0 * TQ, WIN_TILES)

    win_start = pl.multiple_of(jnp.maximum(8 * a - CMP_NEAR_BACK, 0), 8)
    tab_start = pl.multiple_of(win_start - (8 * a - CMP_NEAR_BACK), 8)
    crow = lax.broadcasted_iota(jnp.int32, (ncp, QL), 0)
    cb_ref[...] = jnp.where(crow < win_start, far_row, NEG)
    cb_ref[pl.ds(win_start, CMP_NEAR), :] = tab_ref[0, TAB_CMP, pl.ds(tab_start, CMP_NEAR), :]
    s_c = _dot(kc_ref[0], qt_b) + cb_ref[...]

    parts = []
    for i in range(WIN_TILES):
        d = a - (win_t0 + i)
        entry = jnp.where(d < 0, TAB_MASKED, jnp.where(d == WIN_TILES - 1, TAB_WIN_EDGE, d))
        parts.append(s_w[i * TQ:(i + 1) * TQ, :] + tab_ref[0, entry])
    s_w = jnp.concatenate(parts, axis=0)
    m_w = jnp.max(s_w, axis=0, keepdims=True)
    acc_w = _dot(vwt_ref[0, hk_vrows, win_keys], jnp.exp2((s_w - m_w).astype(_BF16)))

    m_c = jnp.max(s_c, axis=0, keepdims=True)
    e_c = jnp.exp2(s_c - m_c)
    l_c = jnp.sum(e_c, axis=0, keepdims=True)
    p_c = e_c * jnp.where(m_c > 0.5 * NEG, 1.0 / l_c, 0.0)
    o_c = _dot(vct_ref[0, hk_rows, :], p_c.astype(_BF16))
    ocw_ref[...] = gate(0) * o_c + gate(2) * normalized(acc_w)

    psum = p_c[:, 0:TQ]
    for g in range(1, NSA_GROUP):
        psum = psum + p_c[:, g * TQ:(g + 1) * TQ]
    hi = psum.astype(_BF16)
    r1 = psum - hi.astype(_F32)
    mid = r1.astype(_BF16)
    lo = (r1 - mid.astype(_F32)).astype(_BF16)
    ovt = ovt_ref[...]
    imp = _dot(ovt, hi) + _dot(ovt, mid) + _dot(ovt, lo)
    jrow = lax.broadcasted_iota(jnp.int32, (n_sel, TQ), 0)
    cur = (a * TQ + lax.broadcasted_iota(jnp.int32, (n_sel, TQ), 1)) // SEL_BLOCK
    forced = (jrow == 0) | (jrow == cur) | (jrow == cur - 1)
    score = jnp.where(jrow <= cur, imp + jnp.where(forced, FORCE_BONUS, 0.0), NEG)
    sc_ref[...] = score

    cnt_ref[...] = jnp.zeros((n_sel, TQ), _F32)
    sub = lax.broadcasted_iota(jnp.int32, (8, TQ), 0)
    last_valid = (TQ // SEL_BLOCK) * (a + 1) - 1
    for chunk in range(n_sel // RANK_CHUNK):
        @pl.when(chunk * RANK_CHUNK <= last_valid)
        def _():
            rows = [sc_ref[8 * r:8 * r + 8, :] for r in range(n_sel // 8)]
            cnts = [cnt_ref[8 * r:8 * r + 8, :] for r in range(n_sel // 8)]
            for jp in range(chunk * RANK_CHUNK, (chunk + 1) * RANK_CHUNK):
                srow = sc_ref[jp:jp + 1, :]
                for r in range(n_sel // 8):
                    if 8 * r > jp:
                        inc = jnp.where(srow >= rows[r], 1.0, 0.0)
                    elif 8 * r + 7 < jp:
                        inc = jnp.where(srow > rows[r], 1.0, 0.0)
                    else:
                        tie = jnp.where(sub + 8 * r > jp, 1.0, 0.0)
                        inc = jnp.where(srow > rows[r], 1.0, jnp.where(srow >= rows[r], tie, 0.0))
                    cnts[r] = cnts[r] + inc
            for r in range(n_sel // 8):
                cnt_ref[8 * r:8 * r + 8, :] = cnts[r]

    mb = jnp.where(cnt_ref[...] < SEL_TOP_N, 0.0, NEG)
    mb_ref[...] = jnp.concatenate([mb] * NSA_GROUP, axis=1)

    eks = eks_ref[...]
    blocks_per_group = GROUP_TILES * (TQ // SEL_BLOCK)
    group_keys = GROUP_TILES * TQ
    last_group = a // GROUP_TILES

    def sel_scores(gi):
        q_aug = augmented_queries(mb_ref[pl.ds(pl.multiple_of(gi * blocks_per_group, 8), blocks_per_group), :])
        return group_scores(ks_ref, eks, q_aug, gi * group_keys, GROUP_TILES)

    def sel_values(gi):
        return vst_ref[0, hk_vrows, pl.ds(pl.multiple_of(gi * group_keys, group_keys), group_keys)]

    def sel_step(gi, carry, near, last):
        m, alpha_prev, acc = carry
        s_next = None if last else sel_scores(gi + 1)
        pv_prev = _dot(sel_values(jnp.maximum(gi - 1, 0)), ps_ref[...])
        s = ss_ref[...]
        if near:
            parts = []
            for i in range(GROUP_TILES):
                d = a - (gi * GROUP_TILES + i)
                entry = jnp.where(d < 0, TAB_MASKED, jnp.where(d >= N_NEAR, TAB_ZERO, d))
                parts.append(s[i * TQ:(i + 1) * TQ, :] + tab_ref[0, entry])
            s = jnp.concatenate(parts, axis=0)
        if near:
            st_ref[...] = s
        m_new = jnp.maximum(m, jnp.max(s, axis=0, keepdims=True))
        alpha = jnp.exp2(m - m_new)
        acc = alpha_prev * acc + pv_prev
        s = st_ref[...] if near else ss_ref[...]
        ps_ref[...] = jnp.exp2((s - m_new).astype(_BF16))
        if not last:
            ss_ref[...] = s_next
        return m_new, alpha, acc

    ss_ref[...] = sel_scores(0)
    ps_ref[...] = jnp.zeros(ps_ref.shape, _BF16)
    carry = (jnp.full((1, QL), NEG, _F32), jnp.ones((1, QL), _F32), jnp.zeros((VT_ROWS, QL), _F32))
    n_far = jnp.maximum(last_group - N_NEAR // GROUP_TILES, 0)
    carry = lax.fori_loop(0, n_far, functools.partial(sel_step, near=False, last=False), carry)
    carry = lax.fori_loop(n_far, last_group, functools.partial(sel_step, near=True, last=False), carry)
    _, alpha_s, acc_s = sel_step(last_group, carry, near=True, last=True)
    acc_s = alpha_s * acc_s + _dot(sel_values(last_group), ps_ref[...])

    o_t = ocw_ref[...] + gate(1) * normalized(acc_s)
    stacked = jnp.concatenate([o_t[:, g * TQ:(g + 1) * TQ] for g in range(NSA_GROUP)], axis=0)
    o_ref[0] = stacked.T * _silu(nz_ref[0])


def _nsa(nq, ks, kw, vst, vwt, kcmp, vcmpt, tab, far, ovt, smt, nz):
    batch, seq, _ = nq.shape
    ncp = kcmp.shape[1]
    n_sel = seq // SEL_BLOCK
    per_b = lambda shape: pl.BlockSpec((1,) + shape, lambda b, h, a: (b, 0, 0))
    const = lambda shape: pl.BlockSpec(shape, lambda b, h, a: (0,) * len(shape))
    vt_shape = (NSA_KV_HEADS * VT_ROWS, seq)
    return pl.pallas_call(
        _nsa_kernel,
        out_shape=jax.ShapeDtypeStruct((batch, seq, NSA_W), _F32),
        grid=(batch, NSA_KV_HEADS, seq // TQ),
        in_specs=[
            pl.BlockSpec((1, TQ, GQ_W), lambda b, h, a: (b, a, h)),
            per_b((seq, KV_W)), per_b((seq, KV_W)), per_b(vt_shape), per_b(vt_shape),
            per_b((ncp, KV_W)), per_b((KV_W, ncp)),
            pl.BlockSpec((1, N_TAB, TQ, QL), lambda b, h, a: (h, 0, 0, 0)),
            pl.BlockSpec((1, 8, QL), lambda b, h, a: (h, 0, 0)),
            const((n_sel, ncp)), const((GROUP_TILES * TQ, LANES)), const((WIN_TILES * TQ, LANES)),
            pl.BlockSpec((1, LANES, TQ), lambda b, h, a: (b, 0, a)),
            pl.BlockSpec((1, TQ, GQ_W), lambda b, h, a: (b, a, h)),
        ],
        out_specs=pl.BlockSpec((1, TQ, GQ_W), lambda b, h, a: (b, a, h)),
        scratch_shapes=[pltpu.VMEM((ncp, QL), _F32), pltpu.VMEM((n_sel, TQ), _F32),
                        pltpu.VMEM((n_sel, TQ), _F32), pltpu.VMEM((n_sel, QL), _F32),
                        pltpu.VMEM((GROUP_TILES * TQ, QL), _F32), pltpu.VMEM((GROUP_TILES * TQ, QL), _F32),
                        pltpu.VMEM((GROUP_TILES * TQ, QL), _BF16), pltpu.VMEM((NSA_DH, QL), _F32)],
        compiler_params=pltpu.CompilerParams(dimension_semantics=("arbitrary", "arbitrary", "arbitrary"),
                                             vmem_limit_bytes=VMEM_LIMIT),
        name="nsa",
    )(nq, ks, kw, vst, vwt, kcmp, vcmpt, tab, far, ovt, _key_onehots(GROUP_TILES, True),
      _key_onehots(WIN_TILES, False), smt, nz)


def _out_proj_kernel(og_ref, on_ref, x_ref, w_ref, fw_ref, o_ref):
    h = (x_ref[...] + _dot(og_ref[...].astype(_BF16), w_ref[0:GDN_W, :])
         + _dot(on_ref[...].astype(_BF16), w_ref[GDN_W:, :]))
    o_ref[...] = h * lax.rsqrt(jnp.mean(h * h, axis=-1, keepdims=True) + EPS) * fw_ref[...]


def _out_proj(o_g, o_n, x2d, w_out, final_norm_w, tm):
    n = x2d.shape[0]
    row = lambda w: pl.BlockSpec((tm, w), lambda i: (i, 0))
    return pl.pallas_call(
        _out_proj_kernel,
        out_shape=jax.ShapeDtypeStruct((n, D_MODEL), _F32),
        grid=(n // tm,),
        in_specs=[row(GDN_W), row(NSA_W), row(D_MODEL),
                  pl.BlockSpec((GDN_W + NSA_W, D_MODEL), lambda i: (0, 0)),
                  pl.BlockSpec((1, D_MODEL), lambda i: (0, 0))],
        out_specs=row(D_MODEL),
        compiler_params=pltpu.CompilerParams(dimension_semantics=("arbitrary",),
                                             vmem_limit_bytes=VMEM_LIMIT),
        name="out_proj",
    )(o_g, o_n, x2d, w_out.astype(_BF16), final_norm_w.reshape(1, D_MODEL).astype(_F32))


def kernel(x, norm_w, w_in, conv_w, a_log, dt_bias, gdn_norm_w, cmp_pe_k, cmp_pe_v, cmp_k_w1, cmp_k_w2,
           cmp_v_w1, cmp_v_w2, w_out, rel_bias, final_norm_w):
    batch, seq, _ = x.shape
    assert w_in.shape[0] == 1, "single-layer problem"
    assert seq % (8 * TQ) == 0
    n = batch * seq
    ncp = seq // CMP_STRIDE
    n_cmp = (seq - CMP_LEN) // CMP_STRIDE + 1
    tm = min(512, seq)

    x2d = x.reshape(n, D_MODEL)
    tab, far = _bias_tables(rel_bias)
    (conv_in, gz, sm, smt, nq, kc, vc, ks, kw, vst, vwt, nz) = _in_proj(
        x2d, norm_w[0], _regroup_w_in(w_in[0]), batch, seq, tm)

    o_g = _gdn(conv_in.reshape(batch, seq, GDN_CONV_W), conv_w[0], sm.reshape(batch, seq, LANES), smt,
               a_log[0], dt_bias[0], gz.reshape(batch, seq, GDN_W), gdn_norm_w[0], tm)

    kcmp, vcmpt = _compress(kc.reshape(batch, ncp, _GROUP_W), vc.reshape(batch, ncp, _GROUP_W),
                            _compress_weights(cmp_pe_k[0], cmp_k_w1[0], cmp_k_w2[0]),
                            _compress_weights(cmp_pe_v[0], cmp_v_w1[0], cmp_v_w2[0]))

    o_n = _nsa(nq.reshape(batch, seq, NSA_W), ks.reshape(batch, seq, KV_W), kw.reshape(batch, seq, KV_W),
               vst, vwt, kcmp, vcmpt, tab, far, _overlap_t(seq // SEL_BLOCK, ncp, n_cmp), smt,
               nz.reshape(batch, seq, NSA_W))

    out = _out_proj(o_g.reshape(n, GDN_W), o_n.reshape(n, NSA_W), x2d, w_out[0], final_norm_w, tm)
    return out.reshape(batch, seq, D_MODEL)
```

```python
import functools
import math

import jax
import jax.numpy as jnp
import numpy as np
from jax import lax
from jax.experimental import pallas as pl
from jax.experimental.pallas import tpu as pltpu

D_MODEL = 1024
GDN_HEADS = 4
GDN_DK = 128
GDN_DV = 128
CONV_K = 4
CHUNK = 64
NSA_HEADS = 8
NSA_KV_HEADS = 2
NSA_GROUP = NSA_HEADS // NSA_KV_HEADS
NSA_DH = 64
CMP_LEN = 32
CMP_STRIDE = 16
CMP_HIDDEN = 128
SEL_BLOCK = 64
SEL_TOP_N = 16
WINDOW = 512
FORCE_BONUS = 1e4
N_BUCKETS = 32
MAX_DISTANCE = 1024
EPS = 1e-6
NEG = -1e30

GDN_QK_W = GDN_HEADS * GDN_DK
GDN_W = GDN_HEADS * GDN_DV
GDN_CONV_W = 2 * GDN_QK_W + GDN_W
NSA_W = NSA_HEADS * NSA_DH
KV_W = NSA_KV_HEADS * NSA_DH
PROJ_SIZES = (GDN_QK_W, GDN_QK_W, GDN_W, GDN_W, GDN_HEADS, GDN_HEADS,
              NSA_W, KV_W, KV_W, KV_W, KV_W, KV_W, KV_W, 3 * NSA_HEADS, NSA_W)

LANES = 128
TQ = 128
QL = NSA_GROUP * TQ
GQ_W = NSA_GROUP * NSA_DH
N_NEAR = 8
TAB_ZERO = N_NEAR
TAB_MASKED = N_NEAR + 1
TAB_WIN_EDGE = N_NEAR + 2
TAB_CMP = N_NEAR + 3
N_TAB = N_NEAR + 4
GROUP_TILES = 4
WIN_TILES = WINDOW // TQ + 1
RANK_CHUNK = 16
VT_ROWS = 80
LOG2E = math.log2(math.e)
CMP_NEAR = 64
CMP_NEAR_BACK = 56
VMEM_LIMIT = 56 * 1024 * 1024

_F32 = jnp.float32
_BF16 = jnp.bfloat16


def _dot(a, b):
    return jnp.dot(a, b, preferred_element_type=_F32)


def _dot_nt(a, b):
    return lax.dot_general(a, b, (((1,), (1,)), ((), ())), preferred_element_type=_F32)


def _dot_tn(a, b):
    return lax.dot_general(a, b, (((0,), (0,)), ((), ())), preferred_element_type=_F32)


def _sigmoid(x):
    return 1.0 / (1.0 + jnp.exp(-x))


def _silu(x):
    return x * _sigmoid(x)


def _bias_tab_kernel(rel_ref, o_ref, far_ref):
    hk = pl.program_id(0)
    e = pl.program_id(1)
    is_cmp = e == TAB_CMP
    is_edge = e == TAB_WIN_EDGE
    base = jnp.where(e < N_NEAR, e * TQ,
                     jnp.where(is_edge, WINDOW, CMP_STRIDE * CMP_NEAR_BACK - (CMP_LEN - 1)))
    row_scale = jnp.where(is_cmp, CMP_STRIDE, 1)
    upper = jnp.where(is_edge, WINDOW, 1 << 30)
    row = lax.broadcasted_iota(jnp.int32, (TQ, TQ), 0)
    tok = lax.broadcasted_iota(jnp.int32, (TQ, TQ), 1)
    dist = base + tok - row_scale * row
    n = jnp.maximum(dist, 0)
    max_exact = N_BUCKETS // 2
    large = max_exact + (jnp.log(jnp.maximum(n, max_exact).astype(_F32) / max_exact)
                         / math.log(MAX_DISTANCE / max_exact) * (N_BUCKETS - max_exact)).astype(jnp.int32)
    bucket = jnp.where(n < max_exact, n, jnp.minimum(large, N_BUCKETS - 1))
    ok = (dist >= 0) & (dist < upper)
    for g in range(NSA_GROUP):
        head = hk * NSA_GROUP + g
        lanes = slice(g * TQ, (g + 1) * TQ)
        far = jnp.full((8, TQ), rel_ref[N_BUCKETS - 1, head] * LOG2E, _F32)
        far_hi = far.astype(_BF16).astype(_F32)
        far_lo = (far - far_hi).astype(_BF16).astype(_F32)
        val = jnp.zeros((TQ, TQ), _F32)
        for b in range(N_BUCKETS):
            val = jnp.where(bucket == b, rel_ref[b, head], val)
        val = val * LOG2E - jnp.where(is_cmp, 0.0, far_hi[0:1, :] + far_lo[0:1, :])
        val = jnp.where(ok, val, NEG)
        val = jnp.where(e == TAB_ZERO, 0.0, jnp.where(e == TAB_MASKED, NEG, val))
        o_ref[0, 0, :, lanes] = val

        sub = lax.broadcasted_iota(jnp.int32, (8, TQ), 0)
        far_ref[0, :, lanes] = jnp.where(sub == 0, far_hi, jnp.where(sub == 1, far_lo,
                                                                     jnp.where(sub == 2, far, 0.0)))


def _bias_tables(rel_bias):
    return pl.pallas_call(
        _bias_tab_kernel,
        out_shape=(jax.ShapeDtypeStruct((NSA_KV_HEADS, N_TAB, TQ, QL), _F32),
                   jax.ShapeDtypeStruct((NSA_KV_HEADS, 8, QL), _F32)),
        grid=(NSA_KV_HEADS, N_TAB),
        in_specs=[pl.BlockSpec(memory_space=pltpu.SMEM)],
        out_specs=(pl.BlockSpec((1, 1, TQ, QL), lambda h, e: (h, e, 0, 0)),
                   pl.BlockSpec((1, 8, QL), lambda h, e: (h, 0, 0))),
        name="bias_tables",
    )(rel_bias.astype(_F32))


_C_CONV = (0, GDN_CONV_W)
_C_GZ = (_C_CONV[1], _C_CONV[1] + GDN_W)
_C_SMALL = (_C_GZ[1], _C_GZ[1] + LANES)
_C_NQ = (_C_SMALL[1], _C_SMALL[1] + NSA_W)
_C_KC = (_C_NQ[1], _C_NQ[1] + KV_W)
_C_VC = (_C_KC[1], _C_KC[1] + KV_W)
_C_KS = (_C_VC[1], _C_VC[1] + KV_W)
_C_KW = (_C_KS[1], _C_KS[1] + KV_W)
_C_VS = (_C_KW[1], _C_KW[1] + KV_W)
_C_VW = (_C_VS[1], _C_VS[1] + KV_W)
_C_NZ = (_C_VW[1], _C_VW[1] + NSA_W)
_W_COLS = _C_NZ[1]


def _regroup_w_in(w):
    gq, gk, gv, gz, gb, ga, nq, kc, vc, ks, vs, kw, vw, ng, nz = jnp.split(
        w, [int(p) for p in np.cumsum(PROJ_SIZES)[:-1]], axis=-1)
    small = jnp.concatenate([gb, ga, ng], axis=-1)
    small = jnp.pad(small, ((0, 0), (0, LANES - small.shape[-1])))
    return jnp.concatenate([gq, gk, gv, gz, small, nq, kc, vc, ks, kw, vs, vw, nz], axis=-1).astype(_BF16)


def _in_proj_kernel(x_ref, nw_ref, w_ref, conv_ref, gz_ref, sm_ref, smt_ref, nq_ref, kc_ref, vc_ref,
                    ks_ref, kw_ref, vst_ref, vwt_ref, nz_ref):
    x = x_ref[...]
    u = x * lax.rsqrt(jnp.mean(x * x, axis=-1, keepdims=True) + EPS) * nw_ref[...]
    u = u.astype(_BF16)

    def proj(c):
        return _dot(u, w_ref[:, c[0]:c[1]])

    conv_ref[...] = proj(_C_CONV)
    gz_ref[...] = proj(_C_GZ)
    sm = proj(_C_SMALL)
    sm_ref[...] = sm
    smt_ref[0] = sm.T
    nq_ref[...] = proj(_C_NQ).astype(_BF16)
    kc_ref[...] = proj(_C_KC).astype(_BF16)
    vc_ref[...] = proj(_C_VC).astype(_BF16)
    ks_ref[...] = proj(_C_KS).astype(_BF16)
    kw_ref[...] = proj(_C_KW).astype(_BF16)
    tm = x.shape[0]
    ones_pad = (lax.broadcasted_iota(jnp.int32, (VT_ROWS - NSA_DH, tm), 0) == 0).astype(_F32)

    def values_t(c):
        vt = proj(c).T
        parts = []
        for h in range(NSA_KV_HEADS):
            parts += [vt[h * NSA_DH:(h + 1) * NSA_DH, :], ones_pad]
        return jnp.concatenate(parts, axis=0).astype(_BF16)

    vst_ref[0] = values_t(_C_VS)
    vwt_ref[0] = values_t(_C_VW)
    nz_ref[...] = proj(_C_NZ)


def _in_proj(x2d, norm_w, w_cat, batch, seq, tm):
    n = x2d.shape[0]
    tiles_per_seq = seq // tm
    row = lambda w: pl.BlockSpec((tm, w), lambda i: (i, 0))
    tr_map = lambda i: (i // tiles_per_seq, 0, i % tiles_per_seq)
    tr = pl.BlockSpec((1, LANES, tm), tr_map)
    trv = pl.BlockSpec((1, NSA_KV_HEADS * VT_ROWS, tm), tr_map)
    sds = jax.ShapeDtypeStruct
    return pl.pallas_call(
        _in_proj_kernel,
        out_shape=(
            sds((n, GDN_CONV_W), _F32), sds((n, GDN_W), _F32), sds((n, LANES), _F32),
            sds((batch, LANES, seq), _F32), sds((n, NSA_W), _BF16), sds((n, KV_W), _BF16),
            sds((n, KV_W), _BF16), sds((n, KV_W), _BF16), sds((n, KV_W), _BF16),
            sds((batch, NSA_KV_HEADS * VT_ROWS, seq), _BF16), sds((batch, NSA_KV_HEADS * VT_ROWS, seq), _BF16),
            sds((n, NSA_W), _F32)),
        grid=(n // tm,),
        in_specs=[row(D_MODEL), pl.BlockSpec((1, D_MODEL), lambda i: (0, 0)),
                  pl.BlockSpec((D_MODEL, _W_COLS), lambda i: (0, 0))],
        out_specs=(row(GDN_CONV_W), row(GDN_W), row(LANES), tr, row(NSA_W), row(KV_W), row(KV_W),
                   row(KV_W), row(KV_W), trv, trv, row(NSA_W)),
        compiler_params=pltpu.CompilerParams(dimension_semantics=("arbitrary",),
                                             vmem_limit_bytes=VMEM_LIMIT),
        name="in_proj",
    )(x2d, norm_w.reshape(1, D_MODEL).astype(_F32), w_cat)


_GDN_PAIR = 2 * CHUNK
_STACK = GDN_HEADS * CHUNK
_PREP_UNROLL = 4


def _segment_cumsum(g, axis):
    pos = lax.broadcasted_iota(jnp.int32, g.shape, axis) % CHUNK
    k = 1
    while k < CHUNK:
        g = g + jnp.where(pos >= k, pltpu.roll(g, k, axis), 0.0)
        k *= 2
    return g


def _stacked_unit_lower_inverses(lows):
    r = lax.broadcasted_iota(jnp.int32, (_STACK, _STACK), 0)
    c = lax.broadcasted_iota(jnp.int32, (_STACK, _STACK), 1)
    eye = (r == c).astype(_F32)
    ms = [-low for low in lows]
    xs = [eye + m for m in ms]
    ms = [_dot(m, m) for m in ms]
    span = 4
    while span < CHUNK:
        boths = [_dot(jnp.concatenate([x, m], axis=0), m) for x, m in zip(xs, ms)]
        xs = [x + both[:_STACK] for x, both in zip(xs, boths)]
        ms = [both[_STACK:] for both in boths]
        span *= 2
    return [x + _dot(x, m) for x, m in zip(xs, ms)]


def _gdn_kernel(x_ref, cw_ref, sm_ref, smt_ref, prow_ref, pcol_ref, gz_ref, gnw_ref, o_ref,
                xs_ref, q_ref, k_ref, v_ref, col_ref, grow_ref, state_ref,
                u_ref, w_ref, qd_ref, kd_ref, a_ref, egl_ref, oraw_ref, *, tile):
    t_idx = pl.program_id(1)

    @pl.when(t_idx == 0)
    def _():
        xs_ref[0:8, :] = jnp.zeros((8, GDN_CONV_W), _F32)
        state_ref[...] = jnp.zeros_like(state_ref)

    @pl.when(t_idx != 0)
    def _():
        xs_ref[0:8, :] = xs_ref[tile:tile + 8, :]

    xs_ref[8:tile + 8, :] = x_ref[0]

    y = jnp.zeros((tile, GDN_CONV_W), _F32)
    for tap in range(CONV_K):
        off = 8 - (CONV_K - 1) + tap
        y = y + cw_ref[tap:tap + 1, :] * xs_ref[off:off + tile, :]
    y = _silu(y)

    for h in range(GDN_HEADS):
        lo = h * GDN_DK
        qh = y[:, lo:lo + GDN_DK]
        kh = y[:, GDN_QK_W + lo:GDN_QK_W + lo + GDN_DK]
        q_ref[:, lo:lo + GDN_DK] = qh * lax.rsqrt(jnp.sum(qh * qh, axis=-1, keepdims=True) + EPS) * GDN_DK ** -0.5
        k_ref[:, lo:lo + GDN_DK] = kh * lax.rsqrt(jnp.sum(kh * kh, axis=-1, keepdims=True) + EPS)
    v_ref[...] = y[:, 2 * GDN_QK_W:]

    sm = sm_ref[0]
    lane = lax.broadcasted_iota(jnp.int32, sm.shape, 1)
    zc = sm + prow_ref[1:2, :]
    gcol = -jnp.exp(prow_ref[0:1, :]) * (jnp.maximum(zc, 0.0) + jnp.log1p(jnp.exp(-jnp.abs(zc))))
    gcol = _segment_cumsum(gcol, 0)
    col_ref[...] = jnp.where(lane < GDN_HEADS, _sigmoid(sm), gcol)
    zr = smt_ref[0, 0:8, :] + pcol_ref[1]
    grow = -jnp.exp(pcol_ref[0]) * (jnp.maximum(zr, 0.0) + jnp.log1p(jnp.exp(-jnp.abs(zr))))
    grow_ref[...] = _segment_cumsum(grow, 1)

    r = lax.broadcasted_iota(jnp.int32, (_STACK, _STACK), 0)
    c = lax.broadcasted_iota(jnp.int32, (_STACK, _STACK), 1)
    same_head = (r // CHUNK) == (c // CHUNK)
    tril = same_head & (r >= c)
    strict = same_head & (r > c)
    heads = range(GDN_HEADS)

    def prep_gram(ci, half):
        rows = pl.ds(pl.multiple_of(ci * CHUNK, CHUNK), CHUNK)
        slab = pl.ds(pl.multiple_of((ci // 2) * _GDN_PAIR, _GDN_PAIR), _GDN_PAIR)
        colv = col_ref[rows, :]
        grow2 = grow_ref[:, slab]

        def stack(ref, width):
            return jnp.concatenate([ref[rows, h * width:(h + 1) * width] for h in heads], axis=0)

        q = stack(q_ref, GDN_DK)
        k = stack(k_ref, GDN_DK)
        v = stack(v_ref, GDN_DV)
        beta = jnp.concatenate([colv[:, h:h + 1] for h in heads], axis=0)
        gc = jnp.concatenate([colv[:, GDN_HEADS + h:GDN_HEADS + h + 1] for h in heads], axis=0)
        gr = jnp.concatenate([grow2[GDN_HEADS + h:GDN_HEADS + h + 1, half * CHUNK:(half + 1) * CHUNK]
                              for h in heads], axis=1)
        last = [colv[CHUNK - 1:CHUNK, GDN_HEADS + h:GDN_HEADS + h + 1] for h in heads]
        g_last = jnp.concatenate([jnp.broadcast_to(x, (CHUNK, 1)) for x in last], axis=0)
        decay = jnp.where(tril, jnp.exp(jnp.where(tril, gc - gr, 0.0)), 0.0)
        kb = k * beta
        eg = jnp.exp(gc)
        low = jnp.where(strict, _dot_nt(kb, k) * decay, 0.0)
        a_ref[ci] = jnp.where(tril, _dot_nt(q, k) * decay, 0.0)
        kd_ref[ci] = k * jnp.exp(g_last - gc)
        qd_ref[ci] = q * eg
        egl_ref[ci] = jnp.concatenate([jnp.broadcast_to(jnp.exp(x), (1, GDN_DV)) for x in last]
                                      + [jnp.zeros((8 - GDN_HEADS, GDN_DV), _F32)], axis=0)
        return low, jnp.concatenate([v * beta, kb * eg], axis=1)

    def prep_body(pi, carry):
        cis = [pi * _PREP_UNROLL + j for j in range(_PREP_UNROLL)]
        lows, rhss = zip(*[prep_gram(ci, j % 2) for j, ci in enumerate(cis)])
        tinvs = _stacked_unit_lower_inverses(lows)
        sols = [_dot(tinv, rhs) for tinv, rhs in zip(tinvs, rhss)]
        for ci, sol in zip(cis, sols):
            u_ref[ci] = sol[:, :GDN_DV]
            w_ref[ci] = sol[:, GDN_DV:]
        return carry

    n_chunks = tile // CHUNK
    lax.fori_loop(0, n_chunks // _PREP_UNROLL, prep_body, 0)

    states = [state_ref[h] for h in heads]
    for ci in range(n_chunks):
        u, w, qd, kd, egl = u_ref[ci], w_ref[ci], qd_ref[ci], kd_ref[ci], egl_ref[ci]
        v_news, inters = [], []
        for h in heads:
            rs = slice(h * CHUNK, (h + 1) * CHUNK)
            wq = _dot(jnp.concatenate([w[rs], qd[rs]], axis=0), states[h])
            v_new = u[rs] - wq[:CHUNK]
            states[h] = states[h] * egl[h:h + 1, :] + _dot_tn(kd[rs], v_new)
            v_news.append(v_new)
            inters.append(wq[CHUNK:])
        o = jnp.concatenate(inters, axis=0) + _dot(a_ref[ci], jnp.concatenate(v_news, axis=0))
        for h in heads:
            oraw_ref[ci * CHUNK:(ci + 1) * CHUNK, h * GDN_DV:(h + 1) * GDN_DV] = o[h * CHUNK:(h + 1) * CHUNK]
    for h in heads:
        state_ref[h] = states[h]

    gnw = gnw_ref[...]
    for h in heads:
        lanes = slice(h * GDN_DV, (h + 1) * GDN_DV)
        o = oraw_ref[:, lanes]
        o = o * lax.rsqrt(jnp.mean(o * o, axis=-1, keepdims=True) + EPS) * gnw
        o_ref[0, :, lanes] = o * _silu(gz_ref[0, :, lanes])


def _gdn(conv_in, conv_w, sm, smt, a_log, dt_bias, gz, gdn_norm_w, tile):
    batch, seq, _ = conv_in.shape
    prow = jnp.zeros((2, LANES), _F32)
    prow = prow.at[0, GDN_HEADS:2 * GDN_HEADS].set(a_log.astype(_F32))
    prow = prow.at[1, GDN_HEADS:2 * GDN_HEADS].set(dt_bias.astype(_F32))
    pcol = jnp.broadcast_to(prow[:, :8, None], (2, 8, tile))
    n_chunks = tile // CHUNK
    assert n_chunks % _PREP_UNROLL == 0 and _PREP_UNROLL % 2 == 0
    kern = functools.partial(_gdn_kernel, tile=tile)
    return pl.pallas_call(
        kern,
        out_shape=jax.ShapeDtypeStruct((batch, seq, GDN_W), _F32),
        grid=(batch, seq // tile),
        in_specs=[
            pl.BlockSpec((1, tile, GDN_CONV_W), lambda b, t: (b, t, 0)),
            pl.BlockSpec((CONV_K, GDN_CONV_W), lambda b, t: (0, 0)),
            pl.BlockSpec((1, tile, LANES), lambda b, t: (b, t, 0)),
            pl.BlockSpec((1, LANES, tile), lambda b, t: (b, 0, t)),
            pl.BlockSpec((2, LANES), lambda b, t: (0, 0)),
            pl.BlockSpec((2, 8, tile), lambda b, t: (0, 0, 0)),
            pl.BlockSpec((1, tile, GDN_W), lambda b, t: (b, t, 0)),
            pl.BlockSpec((1, GDN_DV), lambda b, t: (0, 0)),
        ],
        out_specs=pl.BlockSpec((1, tile, GDN_W), lambda b, t: (b, t, 0)),
        scratch_shapes=[
            pltpu.VMEM((tile + 8, GDN_CONV_W), _F32),
            pltpu.VMEM((tile, GDN_QK_W), _F32),
            pltpu.VMEM((tile, GDN_QK_W), _F32),
            pltpu.VMEM((tile, GDN_W), _F32),
            pltpu.VMEM((tile, LANES), _F32),
            pltpu.VMEM((8, tile), _F32),
            pltpu.VMEM((GDN_HEADS, GDN_DK, GDN_DV), _F32),
            pltpu.VMEM((n_chunks, _STACK, GDN_DV), _F32),
            pltpu.VMEM((n_chunks, _STACK, GDN_DV), _F32),
            pltpu.VMEM((n_chunks, _STACK, GDN_DK), _F32),
            pltpu.VMEM((n_chunks, _STACK, GDN_DK), _F32),
            pltpu.VMEM((n_chunks, _STACK, _STACK), _F32),
            pltpu.VMEM((n_chunks, 8, GDN_DV), _F32),
            pltpu.VMEM((tile, GDN_W), _F32),
        ],
        compiler_params=pltpu.CompilerParams(dimension_semantics=("arbitrary", "arbitrary"),
                                             vmem_limit_bytes=VMEM_LIMIT),
        name="gdn",
    )(conv_in, conv_w.astype(_F32), sm, smt, prow, pcol, gz, gdn_norm_w.reshape(1, GDN_DV).astype(_F32))


_GROUP_W = CMP_STRIDE * KV_W
_HID2 = NSA_KV_HEADS * CMP_HIDDEN


def _compress_weights(pe, w1, w2):
    eye = jnp.eye(NSA_KV_HEADS, dtype=_F32)
    halves = []
    pes = []
    for part in range(CMP_LEN // CMP_STRIDE):
        w1p = w1[part * CMP_STRIDE:(part + 1) * CMP_STRIDE].astype(_F32)
        big = jnp.einsum('ldf,hg->lhdgf', w1p, eye).reshape(_GROUP_W, _HID2)
        halves.append(big.astype(_BF16))
        pep = pe[part * CMP_STRIDE:(part + 1) * CMP_STRIDE].astype(_F32)
        pes.append(jnp.broadcast_to(pep[:, None, :], (CMP_STRIDE, NSA_KV_HEADS, NSA_DH)).reshape(1, _GROUP_W))
    w2big = jnp.einsum('fd,hg->hfgd', w2.astype(_F32), eye).reshape(_HID2, KV_W).astype(_BF16)
    pe2 = jnp.broadcast_to(jnp.stack(pes, 0), (2, 8, _GROUP_W)).astype(_BF16)
    return halves[0], halves[1], w2big, pe2


def _compress_kernel(kc_ref, vc_ref, kwa_ref, kwb_ref, kw2_ref, kpe_ref, vwa_ref, vwb_ref, vw2_ref, vpe_ref,
                     ko_ref, vo_ref):
    def mlp(g_ref, wa_ref, wb_ref, w2_ref, pe_ref):
        g = g_ref[0]
        ncp = g.shape[0]
        p0 = _dot(g, wa_ref[...])
        p1 = _dot(g, wb_ref[...])
        const = _dot(pe_ref[0], wa_ref[...]) + _dot(pe_ref[1], wb_ref[...])
        pre = p0 + pltpu.roll(p1, ncp - 1, 0) + const[0:1, :]
        return _dot(_silu(pre).astype(_BF16), w2_ref[...])

    ko_ref[0] = mlp(kc_ref, kwa_ref, kwb_ref, kw2_ref, kpe_ref).astype(_BF16)
    vo_ref[0] = mlp(vc_ref, vwa_ref, vwb_ref, vw2_ref, vpe_ref).T.astype(_BF16)


def _compress(kc, vc, kweights, vweights):
    batch, ncp, _ = kc.shape
    full = lambda a: pl.BlockSpec(a.shape, lambda b: (0,) * a.ndim)
    per_b = pl.BlockSpec((1, ncp, _GROUP_W), lambda b: (b, 0, 0))
    return pl.pallas_call(
        _compress_kernel,
        out_shape=(jax.ShapeDtypeStruct((batch, ncp, KV_W), _BF16),
                   jax.ShapeDtypeStruct((batch, KV_W, ncp), _BF16)),
        grid=(batch,),
        in_specs=[per_b, per_b] + [full(a) for a in kweights] + [full(a) for a in vweights],
        out_specs=(pl.BlockSpec((1, ncp, KV_W), lambda b: (b, 0, 0)),
                   pl.BlockSpec((1, KV_W, ncp), lambda b: (b, 0, 0))),
        compiler_params=pltpu.CompilerParams(dimension_semantics=("arbitrary",),
                                             vmem_limit_bytes=VMEM_LIMIT),
        name="compress",
    )(kc, vc, *kweights, *vweights)


def _overlap_t(n_sel, ncp, n_cmp):
    cs = CMP_STRIDE * np.arange(ncp)
    ss = SEL_BLOCK * np.arange(n_sel)
    ov = ((cs[None, :] < ss[:, None] + SEL_BLOCK) & (cs[None, :] + CMP_LEN > ss[:, None])
          & (np.arange(ncp)[None, :] < n_cmp))
    return jnp.asarray(ov.astype(np.float32), dtype=_BF16)


def _key_onehots(n_tiles, with_blocks):
    key = np.arange(n_tiles * TQ)[:, None]
    col = np.arange(LANES)[None, :]
    ek = np.broadcast_to((col == 8) | (col == 9), (n_tiles * TQ, LANES))
    if with_blocks:
        ek = ek | (col == key // SEL_BLOCK)
    return jnp.asarray(ek.astype(np.float32), dtype=_BF16)


def _nsa_kernel(q_ref, ks_ref, kw_ref, vst_ref, vwt_ref, kc_ref, vct_ref, tab_ref, far_ref, ovt_ref, eks_ref,
                ekw_ref, smt_ref, nz_ref, o_ref, cb_ref, sc_ref, cnt_ref, mb_ref, ss_ref, st_ref, ps_ref, ocw_ref):
    a = pl.program_id(1)
    ncp = kc_ref.shape[1]
    n_sel = ovt_ref.shape[0]
    heads = range(NSA_KV_HEADS)
    drows = [slice(h * NSA_DH, (h + 1) * NSA_DH) for h in heads]
    vrows = [slice(h * VT_ROWS, (h + 1) * VT_ROWS) for h in heads]
    qcols = [slice(h * GQ_W, (h + 1) * GQ_W) for h in heads]

    qrow = lax.broadcasted_iota(jnp.int32, (KV_W, QL), 0)

    def transposed_queries(h):
        qt = (q_ref[0, :, qcols[h]].astype(_F32) * (NSA_DH ** -0.5 * LOG2E)).T
        qt4 = jnp.concatenate([qt[g * NSA_DH:(g + 1) * NSA_DH, :] for g in range(NSA_GROUP)], axis=1)
        return jnp.where(qrow // NSA_DH == h, jnp.concatenate([qt4] * NSA_KV_HEADS, axis=0), 0.0).astype(_BF16)

    qt_b = [transposed_queries(h) for h in heads]
    far_rows = [far_ref[h] for h in heads]
    zero_rows = jnp.zeros((LANES - 16, QL), _BF16)

    def augmented_queries(h, mask_rows):
        extra = jnp.concatenate([mask_rows, far_rows[h]], axis=0).astype(_BF16)
        return jnp.concatenate([qt_b[h], extra, zero_rows], axis=0)

    def augmented_keys(k_ref, ek, key0, n_tiles):
        keys = pl.ds(pl.multiple_of(key0, TQ), n_tiles * TQ)
        return jnp.concatenate([k_ref[0, keys, :], ek], axis=1)

    def gate(h, branch):
        first = 8 + 8 * branch + NSA_GROUP * h
        return _sigmoid(jnp.concatenate([smt_ref[0, first + g:first + g + 1, :] for g in range(NSA_GROUP)], axis=1))

    def normalized(acc):
        return acc[0:NSA_DH, :] * (1.0 / acc[NSA_DH:NSA_DH + 1, :])

    win_t0 = jnp.maximum(a - (WIN_TILES - 1), 0)
    win_keys = pl.ds(pl.multiple_of(win_t0 * TQ, TQ), WIN_TILES * TQ)
    k_aug_w = augmented_keys(kw_ref, ekw_ref[...], win_t0 * TQ, WIN_TILES)
    no_mask = jnp.zeros((8, QL), _F32)
    s_w = [_dot(k_aug_w, augmented_queries(h, no_mask)) for h in heads]

    win_start = pl.multiple_of(jnp.maximum(8 * a - CMP_NEAR_BACK, 0), 8)
    tab_start = pl.multiple_of(win_start - (8 * a - CMP_NEAR_BACK), 8)
    crow = lax.broadcasted_iota(jnp.int32, (ncp, QL), 0)
    for h in heads:
        cb_ref[h] = jnp.where(crow < win_start, far_rows[h][2:3, :], NEG)
        cb_ref[h, pl.ds(win_start, CMP_NEAR), :] = tab_ref[h, TAB_CMP, pl.ds(tab_start, CMP_NEAR), :]
    k_cmp = kc_ref[0]
    s_c = [_dot(k_cmp, qt_b[h]) + cb_ref[h] for h in heads]

    win_entries = []
    for i in range(WIN_TILES):
        d = a - (win_t0 + i)
        win_entries.append(jnp.where(d < 0, TAB_MASKED, jnp.where(d == WIN_TILES - 1, TAB_WIN_EDGE, d)))
    s_w = [jnp.concatenate([s_w[h][i * TQ:(i + 1) * TQ, :] + tab_ref[h, win_entries[i]]
                            for i in range(WIN_TILES)], axis=0) for h in heads]
    m_w = [jnp.max(s, axis=0, keepdims=True) for s in s_w]
    acc_w = [_dot(vwt_ref[0, vrows[h], win_keys], jnp.exp2((s_w[h] - m_w[h]).astype(_BF16))) for h in heads]

    m_c = [jnp.max(s, axis=0, keepdims=True) for s in s_c]
    e_c = [jnp.exp2(s - m) for s, m in zip(s_c, m_c)]
    l_c = [jnp.sum(e, axis=0, keepdims=True) for e in e_c]
    p_c = [e * jnp.where(m > 0.5 * NEG, 1.0 / l, 0.0) for e, m, l in zip(e_c, m_c, l_c)]
    o_c = [_dot(vct_ref[0, drows[h], :], p_c[h].astype(_BF16)) for h in heads]
    for h in heads:
        ocw_ref[h] = gate(h, 0) * o_c[h] + gate(h, 2) * normalized(acc_w[h])

    ovt = ovt_ref[...]
    jrow = lax.broadcasted_iota(jnp.int32, (n_sel, TQ), 0)
    cur = (a * TQ + lax.broadcasted_iota(jnp.int32, (n_sel, TQ), 1)) // SEL_BLOCK
    forced = (jrow == 0) | (jrow == cur) | (jrow == cur - 1)
    for h in heads:
        psum = p_c[h][:, 0:TQ]
        for g in range(1, NSA_GROUP):
            psum = psum + p_c[h][:, g * TQ:(g + 1) * TQ]
        hi = psum.astype(_BF16)
        r1 = psum - hi.astype(_F32)
        mid = r1.astype(_BF16)
        lo = (r1 - mid.astype(_F32)).astype(_BF16)
        imp = _dot(ovt, hi) + _dot(ovt, mid) + _dot(ovt, lo)
        sc_ref[h] = jnp.where(jrow <= cur, imp + jnp.where(forced, FORCE_BONUS, 0.0), NEG)

    cnt_ref[...] = jnp.zeros(cnt_ref.shape, _F32)
    sub = lax.broadcasted_iota(jnp.int32, (8, TQ), 0)
    last_valid = (TQ // SEL_BLOCK) * (a + 1) - 1
    row_groups = range(n_sel // 8)
    for chunk in range(n_sel // RANK_CHUNK):
        @pl.when(chunk * RANK_CHUNK <= last_valid)
        def _():
            for h in heads:
                rows = [sc_ref[h, 8 * r:8 * r + 8, :] for r in row_groups]
                cnts = [cnt_ref[h, 8 * r:8 * r + 8, :] for r in row_groups]
                for jp in range(chunk * RANK_CHUNK, (chunk + 1) * RANK_CHUNK):
                    srow = sc_ref[h, jp:jp + 1, :]
                    for r in row_groups:
                        if 8 * r > jp:
                            inc = jnp.where(srow >= rows[r], 1.0, 0.0)
                        elif 8 * r + 7 < jp:
                            inc = jnp.where(srow > rows[r], 1.0, 0.0)
                        else:
                            tie = jnp.where(sub + 8 * r > jp, 1.0, 0.0)
                            inc = jnp.where(srow > rows[r], 1.0, jnp.where(srow >= rows[r], tie, 0.0))
                        cnts[r] = cnts[r] + inc
                for r in row_groups:
                    cnt_ref[h, 8 * r:8 * r + 8, :] = cnts[r]

    for h in heads:
        mb = jnp.where(cnt_ref[h] < SEL_TOP_N, 0.0, NEG)
        mb_ref[h] = jnp.concatenate([mb] * NSA_GROUP, axis=1)

    eks = eks_ref[...]
    blocks_per_group = GROUP_TILES * (TQ // SEL_BLOCK)
    group_keys = GROUP_TILES * TQ
    last_group = a // GROUP_TILES

    def sel_scores(gi):
        k_aug = augmented_keys(ks_ref, eks, gi * group_keys, GROUP_TILES)
        mask_rows = pl.ds(pl.multiple_of(gi * blocks_per_group, 8), blocks_per_group)
        return [_dot(k_aug, augmented_queries(h, mb_ref[h, mask_rows, :])) for h in heads]

    def sel_values(h, gi):
        return vst_ref[0, vrows[h], pl.ds(pl.multiple_of(gi * group_keys, group_keys), group_keys)]

    def sel_step(gi, carry, near, last):
        s_next = None if last else sel_scores(gi + 1)
        pv_prev = [_dot(sel_values(h, jnp.maximum(gi - 1, 0)), ps_ref[h]) for h in heads]
        entries = []
        if near:
            for i in range(GROUP_TILES):
                d = a - (gi * GROUP_TILES + i)
                entries.append(jnp.where(d < 0, TAB_MASKED, jnp.where(d >= N_NEAR, TAB_ZERO, d)))
        new_carry = []
        for h in heads:
            m, alpha_prev, acc = carry[h]
            s = ss_ref[h]
            if near:
                s = jnp.concatenate([s[i * TQ:(i + 1) * TQ, :] + tab_ref[h, entries[i]]
                                     for i in range(GROUP_TILES)], axis=0)
                st_ref[h] = s
            m_new = jnp.maximum(m, jnp.max(s, axis=0, keepdims=True))
            alpha = jnp.exp2(m - m_new)
            acc = alpha_prev * acc + pv_prev[h]
            s = st_ref[h] if near else ss_ref[h]
            ps_ref[h] = jnp.exp2((s - m_new).astype(_BF16))
            new_carry.append((m_new, alpha, acc))
        if not last:
            for h in heads:
                ss_ref[h] = s_next[h]
        return tuple(new_carry)

    s_first = sel_scores(0)
    for h in heads:
        ss_ref[h] = s_first[h]
    ps_ref[...] = jnp.zeros(ps_ref.shape, _BF16)
    carry = tuple((jnp.full((1, QL), NEG, _F32), jnp.ones((1, QL), _F32), jnp.zeros((VT_ROWS, QL), _F32))
                  for _ in heads)
    n_far = jnp.maximum(last_group - N_NEAR // GROUP_TILES, 0)
    carry = lax.fori_loop(0, n_far, functools.partial(sel_step, near=False, last=False), carry)
    carry = lax.fori_loop(n_far, last_group, functools.partial(sel_step, near=True, last=False), carry)
    carry = sel_step(last_group, carry, near=True, last=True)

    for h in heads:
        _, alpha_s, acc_s = carry[h]
        acc_s = alpha_s * acc_s + _dot(sel_values(h, last_group), ps_ref[h])
        o_t = ocw_ref[h] + gate(h, 1) * normalized(acc_s)
        stacked = jnp.concatenate([o_t[:, g * TQ:(g + 1) * TQ] for g in range(NSA_GROUP)], axis=0)
        o_ref[0, :, qcols[h]] = stacked.T * _silu(nz_ref[0, :, qcols[h]])


def _nsa(nq, ks, kw, vst, vwt, kcmp, vcmpt, tab, far, ovt, smt, nz):
    batch, seq, _ = nq.shape
    ncp = kcmp.shape[1]
    n_sel = seq // SEL_BLOCK
    per_b = lambda shape: pl.BlockSpec((1,) + shape, lambda b, a: (b, 0, 0))
    const = lambda shape: pl.BlockSpec(shape, lambda b, a: (0,) * len(shape), pipeline_mode=pl.Buffered(1))
    tile = lambda width: pl.BlockSpec((1, TQ, width), lambda b, a: (b, a, 0))
    vt_shape = (NSA_KV_HEADS * VT_ROWS, seq)
    per_head = lambda shape, dtype: pltpu.VMEM((NSA_KV_HEADS,) + shape, dtype)
    return pl.pallas_call(
        _nsa_kernel,
        out_shape=jax.ShapeDtypeStruct((batch, seq, NSA_W), _F32),
        grid=(batch, seq // TQ),
        in_specs=[
            tile(NSA_W),
            per_b((seq, KV_W)), per_b((seq, KV_W)), per_b(vt_shape), per_b(vt_shape),
            per_b((ncp, KV_W)), per_b((KV_W, ncp)),
            const((NSA_KV_HEADS, N_TAB, TQ, QL)), const((NSA_KV_HEADS, 8, QL)),
            const((n_sel, ncp)), const((GROUP_TILES * TQ, LANES)), const((WIN_TILES * TQ, LANES)),
            pl.BlockSpec((1, LANES, TQ), lambda b, a: (b, 0, a)),
            tile(NSA_W),
        ],
        out_specs=tile(NSA_W),
        scratch_shapes=[per_head((ncp, QL), _F32), per_head((n_sel, TQ), _F32),
                        per_head((n_sel, TQ), _F32), per_head((n_sel, QL), _F32),
                        per_head((GROUP_TILES * TQ, QL), _F32), per_head((GROUP_TILES * TQ, QL), _F32),
                        per_head((GROUP_TILES * TQ, QL), _BF16), per_head((NSA_DH, QL), _F32)],
        compiler_params=pltpu.CompilerParams(dimension_semantics=("arbitrary", "arbitrary"),
                                             vmem_limit_bytes=VMEM_LIMIT),
        name="nsa",
    )(nq, ks, kw, vst, vwt, kcmp, vcmpt, tab, far, ovt, _key_onehots(GROUP_TILES, True),
      _key_onehots(WIN_TILES, False), smt, nz)


def _out_proj_kernel(og_ref, on_ref, x_ref, w_ref, fw_ref, o_ref):
    h = (x_ref[...] + _dot(og_ref[...].astype(_BF16), w_ref[0:GDN_W, :])
         + _dot(on_ref[...].astype(_BF16), w_ref[GDN_W:, :]))
    o_ref[...] = h * lax.rsqrt(jnp.mean(h * h, axis=-1, keepdims=True) + EPS) * fw_ref[...]


def _out_proj(o_g, o_n, x2d, w_out, final_norm_w, tm):
    n = x2d.shape[0]
    row = lambda w: pl.BlockSpec((tm, w), lambda i: (i, 0))
    return pl.pallas_call(
        _out_proj_kernel,
        out_shape=jax.ShapeDtypeStruct((n, D_MODEL), _F32),
        grid=(n // tm,),
        in_specs=[row(GDN_W), row(NSA_W), row(D_MODEL),
                  pl.BlockSpec((GDN_W + NSA_W, D_MODEL), lambda i: (0, 0)),
                  pl.BlockSpec((1, D_MODEL), lambda i: (0, 0))],
        out_specs=row(D_MODEL),
        compiler_params=pltpu.CompilerParams(dimension_semantics=("arbitrary",),
                                             vmem_limit_bytes=VMEM_LIMIT),
        name="out_proj",
    )(o_g, o_n, x2d, w_out.astype(_BF16), final_norm_w.reshape(1, D_MODEL).astype(_F32))


def kernel(x, norm_w, w_in, conv_w, a_log, dt_bias, gdn_norm_w, cmp_pe_k, cmp_pe_v, cmp_k_w1, cmp_k_w2,
           cmp_v_w1, cmp_v_w2, w_out, rel_bias, final_norm_w):
    batch, seq, _ = x.shape
    assert w_in.shape[0] == 1, "single-layer problem"
    assert seq % (8 * TQ) == 0
    n = batch * seq
    ncp = seq // CMP_STRIDE
    n_cmp = (seq - CMP_LEN) // CMP_STRIDE + 1
    tm = min(512, seq)

    x2d = x.reshape(n, D_MODEL)
    tab, far = _bias_tables(rel_bias)
    (conv_in, gz, sm, smt, nq, kc, vc, ks, kw, vst, vwt, nz) = _in_proj(
        x2d, norm_w[0], _regroup_w_in(w_in[0]), batch, seq, tm)

    o_g = _gdn(conv_in.reshape(batch, seq, GDN_CONV_W), conv_w[0], sm.reshape(batch, seq, LANES), smt,
               a_log[0], dt_bias[0], gz.reshape(batch, seq, GDN_W), gdn_norm_w[0], tm)

    kcmp, vcmpt = _compress(kc.reshape(batch, ncp, _GROUP_W), vc.reshape(batch, ncp, _GROUP_W),
                            _compress_weights(cmp_pe_k[0], cmp_k_w1[0], cmp_k_w2[0]),
                            _compress_weights(cmp_pe_v[0], cmp_v_w1[0], cmp_v_w2[0]))

    o_n = _nsa(nq.reshape(batch, seq, NSA_W), ks.reshape(batch, seq, KV_W), kw.reshape(batch, seq, KV_W),
               vst, vwt, kcmp, vcmpt, tab, far, _overlap_t(seq // SEL_BLOCK, ncp, n_cmp), smt,
               nz.reshape(batch, seq, NSA_W))

    out = _out_proj(o_g.reshape(n, GDN_W), o_n.reshape(n, NSA_W), x2d, w_out[0], final_norm_w, tm)
    return out.reshape(batch, seq, D_MODEL)
```

```python
import functools
import math

import jax
import jax.numpy as jnp
import numpy as np
from jax import lax
from jax.experimental import pallas as pl
from jax.experimental.pallas import tpu as pltpu

D_MODEL = 1024
GDN_HEADS = 4
GDN_DK = 128
GDN_DV = 128
CONV_K = 4
CHUNK = 64
NSA_HEADS = 8
NSA_KV_HEADS = 2
NSA_GROUP = NSA_HEADS // NSA_KV_HEADS
NSA_DH = 64
CMP_LEN = 32
CMP_STRIDE = 16
CMP_HIDDEN = 128
SEL_BLOCK = 64
SEL_TOP_N = 16
WINDOW = 512
FORCE_BONUS = 1e4
N_BUCKETS = 32
MAX_DISTANCE = 1024
EPS = 1e-6
NEG = -1e30

GDN_QK_W = GDN_HEADS * GDN_DK
GDN_W = GDN_HEADS * GDN_DV
GDN_CONV_W = 2 * GDN_QK_W + GDN_W
NSA_W = NSA_HEADS * NSA_DH
KV_W = NSA_KV_HEADS * NSA_DH
PROJ_SIZES = (GDN_QK_W, GDN_QK_W, GDN_W, GDN_W, GDN_HEADS, GDN_HEADS,
              NSA_W, KV_W, KV_W, KV_W, KV_W, KV_W, KV_W, 3 * NSA_HEADS, NSA_W)

LANES = 128
TQ = 128
QL = NSA_GROUP * TQ
GQ_W = NSA_GROUP * NSA_DH
N_NEAR = 8
TAB_ZERO = N_NEAR
TAB_MASKED = N_NEAR + 1
TAB_WIN_EDGE = N_NEAR + 2
TAB_CMP = N_NEAR + 3
N_TAB = N_NEAR + 4
GROUP_TILES = 4
WIN_TILES = WINDOW // TQ + 1
RANK_CHUNK = 16
RANK_ROWS = 32
VT_ROWS = 80
LOG2E = math.log2(math.e)
CMP_NEAR = 64
CMP_NEAR_BACK = 56
VMEM_LIMIT = 56 * 1024 * 1024

_F32 = jnp.float32
_BF16 = jnp.bfloat16


def _dot(a, b):
    return jnp.dot(a, b, preferred_element_type=_F32)


def _dot_nt(a, b):
    return lax.dot_general(a, b, (((1,), (1,)), ((), ())), preferred_element_type=_F32)


def _dot_tn(a, b):
    return lax.dot_general(a, b, (((0,), (0,)), ((), ())), preferred_element_type=_F32)


def _sigmoid(x):
    return 0.5 * jnp.tanh(0.5 * x) + 0.5


def _silu(x):
    return x * _sigmoid(x)


def _bias_tab_kernel(rel_ref, o_ref, far_ref):
    hk = pl.program_id(0)
    e = pl.program_id(1)
    is_cmp = e == TAB_CMP
    is_edge = e == TAB_WIN_EDGE
    base = jnp.where(e < N_NEAR, e * TQ,
                     jnp.where(is_edge, WINDOW, CMP_STRIDE * CMP_NEAR_BACK - (CMP_LEN - 1)))
    row_scale = jnp.where(is_cmp, CMP_STRIDE, 1)
    upper = jnp.where(is_edge, WINDOW, 1 << 30)
    row = lax.broadcasted_iota(jnp.int32, (TQ, TQ), 0)
    tok = lax.broadcasted_iota(jnp.int32, (TQ, TQ), 1)
    dist = base + tok - row_scale * row
    n = jnp.maximum(dist, 0)
    max_exact = N_BUCKETS // 2
    large = max_exact + (jnp.log(jnp.maximum(n, max_exact).astype(_F32) / max_exact)
                         / math.log(MAX_DISTANCE / max_exact) * (N_BUCKETS - max_exact)).astype(jnp.int32)
    bucket = jnp.where(n < max_exact, n, jnp.minimum(large, N_BUCKETS - 1))
    ok = (dist >= 0) & (dist < upper)
    for g in range(NSA_GROUP):
        head = hk * NSA_GROUP + g
        lanes = slice(g * TQ, (g + 1) * TQ)
        far = jnp.full((8, TQ), rel_ref[N_BUCKETS - 1, head] * LOG2E, _F32)
        far_hi = far.astype(_BF16).astype(_F32)
        far_lo = (far - far_hi).astype(_BF16).astype(_F32)
        val = jnp.zeros((TQ, TQ), _F32)
        for b in range(N_BUCKETS):
            val = jnp.where(bucket == b, rel_ref[b, head], val)
        val = val * LOG2E - jnp.where(is_cmp, 0.0, far_hi[0:1, :] + far_lo[0:1, :])
        val = jnp.where(ok, val, NEG)
        val = jnp.where(e == TAB_ZERO, 0.0, jnp.where(e == TAB_MASKED, NEG, val))
        o_ref[0, 0, :, lanes] = val

        sub = lax.broadcasted_iota(jnp.int32, (8, TQ), 0)
        far_ref[0, :, lanes] = jnp.where(sub == 0, far_hi, jnp.where(sub == 1, far_lo,
                                                                     jnp.where(sub == 2, far, 0.0)))


def _bias_tables(rel_bias):
    return pl.pallas_call(
        _bias_tab_kernel,
        out_shape=(jax.ShapeDtypeStruct((NSA_KV_HEADS, N_TAB, TQ, QL), _F32),
                   jax.ShapeDtypeStruct((NSA_KV_HEADS, 8, QL), _F32)),
        grid=(NSA_KV_HEADS, N_TAB),
        in_specs=[pl.BlockSpec(memory_space=pltpu.SMEM)],
        out_specs=(pl.BlockSpec((1, 1, TQ, QL), lambda h, e: (h, e, 0, 0)),
                   pl.BlockSpec((1, 8, QL), lambda h, e: (h, 0, 0))),
        name="bias_tables",
    )(rel_bias.astype(_F32))


_C_CONV = (0, GDN_CONV_W)
_C_GZ = (_C_CONV[1], _C_CONV[1] + GDN_W)
_C_NQ = (_C_GZ[1], _C_GZ[1] + NSA_W)
_C_NZ = (_C_NQ[1], _C_NQ[1] + NSA_W)
_C_SMALL_KC = (_C_NZ[1], _C_NZ[1] + 2 * LANES)
_C_VC_KS = (_C_SMALL_KC[1], _C_SMALL_KC[1] + 2 * LANES)
_C_KW_VS = (_C_VC_KS[1], _C_VC_KS[1] + 2 * LANES)
_C_VW = (_C_KW_VS[1], _C_KW_VS[1] + LANES)
_W_COLS = _C_VW[1]
assert KV_W == LANES


def _regroup_w_in(w):
    gq, gk, gv, gz, gb, ga, nq, kc, vc, ks, vs, kw, vw, ng, nz = jnp.split(
        w, [int(p) for p in np.cumsum(PROJ_SIZES)[:-1]], axis=-1)
    small = jnp.concatenate([gb, ga, ng], axis=-1)
    small = jnp.pad(small, ((0, 0), (0, LANES - small.shape[-1])))
    return jnp.concatenate([gq, gk, gv, gz, nq, nz, small, kc, vc, ks, kw, vs, vw], axis=-1).astype(_BF16)


def _in_proj_kernel(x_ref, nw_ref, w_ref, conv_ref, gz_ref, sm_ref, smt_ref, nq_ref, kc_ref, vc_ref,
                    ks_ref, kw_ref, vst_ref, vwt_ref, nz_ref):
    x = x_ref[...]
    u = x * lax.rsqrt(jnp.mean(x * x, axis=-1, keepdims=True) + EPS) * nw_ref[...]
    u = u.astype(_BF16)

    def proj(c):
        return _dot(u, w_ref[:, c[0]:c[1]])

    def proj_pair(c):
        both = proj(c)
        return both[:, :LANES], both[:, LANES:]

    conv_ref[...] = proj(_C_CONV)
    gz_ref[...] = proj(_C_GZ)
    nq_ref[...] = proj(_C_NQ).astype(_BF16)
    nz_ref[...] = proj(_C_NZ)
    sm, kc = proj_pair(_C_SMALL_KC)
    vc, ks = proj_pair(_C_VC_KS)
    kw, vs = proj_pair(_C_KW_VS)
    sm_ref[...] = sm
    smt_ref[0] = sm.T
    kc_ref[...] = kc.astype(_BF16)
    vc_ref[...] = vc.astype(_BF16)
    ks_ref[...] = ks.astype(_BF16)
    kw_ref[...] = kw.astype(_BF16)
    tm = x.shape[0]
    ones_pad = (lax.broadcasted_iota(jnp.int32, (VT_ROWS - NSA_DH, tm), 0) == 0).astype(_F32)

    def values_t(v):
        vt = v.T
        parts = []
        for h in range(NSA_KV_HEADS):
            parts += [vt[h * NSA_DH:(h + 1) * NSA_DH, :], ones_pad]
        return jnp.concatenate(parts, axis=0).astype(_BF16)

    vst_ref[0] = values_t(vs)
    vwt_ref[0] = values_t(proj(_C_VW))


def _in_proj(x2d, norm_w, w_cat, batch, seq, tm):
    n = x2d.shape[0]
    tiles_per_seq = seq // tm
    row = lambda w: pl.BlockSpec((tm, w), lambda i: (i, 0))
    tr_map = lambda i: (i // tiles_per_seq, 0, i % tiles_per_seq)
    tr = pl.BlockSpec((1, LANES, tm), tr_map)
    trv = pl.BlockSpec((1, NSA_KV_HEADS * VT_ROWS, tm), tr_map)
    sds = jax.ShapeDtypeStruct
    return pl.pallas_call(
        _in_proj_kernel,
        out_shape=(
            sds((n, GDN_CONV_W), _F32), sds((n, GDN_W), _F32), sds((n, LANES), _F32),
            sds((batch, LANES, seq), _F32), sds((n, NSA_W), _BF16), sds((n, KV_W), _BF16),
            sds((n, KV_W), _BF16), sds((n, KV_W), _BF16), sds((n, KV_W), _BF16),
            sds((batch, NSA_KV_HEADS * VT_ROWS, seq), _BF16), sds((batch, NSA_KV_HEADS * VT_ROWS, seq), _BF16),
            sds((n, NSA_W), _F32)),
        grid=(n // tm,),
        in_specs=[row(D_MODEL), pl.BlockSpec((1, D_MODEL), lambda i: (0, 0)),
                  pl.BlockSpec((D_MODEL, _W_COLS), lambda i: (0, 0))],
        out_specs=(row(GDN_CONV_W), row(GDN_W), row(LANES), tr, row(NSA_W), row(KV_W), row(KV_W),
                   row(KV_W), row(KV_W), trv, trv, row(NSA_W)),
        compiler_params=pltpu.CompilerParams(dimension_semantics=("arbitrary",),
                                             vmem_limit_bytes=VMEM_LIMIT),
        name="in_proj",
    )(x2d, norm_w.reshape(1, D_MODEL).astype(_F32), w_cat)


_GDN_PAIR = 2 * CHUNK
_STACK = GDN_HEADS * CHUNK
_PREP_UNROLL = 4


def _segment_cumsum(g, axis):
    pos = lax.broadcasted_iota(jnp.int32, g.shape, axis) % CHUNK
    k = 1
    while k < CHUNK:
        g = g + jnp.where(pos >= k, pltpu.roll(g, k, axis), 0.0)
        k *= 2
    return g


def _stacked_unit_lower_inverses(lows):
    r = lax.broadcasted_iota(jnp.int32, (_STACK, _STACK), 0)
    c = lax.broadcasted_iota(jnp.int32, (_STACK, _STACK), 1)
    eye = (r == c).astype(_F32)
    ms = [-low for low in lows]
    xs = [eye + m for m in ms]
    ms = [_dot(m, m) for m in ms]
    span = 4
    while span < CHUNK:
        boths = [_dot(jnp.concatenate([x, m], axis=0), m) for x, m in zip(xs, ms)]
        xs = [x + both[:_STACK] for x, both in zip(xs, boths)]
        ms = [both[_STACK:] for both in boths]
        span *= 2
    return [x + _dot(x, m) for x, m in zip(xs, ms)]


def _gdn_kernel(x_ref, cw_ref, sm_ref, smt_ref, prow_ref, pcol_ref, gz_ref, gnw_ref, o_ref,
                xs_ref, q_ref, k_ref, v_ref, col_ref, grow_ref, state_ref,
                u_ref, w_ref, qd_ref, kd_ref, a_ref, egl_ref, oraw_ref, *, tile):
    t_idx = pl.program_id(1)

    @pl.when(t_idx == 0)
    def _():
        xs_ref[0:8, :] = jnp.zeros((8, GDN_CONV_W), _F32)
        state_ref[...] = jnp.zeros_like(state_ref)

    @pl.when(t_idx != 0)
    def _():
        xs_ref[0:8, :] = xs_ref[tile:tile + 8, :]

    xs_ref[8:tile + 8, :] = x_ref[0]

    y = jnp.zeros((tile, GDN_CONV_W), _F32)
    for tap in range(CONV_K):
        off = 8 - (CONV_K - 1) + tap
        y = y + cw_ref[tap:tap + 1, :] * xs_ref[off:off + tile, :]
    y = _silu(y)

    for h in range(GDN_HEADS):
        lo = h * GDN_DK
        qh = y[:, lo:lo + GDN_DK]
        kh = y[:, GDN_QK_W + lo:GDN_QK_W + lo + GDN_DK]
        q_ref[:, lo:lo + GDN_DK] = qh * lax.rsqrt(jnp.sum(qh * qh, axis=-1, keepdims=True) + EPS) * GDN_DK ** -0.5
        k_ref[:, lo:lo + GDN_DK] = kh * lax.rsqrt(jnp.sum(kh * kh, axis=-1, keepdims=True) + EPS)
    v_ref[...] = y[:, 2 * GDN_QK_W:]

    sm = sm_ref[0]
    lane = lax.broadcasted_iota(jnp.int32, sm.shape, 1)
    zc = sm + prow_ref[1:2, :]
    gcol = -jnp.exp(prow_ref[0:1, :]) * (jnp.maximum(zc, 0.0) + jnp.log1p(jnp.exp(-jnp.abs(zc))))
    gcol = _segment_cumsum(gcol, 0)
    col_ref[...] = jnp.where(lane < GDN_HEADS, _sigmoid(sm), gcol)
    zr = smt_ref[0, 0:8, :] + pcol_ref[1]
    grow = -jnp.exp(pcol_ref[0]) * (jnp.maximum(zr, 0.0) + jnp.log1p(jnp.exp(-jnp.abs(zr))))
    grow_ref[...] = _segment_cumsum(grow, 1)

    r = lax.broadcasted_iota(jnp.int32, (_STACK, _STACK), 0)
    c = lax.broadcasted_iota(jnp.int32, (_STACK, _STACK), 1)
    same_head = (r // CHUNK) == (c // CHUNK)
    tril = same_head & (r >= c)
    strict = same_head & (r > c)
    heads = range(GDN_HEADS)

    def prep_gram(ci, half):
        rows = pl.ds(pl.multiple_of(ci * CHUNK, CHUNK), CHUNK)
        slab = pl.ds(pl.multiple_of((ci // 2) * _GDN_PAIR, _GDN_PAIR), _GDN_PAIR)
        colv = col_ref[rows, :]
        grow2 = grow_ref[:, slab]

        def stack(ref, width):
            return jnp.concatenate([ref[rows, h * width:(h + 1) * width] for h in heads], axis=0)

        q = stack(q_ref, GDN_DK)
        k = stack(k_ref, GDN_DK)
        v = stack(v_ref, GDN_DV)
        beta = jnp.concatenate([colv[:, h:h + 1] for h in heads], axis=0)
        gc = jnp.concatenate([colv[:, GDN_HEADS + h:GDN_HEADS + h + 1] for h in heads], axis=0)
        gr = jnp.concatenate([grow2[GDN_HEADS + h:GDN_HEADS + h + 1, half * CHUNK:(half + 1) * CHUNK]
                              for h in heads], axis=1)
        last = [colv[CHUNK - 1:CHUNK, GDN_HEADS + h:GDN_HEADS + h + 1] for h in heads]
        g_last = jnp.concatenate([jnp.broadcast_to(x, (CHUNK, 1)) for x in last], axis=0)
        decay = jnp.where(tril, jnp.exp(jnp.where(tril, gc - gr, 0.0)), 0.0)
        kb = k * beta
        eg = jnp.exp(gc)
        low = jnp.where(strict, _dot_nt(kb, k) * decay, 0.0)
        a_ref[ci] = jnp.where(tril, _dot_nt(q, k) * decay, 0.0)
        kd_ref[ci] = k * jnp.exp(g_last - gc)
        qd_ref[ci] = q * eg
        egl_ref[ci] = jnp.concatenate([jnp.broadcast_to(jnp.exp(x), (1, GDN_DV)) for x in last]
                                      + [jnp.zeros((8 - GDN_HEADS, GDN_DV), _F32)], axis=0)
        return low, jnp.concatenate([v * beta, kb * eg], axis=1)

    def prep_body(pi, carry):
        cis = [pi * _PREP_UNROLL + j for j in range(_PREP_UNROLL)]
        lows, rhss = zip(*[prep_gram(ci, j % 2) for j, ci in enumerate(cis)])
        tinvs = _stacked_unit_lower_inverses(lows)
        sols = [_dot(tinv, rhs) for tinv, rhs in zip(tinvs, rhss)]
        for ci, sol in zip(cis, sols):
            u_ref[ci] = sol[:, :GDN_DV]
            w_ref[ci] = sol[:, GDN_DV:]
        return carry

    n_chunks = tile // CHUNK
    lax.fori_loop(0, n_chunks // _PREP_UNROLL, prep_body, 0)

    states = [state_ref[h] for h in heads]
    for ci in range(n_chunks):
        u, w, qd, kd, egl = u_ref[ci], w_ref[ci], qd_ref[ci], kd_ref[ci], egl_ref[ci]
        v_news, inters = [], []
        for h in heads:
            rs = slice(h * CHUNK, (h + 1) * CHUNK)
            wq = _dot(jnp.concatenate([w[rs], qd[rs]], axis=0), states[h])
            v_new = u[rs] - wq[:CHUNK]
            states[h] = states[h] * egl[h:h + 1, :] + _dot_tn(kd[rs], v_new)
            v_news.append(v_new)
            inters.append(wq[CHUNK:])
        o = jnp.concatenate(inters, axis=0) + _dot(a_ref[ci], jnp.concatenate(v_news, axis=0))
        for h in heads:
            oraw_ref[ci * CHUNK:(ci + 1) * CHUNK, h * GDN_DV:(h + 1) * GDN_DV] = o[h * CHUNK:(h + 1) * CHUNK]
    for h in heads:
        state_ref[h] = states[h]

    gnw = gnw_ref[...]
    for h in heads:
        lanes = slice(h * GDN_DV, (h + 1) * GDN_DV)
        o = oraw_ref[:, lanes]
        o = o * lax.rsqrt(jnp.mean(o * o, axis=-1, keepdims=True) + EPS) * gnw
        o_ref[0, :, lanes] = (o * _silu(gz_ref[0, :, lanes])).astype(_BF16)


def _gdn(conv_in, conv_w, sm, smt, a_log, dt_bias, gz, gdn_norm_w, tile):
    batch, seq, _ = conv_in.shape
    prow = jnp.zeros((2, LANES), _F32)
    prow = prow.at[0, GDN_HEADS:2 * GDN_HEADS].set(a_log.astype(_F32))
    prow = prow.at[1, GDN_HEADS:2 * GDN_HEADS].set(dt_bias.astype(_F32))
    pcol = jnp.broadcast_to(prow[:, :8, None], (2, 8, tile))
    n_chunks = tile // CHUNK
    assert n_chunks % _PREP_UNROLL == 0 and _PREP_UNROLL % 2 == 0
    kern = functools.partial(_gdn_kernel, tile=tile)
    return pl.pallas_call(
        kern,
        out_shape=jax.ShapeDtypeStruct((batch, seq, GDN_W), _BF16),
        grid=(batch, seq // tile),
        in_specs=[
            pl.BlockSpec((1, tile, GDN_CONV_W), lambda b, t: (b, t, 0)),
            pl.BlockSpec((CONV_K, GDN_CONV_W), lambda b, t: (0, 0)),
            pl.BlockSpec((1, tile, LANES), lambda b, t: (b, t, 0)),
            pl.BlockSpec((1, LANES, tile), lambda b, t: (b, 0, t)),
            pl.BlockSpec((2, LANES), lambda b, t: (0, 0)),
            pl.BlockSpec((2, 8, tile), lambda b, t: (0, 0, 0)),
            pl.BlockSpec((1, tile, GDN_W), lambda b, t: (b, t, 0)),
            pl.BlockSpec((1, GDN_DV), lambda b, t: (0, 0)),
        ],
        out_specs=pl.BlockSpec((1, tile, GDN_W), lambda b, t: (b, t, 0)),
        scratch_shapes=[
            pltpu.VMEM((tile + 8, GDN_CONV_W), _F32),
            pltpu.VMEM((tile, GDN_QK_W), _F32),
            pltpu.VMEM((tile, GDN_QK_W), _F32),
            pltpu.VMEM((tile, GDN_W), _F32),
            pltpu.VMEM((tile, LANES), _F32),
            pltpu.VMEM((8, tile), _F32),
            pltpu.VMEM((GDN_HEADS, GDN_DK, GDN_DV), _F32),
            pltpu.VMEM((n_chunks, _STACK, GDN_DV), _F32),
            pltpu.VMEM((n_chunks, _STACK, GDN_DV), _F32),
            pltpu.VMEM((n_chunks, _STACK, GDN_DK), _F32),
            pltpu.VMEM((n_chunks, _STACK, GDN_DK), _F32),
            pltpu.VMEM((n_chunks, _STACK, _STACK), _F32),
            pltpu.VMEM((n_chunks, 8, GDN_DV), _F32),
            pltpu.VMEM((tile, GDN_W), _F32),
        ],
        compiler_params=pltpu.CompilerParams(dimension_semantics=("arbitrary", "arbitrary"),
                                             vmem_limit_bytes=VMEM_LIMIT),
        name="gdn",
    )(conv_in, conv_w.astype(_F32), sm, smt, prow, pcol, gz, gdn_norm_w.reshape(1, GDN_DV).astype(_F32))


_GROUP_W = CMP_STRIDE * KV_W
_HID2 = NSA_KV_HEADS * CMP_HIDDEN


def _compress_weights(pe, w1, w2):
    eye = jnp.eye(NSA_KV_HEADS, dtype=_F32)
    halves = []
    pes = []
    for part in range(CMP_LEN // CMP_STRIDE):
        w1p = w1[part * CMP_STRIDE:(part + 1) * CMP_STRIDE].astype(_F32)
        big = jnp.einsum('ldf,hg->lhdgf', w1p, eye).reshape(_GROUP_W, _HID2)
        halves.append(big.astype(_BF16))
        pep = pe[part * CMP_STRIDE:(part + 1) * CMP_STRIDE].astype(_F32)
        pes.append(jnp.broadcast_to(pep[:, None, :], (CMP_STRIDE, NSA_KV_HEADS, NSA_DH)).reshape(1, _GROUP_W))
    w2big = jnp.einsum('fd,hg->hfgd', w2.astype(_F32), eye).reshape(_HID2, KV_W).astype(_BF16)
    pe2 = jnp.broadcast_to(jnp.stack(pes, 0), (2, 8, _GROUP_W)).astype(_BF16)
    return halves[0], halves[1], w2big, pe2


def _compress_kernel(kc_ref, vc_ref, kwa_ref, kwb_ref, kw2_ref, kpe_ref, vwa_ref, vwb_ref, vw2_ref, vpe_ref,
                     ko_ref, vo_ref):
    def mlp(g_ref, wa_ref, wb_ref, w2_ref, pe_ref):
        g = g_ref[0]
        ncp = g.shape[0]
        p0 = _dot(g, wa_ref[...])
        p1 = _dot(g, wb_ref[...])
        const = _dot(pe_ref[0], wa_ref[...]) + _dot(pe_ref[1], wb_ref[...])
        pre = p0 + pltpu.roll(p1, ncp - 1, 0) + const[0:1, :]
        return _dot(_silu(pre).astype(_BF16), w2_ref[...])

    ko_ref[0] = mlp(kc_ref, kwa_ref, kwb_ref, kw2_ref, kpe_ref).astype(_BF16)
    vo_ref[0] = mlp(vc_ref, vwa_ref, vwb_ref, vw2_ref, vpe_ref).T.astype(_BF16)


def _compress(kc, vc, kweights, vweights):
    batch, ncp, _ = kc.shape
    full = lambda a: pl.BlockSpec(a.shape, lambda b: (0,) * a.ndim)
    per_b = pl.BlockSpec((1, ncp, _GROUP_W), lambda b: (b, 0, 0))
    return pl.pallas_call(
        _compress_kernel,
        out_shape=(jax.ShapeDtypeStruct((batch, ncp, KV_W), _BF16),
                   jax.ShapeDtypeStruct((batch, KV_W, ncp), _BF16)),
        grid=(batch,),
        in_specs=[per_b, per_b] + [full(a) for a in kweights] + [full(a) for a in vweights],
        out_specs=(pl.BlockSpec((1, ncp, KV_W), lambda b: (b, 0, 0)),
                   pl.BlockSpec((1, KV_W, ncp), lambda b: (b, 0, 0))),
        compiler_params=pltpu.CompilerParams(dimension_semantics=("arbitrary",),
                                             vmem_limit_bytes=VMEM_LIMIT),
        name="compress",
    )(kc, vc, *kweights, *vweights)


def _overlap_t(n_sel, ncp, n_cmp):
    cs = CMP_STRIDE * np.arange(ncp)
    ss = SEL_BLOCK * np.arange(n_sel)
    ov = ((cs[None, :] < ss[:, None] + SEL_BLOCK) & (cs[None, :] + CMP_LEN > ss[:, None])
          & (np.arange(ncp)[None, :] < n_cmp))
    return jnp.asarray(ov.astype(np.float32), dtype=_BF16)


def _key_onehots(n_tiles, with_blocks):
    key = np.arange(n_tiles * TQ)[:, None]
    col = np.arange(LANES)[None, :]
    ek = np.broadcast_to((col == 8) | (col == 9), (n_tiles * TQ, LANES))
    if with_blocks:
        ek = ek | (col == key // SEL_BLOCK)
    return jnp.asarray(ek.astype(np.float32), dtype=_BF16)


def _nsa_kernel(q_ref, ks_ref, kw_ref, vst_ref, vwt_ref, kc_ref, vct_ref, tab_ref, far_ref, ovt_ref, eks_ref,
                ekw_ref, smt_ref, nz_ref, o_ref, cb_ref, sc_ref, cnt_ref, mb_ref, ss_ref, st_ref, ps_ref, ocw_ref):
    a = pl.program_id(1)
    ncp = kc_ref.shape[1]
    n_sel = ovt_ref.shape[0]
    heads = range(NSA_KV_HEADS)
    drows = [slice(h * NSA_DH, (h + 1) * NSA_DH) for h in heads]
    vrows = [slice(h * VT_ROWS, (h + 1) * VT_ROWS) for h in heads]
    qcols = [slice(h * GQ_W, (h + 1) * GQ_W) for h in heads]

    qrow = lax.broadcasted_iota(jnp.int32, (KV_W, QL), 0)

    def transposed_queries(h):
        qt = (q_ref[0, :, qcols[h]].astype(_F32) * (NSA_DH ** -0.5 * LOG2E)).T
        qt4 = jnp.concatenate([qt[g * NSA_DH:(g + 1) * NSA_DH, :] for g in range(NSA_GROUP)], axis=1)
        return jnp.where(qrow // NSA_DH == h, jnp.concatenate([qt4] * NSA_KV_HEADS, axis=0), 0.0).astype(_BF16)

    qt_b = [transposed_queries(h) for h in heads]
    far_rows = [far_ref[h] for h in heads]
    zero_rows = jnp.zeros((LANES - 16, QL), _BF16)

    def augmented_queries(h, mask_rows):
        extra = jnp.concatenate([mask_rows, far_rows[h]], axis=0).astype(_BF16)
        return jnp.concatenate([qt_b[h], extra, zero_rows], axis=0)

    def augmented_keys(k_ref, ek, key0, n_tiles):
        keys = pl.ds(pl.multiple_of(key0, TQ), n_tiles * TQ)
        return jnp.concatenate([k_ref[0, keys, :], ek], axis=1)

    def gate(h, branch):
        first = 8 + 8 * branch + NSA_GROUP * h
        return _sigmoid(jnp.concatenate([smt_ref[0, first + g:first + g + 1, :] for g in range(NSA_GROUP)], axis=1))

    def normalized(acc):
        return acc[0:NSA_DH, :] * (1.0 / acc[NSA_DH:NSA_DH + 1, :])

    win_t0 = jnp.maximum(a - (WIN_TILES - 1), 0)
    win_keys = pl.ds(pl.multiple_of(win_t0 * TQ, TQ), WIN_TILES * TQ)
    win_start = pl.multiple_of(jnp.maximum(8 * a - CMP_NEAR_BACK, 0), 8)
    tab_start = pl.multiple_of(win_start - (8 * a - CMP_NEAR_BACK), 8)

    def compressed_scores(n_rows):
        crow = lax.broadcasted_iota(jnp.int32, (n_rows, QL), 0)
        for h in heads:
            cb_ref[h, 0:n_rows, :] = jnp.where(crow < win_start, far_rows[h][2:3, :], NEG)
            cb_ref[h, pl.ds(win_start, CMP_NEAR), :] = tab_ref[h, TAB_CMP, pl.ds(tab_start, CMP_NEAR), :]
        k_cmp = kc_ref[0, 0:n_rows, :]
        return [_dot(k_cmp, qt_b[h]) + cb_ref[h, 0:n_rows, :] for h in heads]

    def compressed_branch(s_c, acc_w):
        n_rows = s_c[0].shape[0]
        m_c = [jnp.max(s, axis=0, keepdims=True) for s in s_c]
        e_c = [jnp.exp2(s - m) for s, m in zip(s_c, m_c)]
        l_c = [jnp.sum(e, axis=0, keepdims=True) for e in e_c]
        p_c = [e * jnp.where(m > 0.5 * NEG, 1.0 / l, 0.0) for e, m, l in zip(e_c, m_c, l_c)]
        o_c = [_dot(vct_ref[0, drows[h], 0:n_rows], p_c[h].astype(_BF16)) for h in heads]
        for h in heads:
            ocw_ref[h] = gate(h, 0) * o_c[h] + gate(h, 2) * normalized(acc_w[h])
        ovt = ovt_ref[:, 0:n_rows]
        jrow = lax.broadcasted_iota(jnp.int32, (n_sel, TQ), 0)
        cur = (a * TQ + lax.broadcasted_iota(jnp.int32, (n_sel, TQ), 1)) // SEL_BLOCK
        forced = (jrow == 0) | (jrow == cur) | (jrow == cur - 1)
        for h in heads:
            psum = p_c[h][:, 0:TQ]
            for g in range(1, NSA_GROUP):
                psum = psum + p_c[h][:, g * TQ:(g + 1) * TQ]
            hi = psum.astype(_BF16)
            r1 = psum - hi.astype(_F32)
            mid = r1.astype(_BF16)
            lo = (r1 - mid.astype(_F32)).astype(_BF16)
            imp = _dot(ovt, hi) + _dot(ovt, mid) + _dot(ovt, lo)
            sc_ref[h] = jnp.where(jrow <= cur, imp + jnp.where(forced, FORCE_BONUS, 0.0), NEG)

    win_entries = []
    for i in range(WIN_TILES):
        d = a - (win_t0 + i)
        win_entries.append(jnp.where(d < 0, TAB_MASKED, jnp.where(d == WIN_TILES - 1, TAB_WIN_EDGE, d)))
    def front_stage(n_rows):
        k_aug_w = augmented_keys(kw_ref, ekw_ref[...], win_t0 * TQ, WIN_TILES)
        no_mask = jnp.zeros((8, QL), _F32)
        s_w = [_dot(k_aug_w, augmented_queries(h, no_mask)) for h in heads]
        s_c = compressed_scores(n_rows)
        s_wb = [jnp.concatenate([s_w[h][i * TQ:(i + 1) * TQ, :] + tab_ref[h, win_entries[i]]
                                 for i in range(WIN_TILES)], axis=0) for h in heads]
        m_w = [jnp.max(s, axis=0, keepdims=True) for s in s_wb]
        acc_w = [_dot(vwt_ref[0, vrows[h], win_keys], jnp.exp2((s_wb[h] - m_w[h]).astype(_BF16))) for h in heads]
        compressed_branch(s_c, acc_w)

    half_rows = ncp // 2
    needs_all = 8 * a + 8 > half_rows

    @pl.when(jnp.logical_not(needs_all))
    def _():
        front_stage(half_rows)

    @pl.when(needs_all)
    def _():
        front_stage(ncp)

    cnt_ref[...] = jnp.zeros(cnt_ref.shape, _F32)
    sub = lax.broadcasted_iota(jnp.int32, (8, TQ), 0)
    last_valid = (TQ // SEL_BLOCK) * (a + 1) - 1
    for chunk in range(n_sel // RANK_CHUNK):
        for row_chunk in range(n_sel // RANK_ROWS):
            row_groups = range(row_chunk * RANK_ROWS // 8, (row_chunk + 1) * RANK_ROWS // 8)

            @pl.when((chunk * RANK_CHUNK <= last_valid) & (row_chunk * RANK_ROWS <= last_valid))
            def _():
                for h in heads:
                    rows = {r: sc_ref[h, 8 * r:8 * r + 8, :] for r in row_groups}
                    cnts = {r: cnt_ref[h, 8 * r:8 * r + 8, :] for r in row_groups}
                    for jp in range(chunk * RANK_CHUNK, (chunk + 1) * RANK_CHUNK):
                        srow = sc_ref[h, jp:jp + 1, :]
                        for r in row_groups:
                            if 8 * r > jp:
                                inc = jnp.where(srow >= rows[r], 1.0, 0.0)
                            elif 8 * r + 7 < jp:
                                inc = jnp.where(srow > rows[r], 1.0, 0.0)
                            else:
                                tie = jnp.where(sub + 8 * r > jp, 1.0, 0.0)
                                inc = jnp.where(srow > rows[r], 1.0, jnp.where(srow >= rows[r], tie, 0.0))
                            cnts[r] = cnts[r] + inc
                    for r in row_groups:
                        cnt_ref[h, 8 * r:8 * r + 8, :] = cnts[r]

    for h in heads:
        mb = jnp.where(cnt_ref[h] < SEL_TOP_N, 0.0, NEG)
        mb_ref[h] = jnp.concatenate([mb] * NSA_GROUP, axis=1)

    eks = eks_ref[...]
    blocks_per_group = GROUP_TILES * (TQ // SEL_BLOCK)
    group_keys = GROUP_TILES * TQ
    last_group = a // GROUP_TILES

    def sel_scores(gi):
        k_aug = augmented_keys(ks_ref, eks, gi * group_keys, GROUP_TILES)
        mask_rows = pl.ds(pl.multiple_of(gi * blocks_per_group, 8), blocks_per_group)
        return [_dot(k_aug, augmented_queries(h, mb_ref[h, mask_rows, :])) for h in heads]

    def sel_values(h, gi):
        return vst_ref[0, vrows[h], pl.ds(pl.multiple_of(gi * group_keys, group_keys), group_keys)]

    def sel_step(gi, carry, near, last):
        s_next = None if last else sel_scores(gi + 1)
        pv_prev = [_dot(sel_values(h, jnp.maximum(gi - 1, 0)), ps_ref[h]) for h in heads]
        entries = []
        if near:
            for i in range(GROUP_TILES):
                d = a - (gi * GROUP_TILES + i)
                entries.append(jnp.where(d < 0, TAB_MASKED, jnp.where(d >= N_NEAR, TAB_ZERO, d)))
        new_carry = []
        for h in heads:
            m, alpha_prev, acc = carry[h]
            s = ss_ref[h]
            if near:
                s = jnp.concatenate([s[i * TQ:(i + 1) * TQ, :] + tab_ref[h, entries[i]]
                                     for i in range(GROUP_TILES)], axis=0)
                st_ref[h] = s
            m_new = jnp.maximum(m, jnp.max(s, axis=0, keepdims=True))
            alpha = jnp.exp2(m - m_new)
            acc = alpha_prev * acc + pv_prev[h]
            s = st_ref[h] if near else ss_ref[h]
            ps_ref[h] = jnp.exp2((s - m_new).astype(_BF16))
            new_carry.append((m_new, alpha, acc))
        if not last:
            for h in heads:
                ss_ref[h] = s_next[h]
        return tuple(new_carry)

    s_first = sel_scores(0)
    for h in heads:
        ss_ref[h] = s_first[h]
    ps_ref[...] = jnp.zeros(ps_ref.shape, _BF16)
    carry = tuple((jnp.full((1, QL), NEG, _F32), jnp.ones((1, QL), _F32), jnp.zeros((VT_ROWS, QL), _F32))
                  for _ in heads)
    n_far = jnp.maximum(last_group - N_NEAR // GROUP_TILES, 0)
    carry = lax.fori_loop(0, n_far, functools.partial(sel_step, near=False, last=False), carry)
    carry = lax.fori_loop(n_far, last_group, functools.partial(sel_step, near=True, last=False), carry)
    carry = sel_step(last_group, carry, near=True, last=True)

    for h in heads:
        _, alpha_s, acc_s = carry[h]
        acc_s = alpha_s * acc_s + _dot(sel_values(h, last_group), ps_ref[h])
        o_t = ocw_ref[h] + gate(h, 1) * normalized(acc_s)
        stacked = jnp.concatenate([o_t[:, g * TQ:(g + 1) * TQ] for g in range(NSA_GROUP)], axis=0)
        o_ref[0, :, qcols[h]] = (stacked.T * _silu(nz_ref[0, :, qcols[h]])).astype(_BF16)


def _nsa(nq, ks, kw, vst, vwt, kcmp, vcmpt, tab, far, ovt, smt, nz):
    batch, seq, _ = nq.shape
    ncp = kcmp.shape[1]
    n_sel = seq // SEL_BLOCK
    per_b = lambda shape: pl.BlockSpec((1,) + shape, lambda b, a: (b, 0, 0))
    const = lambda shape: pl.BlockSpec(shape, lambda b, a: (0,) * len(shape), pipeline_mode=pl.Buffered(1))
    tile = lambda width: pl.BlockSpec((1, TQ, width), lambda b, a: (b, a, 0))
    vt_shape = (NSA_KV_HEADS * VT_ROWS, seq)
    per_head = lambda shape, dtype: pltpu.VMEM((NSA_KV_HEADS,) + shape, dtype)
    return pl.pallas_call(
        _nsa_kernel,
        out_shape=jax.ShapeDtypeStruct((batch, seq, NSA_W), _BF16),
        grid=(batch, seq // TQ),
        in_specs=[
            tile(NSA_W),
            per_b((seq, KV_W)), per_b((seq, KV_W)), per_b(vt_shape), per_b(vt_shape),
            per_b((ncp, KV_W)), per_b((KV_W, ncp)),
            const((NSA_KV_HEADS, N_TAB, TQ, QL)), const((NSA_KV_HEADS, 8, QL)),
            const((n_sel, ncp)), const((GROUP_TILES * TQ, LANES)), const((WIN_TILES * TQ, LANES)),
            pl.BlockSpec((1, LANES, TQ), lambda b, a: (b, 0, a)),
            tile(NSA_W),
        ],
        out_specs=tile(NSA_W),
        scratch_shapes=[per_head((ncp, QL), _F32), per_head((n_sel, TQ), _F32),
                        per_head((n_sel, TQ), _F32), per_head((n_sel, QL), _F32),
                        per_head((GROUP_TILES * TQ, QL), _F32), per_head((GROUP_TILES * TQ, QL), _F32),
                        per_head((GROUP_TILES * TQ, QL), _BF16), per_head((NSA_DH, QL), _F32)],
        compiler_params=pltpu.CompilerParams(dimension_semantics=("arbitrary", "arbitrary"),
                                             vmem_limit_bytes=VMEM_LIMIT),
        name="nsa",
    )(nq, ks, kw, vst, vwt, kcmp, vcmpt, tab, far, ovt, _key_onehots(GROUP_TILES, True),
      _key_onehots(WIN_TILES, False), smt, nz)


def _out_proj_kernel(og_ref, on_ref, x_ref, w_ref, fw_ref, o_ref):
    h = x_ref[...] + _dot(og_ref[...], w_ref[0:GDN_W, :]) + _dot(on_ref[...], w_ref[GDN_W:, :])
    o_ref[...] = h * lax.rsqrt(jnp.mean(h * h, axis=-1, keepdims=True) + EPS) * fw_ref[...]


def _out_proj(o_g, o_n, x2d, w_out, final_norm_w, tm):
    n = x2d.shape[0]
    row = lambda w: pl.BlockSpec((tm, w), lambda i: (i, 0))
    return pl.pallas_call(
        _out_proj_kernel,
        out_shape=jax.ShapeDtypeStruct((n, D_MODEL), _F32),
        grid=(n // tm,),
        in_specs=[row(GDN_W), row(NSA_W), row(D_MODEL),
                  pl.BlockSpec((GDN_W + NSA_W, D_MODEL), lambda i: (0, 0)),
                  pl.BlockSpec((1, D_MODEL), lambda i: (0, 0))],
        out_specs=row(D_MODEL),
        compiler_params=pltpu.CompilerParams(dimension_semantics=("arbitrary",),
                                             vmem_limit_bytes=VMEM_LIMIT),
        name="out_proj",
    )(o_g, o_n, x2d, w_out.astype(_BF16), final_norm_w.reshape(1, D_MODEL).astype(_F32))


def kernel(x, norm_w, w_in, conv_w, a_log, dt_bias, gdn_norm_w, cmp_pe_k, cmp_pe_v, cmp_k_w1, cmp_k_w2,
           cmp_v_w1, cmp_v_w2, w_out, rel_bias, final_norm_w):
    batch, seq, _ = x.shape
    assert w_in.shape[0] == 1, "single-layer problem"
    assert seq % (8 * TQ) == 0
    n = batch * seq
    ncp = seq // CMP_STRIDE
    n_cmp = (seq - CMP_LEN) // CMP_STRIDE + 1
    tm = min(512, seq)

    x2d = x.reshape(n, D_MODEL)
    tab, far = _bias_tables(rel_bias)
    (conv_in, gz, sm, smt, nq, kc, vc, ks, kw, vst, vwt, nz) = _in_proj(
        x2d, norm_w[0], _regroup_w_in(w_in[0]), batch, seq, tm)

    o_g = _gdn(conv_in.reshape(batch, seq, GDN_CONV_W), conv_w[0], sm.reshape(batch, seq, LANES), smt,
               a_log[0], dt_bias[0], gz.reshape(batch, seq, GDN_W), gdn_norm_w[0], tm)

    kcmp, vcmpt = _compress(kc.reshape(batch, ncp, _GROUP_W), vc.reshape(batch, ncp, _GROUP_W),
                            _compress_weights(cmp_pe_k[0], cmp_k_w1[0], cmp_k_w2[0]),
                            _compress_weights(cmp_pe_v[0], cmp_v_w1[0], cmp_v_w2[0]))

    o_n = _nsa(nq.reshape(batch, seq, NSA_W), ks.reshape(batch, seq, KV_W), kw.reshape(batch, seq, KV_W),
               vst, vwt, kcmp, vcmpt, tab, far, _overlap_t(seq // SEL_BLOCK, ncp, n_cmp), smt,
               nz.reshape(batch, seq, NSA_W))

    out = _out_proj(o_g.reshape(n, GDN_W), o_n.reshape(n, NSA_W), x2d, w_out[0], final_norm_w, tm)
    return out.reshape(batch, seq, D_MODEL)
```

```python
import functools
import math

import jax
import jax.numpy as jnp
import numpy as np
from jax import lax
from jax.experimental import pallas as pl
from jax.experimental.pallas import tpu as pltpu

D_MODEL = 1024
GDN_HEADS = 4
GDN_DK = 128
GDN_DV = 128
CONV_K = 4
CHUNK = 64
NSA_HEADS = 8
NSA_KV_HEADS = 2
NSA_GROUP = NSA_HEADS // NSA_KV_HEADS
NSA_DH = 64
CMP_LEN = 32
CMP_STRIDE = 16
CMP_HIDDEN = 128
SEL_BLOCK = 64
SEL_TOP_N = 16
WINDOW = 512
FORCE_BONUS = 1e4
N_BUCKETS = 32
MAX_DISTANCE = 1024
EPS = 1e-6
NEG = -1e30

GDN_QK_W = GDN_HEADS * GDN_DK
GDN_W = GDN_HEADS * GDN_DV
GDN_CONV_W = 2 * GDN_QK_W + GDN_W
NSA_W = NSA_HEADS * NSA_DH
KV_W = NSA_KV_HEADS * NSA_DH
PROJ_SIZES = (GDN_QK_W, GDN_QK_W, GDN_W, GDN_W, GDN_HEADS, GDN_HEADS,
              NSA_W, KV_W, KV_W, KV_W, KV_W, KV_W, KV_W, 3 * NSA_HEADS, NSA_W)

LANES = 128
TQ = 128
QL = NSA_GROUP * TQ
GQ_W = NSA_GROUP * NSA_DH
N_NEAR = 8
TAB_ZERO = N_NEAR
TAB_MASKED = N_NEAR + 1
TAB_WIN_EDGE = N_NEAR + 2
TAB_CMP = N_NEAR + 3
N_TAB = N_NEAR + 4
GROUP_TILES = 4
WIN_TILES = WINDOW // TQ + 1
RANK_CHUNK = 16
RANK_ROWS = 32
VT_ROWS = 80
LOG2E = math.log2(math.e)
CMP_NEAR = 64
CMP_NEAR_BACK = 56
VMEM_LIMIT = 56 * 1024 * 1024

_F32 = jnp.float32
_BF16 = jnp.bfloat16


def _dot(a, b):
    return jnp.dot(a, b, preferred_element_type=_F32)


def _dot_nt(a, b):
    return lax.dot_general(a, b, (((1,), (1,)), ((), ())), preferred_element_type=_F32)


def _dot_tn(a, b):
    return lax.dot_general(a, b, (((0,), (0,)), ((), ())), preferred_element_type=_F32)


def _sigmoid(x):
    return 0.5 * jnp.tanh(0.5 * x) + 0.5


def _silu(x):
    return x * _sigmoid(x)


def _bias_tab_kernel(rel_ref, o_ref, far_ref):
    hk = pl.program_id(0)
    e = pl.program_id(1)
    is_cmp = e == TAB_CMP
    is_edge = e == TAB_WIN_EDGE
    base = jnp.where(e < N_NEAR, e * TQ,
                     jnp.where(is_edge, WINDOW, CMP_STRIDE * CMP_NEAR_BACK - (CMP_LEN - 1)))
    row_scale = jnp.where(is_cmp, CMP_STRIDE, 1)
    upper = jnp.where(is_edge, WINDOW, 1 << 30)
    row = lax.broadcasted_iota(jnp.int32, (TQ, TQ), 0)
    tok = lax.broadcasted_iota(jnp.int32, (TQ, TQ), 1)
    dist = base + tok - row_scale * row
    n = jnp.maximum(dist, 0)
    max_exact = N_BUCKETS // 2
    large = max_exact + (jnp.log(jnp.maximum(n, max_exact).astype(_F32) / max_exact)
                         / math.log(MAX_DISTANCE / max_exact) * (N_BUCKETS - max_exact)).astype(jnp.int32)
    bucket = jnp.where(n < max_exact, n, jnp.minimum(large, N_BUCKETS - 1))
    ok = (dist >= 0) & (dist < upper)
    for g in range(NSA_GROUP):
        head = hk * NSA_GROUP + g
        lanes = slice(g * TQ, (g + 1) * TQ)
        far = jnp.full((8, TQ), rel_ref[N_BUCKETS - 1, head] * LOG2E, _F32)
        far_hi = far.astype(_BF16).astype(_F32)
        far_lo = (far - far_hi).astype(_BF16).astype(_F32)
        val = jnp.zeros((TQ, TQ), _F32)
        for b in range(N_BUCKETS):
            val = jnp.where(bucket == b, rel_ref[b, head], val)
        val = val * LOG2E - jnp.where(is_cmp, 0.0, far_hi[0:1, :] + far_lo[0:1, :])
        val = jnp.where(ok, val, NEG)
        val = jnp.where(e == TAB_ZERO, 0.0, jnp.where(e == TAB_MASKED, NEG, val))
        o_ref[0, 0, :, lanes] = val

        sub = lax.broadcasted_iota(jnp.int32, (8, TQ), 0)
        far_ref[0, :, lanes] = jnp.where(sub == 0, far_hi, jnp.where(sub == 1, far_lo,
                                                                     jnp.where(sub == 2, far, 0.0)))


def _bias_tables(rel_bias):
    return pl.pallas_call(
        _bias_tab_kernel,
        out_shape=(jax.ShapeDtypeStruct((NSA_KV_HEADS, N_TAB, TQ, QL), _F32),
                   jax.ShapeDtypeStruct((NSA_KV_HEADS, 8, QL), _F32)),
        grid=(NSA_KV_HEADS, N_TAB),
        in_specs=[pl.BlockSpec(memory_space=pltpu.SMEM)],
        out_specs=(pl.BlockSpec((1, 1, TQ, QL), lambda h, e: (h, e, 0, 0)),
                   pl.BlockSpec((1, 8, QL), lambda h, e: (h, 0, 0))),
        name="bias_tables",
    )(rel_bias.astype(_F32))


_C_CONV = (0, GDN_CONV_W)
_C_GZ = (_C_CONV[1], _C_CONV[1] + GDN_W)
_C_NQ = (_C_GZ[1], _C_GZ[1] + NSA_W)
_C_NZ = (_C_NQ[1], _C_NQ[1] + NSA_W)
_C_SMALL_KC = (_C_NZ[1], _C_NZ[1] + 2 * LANES)
_C_VC_KS = (_C_SMALL_KC[1], _C_SMALL_KC[1] + 2 * LANES)
_C_KW_VS = (_C_VC_KS[1], _C_VC_KS[1] + 2 * LANES)
_C_VW = (_C_KW_VS[1], _C_KW_VS[1] + LANES)
_W_COLS = _C_VW[1]
assert KV_W == LANES


def _regroup_w_in(w):
    gq, gk, gv, gz, gb, ga, nq, kc, vc, ks, vs, kw, vw, ng, nz = jnp.split(
        w, [int(p) for p in np.cumsum(PROJ_SIZES)[:-1]], axis=-1)
    small = jnp.concatenate([gb, ga, ng], axis=-1)
    small = jnp.pad(small, ((0, 0), (0, LANES - small.shape[-1])))
    return jnp.concatenate([gq, gk, gv, gz, nq, nz, small, kc, vc, ks, kw, vs, vw], axis=-1).astype(_BF16)


def _in_proj_kernel(x_ref, nw_ref, w_ref, conv_ref, gz_ref, sm_ref, smt_ref, nq_ref, kc_ref, vc_ref,
                    ks_ref, kw_ref, vst_ref, vwt_ref, nz_ref):
    x = x_ref[...]
    u = x * lax.rsqrt(jnp.mean(x * x, axis=-1, keepdims=True) + EPS) * nw_ref[...]
    u = u.astype(_BF16)

    def proj(c):
        return _dot(u, w_ref[:, c[0]:c[1]])

    def proj_pair(c):
        both = proj(c)
        return both[:, :LANES], both[:, LANES:]

    conv_ref[...] = proj(_C_CONV)
    gz_ref[...] = proj(_C_GZ)
    nq_ref[...] = proj(_C_NQ).astype(_BF16)
    nz_ref[...] = proj(_C_NZ)
    sm, kc = proj_pair(_C_SMALL_KC)
    vc, ks = proj_pair(_C_VC_KS)
    kw, vs = proj_pair(_C_KW_VS)
    sm_ref[...] = sm
    smt_ref[0] = sm.T
    kc_ref[...] = kc.astype(_BF16)
    vc_ref[...] = vc.astype(_BF16)
    ks_ref[...] = ks.astype(_BF16)
    kw_ref[...] = kw.astype(_BF16)
    tm = x.shape[0]
    ones_pad = (lax.broadcasted_iota(jnp.int32, (VT_ROWS - NSA_DH, tm), 0) == 0).astype(_F32)

    def values_t(v):
        vt = v.T
        parts = []
        for h in range(NSA_KV_HEADS):
            parts += [vt[h * NSA_DH:(h + 1) * NSA_DH, :], ones_pad]
        return jnp.concatenate(parts, axis=0).astype(_BF16)

    vst_ref[0] = values_t(vs)
    vwt_ref[0] = values_t(proj(_C_VW))


def _in_proj(x2d, norm_w, w_cat, batch, seq, tm):
    n = x2d.shape[0]
    tiles_per_seq = seq // tm
    row = lambda w: pl.BlockSpec((tm, w), lambda i: (i, 0))
    tr_map = lambda i: (i // tiles_per_seq, 0, i % tiles_per_seq)
    tr = pl.BlockSpec((1, LANES, tm), tr_map)
    trv = pl.BlockSpec((1, NSA_KV_HEADS * VT_ROWS, tm), tr_map)
    sds = jax.ShapeDtypeStruct
    return pl.pallas_call(
        _in_proj_kernel,
        out_shape=(
            sds((n, GDN_CONV_W), _F32), sds((n, GDN_W), _F32), sds((n, LANES), _F32),
            sds((batch, LANES, seq), _F32), sds((n, NSA_W), _BF16), sds((n, KV_W), _BF16),
            sds((n, KV_W), _BF16), sds((n, KV_W), _BF16), sds((n, KV_W), _BF16),
            sds((batch, NSA_KV_HEADS * VT_ROWS, seq), _BF16), sds((batch, NSA_KV_HEADS * VT_ROWS, seq), _BF16),
            sds((n, NSA_W), _F32)),
        grid=(n // tm,),
        in_specs=[row(D_MODEL), pl.BlockSpec((1, D_MODEL), lambda i: (0, 0)),
                  pl.BlockSpec((D_MODEL, _W_COLS), lambda i: (0, 0))],
        out_specs=(row(GDN_CONV_W), row(GDN_W), row(LANES), tr, row(NSA_W), row(KV_W), row(KV_W),
                   row(KV_W), row(KV_W), trv, trv, row(NSA_W)),
        compiler_params=pltpu.CompilerParams(dimension_semantics=("arbitrary",),
                                             vmem_limit_bytes=VMEM_LIMIT),
        name="in_proj",
    )(x2d, norm_w.reshape(1, D_MODEL).astype(_F32), w_cat)


_GDN_PAIR = 2 * CHUNK
_STACK = GDN_HEADS * CHUNK
_PREP_UNROLL = 4


def _segment_cumsum(g, axis):
    pos = lax.broadcasted_iota(jnp.int32, g.shape, axis) % CHUNK
    k = 1
    while k < CHUNK:
        g = g + jnp.where(pos >= k, pltpu.roll(g, k, axis), 0.0)
        k *= 2
    return g


def _stacked_unit_lower_inverses(lows, out):
    r = lax.broadcasted_iota(jnp.int32, (_STACK, _STACK), 0)
    c = lax.broadcasted_iota(jnp.int32, (_STACK, _STACK), 1)
    eye = (r == c).astype(_F32)
    ms = [-low for low in lows]
    xs = [eye + m for m in ms]
    ms = [_dot(m, m) for m in ms]
    yield
    span = 4
    while span < CHUNK:
        boths = [_dot(jnp.concatenate([x, m], axis=0), m) for x, m in zip(xs, ms)]
        xs = [x + both[:_STACK] for x, both in zip(xs, boths)]
        ms = [both[_STACK:] for both in boths]
        span *= 2
        yield
    out.extend(x + _dot(x, m) for x, m in zip(xs, ms))
    yield


def _gdn_kernel(x_ref, cw_ref, sm_ref, smt_ref, prow_ref, pcol_ref, gz_ref, gnw_ref, o_ref,
                xs_ref, q_ref, k_ref, v_ref, col_ref, grow_ref, state_ref,
                u_ref, w_ref, qd_ref, kd_ref, a_ref, egl_ref, oraw_ref, *, tile, tiles_per_seq):
    t_idx = pl.program_id(0)
    slot_w = t_idx % 2
    slot_r = 1 - slot_w

    @pl.when(t_idx == 0)
    def _():
        for ref in (u_ref, w_ref, qd_ref, kd_ref, a_ref, egl_ref):
            ref[1] = jnp.zeros(ref.shape[1:], _F32)

    @pl.when(t_idx % tiles_per_seq == 0)
    def _():
        xs_ref[0:8, :] = jnp.zeros((8, GDN_CONV_W), _F32)

    @pl.when(t_idx % tiles_per_seq != 0)
    def _():
        xs_ref[0:8, :] = xs_ref[tile:tile + 8, :]

    @pl.when((t_idx == 0) | ((t_idx - 1) % tiles_per_seq == 0))
    def _():
        state_ref[...] = jnp.zeros_like(state_ref)

    xs_ref[8:tile + 8, :] = x_ref[0]

    y = jnp.zeros((tile, GDN_CONV_W), _F32)
    for tap in range(CONV_K):
        off = 8 - (CONV_K - 1) + tap
        y = y + cw_ref[tap:tap + 1, :] * xs_ref[off:off + tile, :]
    y = _silu(y)

    for h in range(GDN_HEADS):
        lo = h * GDN_DK
        qh = y[:, lo:lo + GDN_DK]
        kh = y[:, GDN_QK_W + lo:GDN_QK_W + lo + GDN_DK]
        q_ref[:, lo:lo + GDN_DK] = qh * lax.rsqrt(jnp.sum(qh * qh, axis=-1, keepdims=True) + EPS) * GDN_DK ** -0.5
        k_ref[:, lo:lo + GDN_DK] = kh * lax.rsqrt(jnp.sum(kh * kh, axis=-1, keepdims=True) + EPS)
    v_ref[...] = y[:, 2 * GDN_QK_W:]

    sm = sm_ref[0]
    lane = lax.broadcasted_iota(jnp.int32, sm.shape, 1)
    zc = sm + prow_ref[1:2, :]
    gcol = -jnp.exp(prow_ref[0:1, :]) * (jnp.maximum(zc, 0.0) + jnp.log1p(jnp.exp(-jnp.abs(zc))))
    gcol = _segment_cumsum(gcol, 0)
    col_ref[...] = jnp.where(lane < GDN_HEADS, _sigmoid(sm), gcol)
    zr = smt_ref[0, 0:8, :] + pcol_ref[1]
    grow = -jnp.exp(pcol_ref[0]) * (jnp.maximum(zr, 0.0) + jnp.log1p(jnp.exp(-jnp.abs(zr))))
    grow_ref[...] = _segment_cumsum(grow, 1)

    r = lax.broadcasted_iota(jnp.int32, (_STACK, _STACK), 0)
    c = lax.broadcasted_iota(jnp.int32, (_STACK, _STACK), 1)
    same_head = (r // CHUNK) == (c // CHUNK)
    tril = same_head & (r >= c)
    strict = same_head & (r > c)
    heads = range(GDN_HEADS)

    def prep_gram(ci, half):
        rows = pl.ds(pl.multiple_of(ci * CHUNK, CHUNK), CHUNK)
        slab = pl.ds(pl.multiple_of((ci // 2) * _GDN_PAIR, _GDN_PAIR), _GDN_PAIR)
        colv = col_ref[rows, :]
        grow2 = grow_ref[:, slab]

        def stack(ref, width):
            return jnp.concatenate([ref[rows, h * width:(h + 1) * width] for h in heads], axis=0)

        q = stack(q_ref, GDN_DK)
        k = stack(k_ref, GDN_DK)
        v = stack(v_ref, GDN_DV)
        beta = jnp.concatenate([colv[:, h:h + 1] for h in heads], axis=0)
        gc = jnp.concatenate([colv[:, GDN_HEADS + h:GDN_HEADS + h + 1] for h in heads], axis=0)
        gr = jnp.concatenate([grow2[GDN_HEADS + h:GDN_HEADS + h + 1, half * CHUNK:(half + 1) * CHUNK]
                              for h in heads], axis=1)
        last = [colv[CHUNK - 1:CHUNK, GDN_HEADS + h:GDN_HEADS + h + 1] for h in heads]
        g_last = jnp.concatenate([jnp.broadcast_to(x, (CHUNK, 1)) for x in last], axis=0)
        decay = jnp.where(tril, jnp.exp(jnp.where(tril, gc - gr, 0.0)), 0.0)
        kb = k * beta
        eg = jnp.exp(gc)
        low = jnp.where(strict, _dot_nt(kb, k) * decay, 0.0)
        a_ref[slot_w, ci] = jnp.where(tril, _dot_nt(q, k) * decay, 0.0)
        kd_ref[slot_w, ci] = k * jnp.exp(g_last - gc)
        qd_ref[slot_w, ci] = q * eg
        egl_ref[slot_w, ci] = jnp.concatenate([jnp.broadcast_to(jnp.exp(x), (1, GDN_DV)) for x in last]
                                              + [jnp.zeros((8 - GDN_HEADS, GDN_DV), _F32)], axis=0)
        return low, jnp.concatenate([v * beta, kb * eg], axis=1)

    def scan_chunks(cis):
        states = [state_ref[h] for h in heads]
        for ci in cis:
            u, w, qd, kd = u_ref[slot_r, ci], w_ref[slot_r, ci], qd_ref[slot_r, ci], kd_ref[slot_r, ci]
            egl = egl_ref[slot_r, ci]
            rows = pl.ds(pl.multiple_of(ci * CHUNK, CHUNK), CHUNK)
            hrows = [slice(h * CHUNK, (h + 1) * CHUNK) for h in heads]
            wqs = [_dot(jnp.concatenate([w[rs], qd[rs]], axis=0), states[h]) for h, rs in zip(heads, hrows)]
            yield
            v_news = [u[rs] - wq[:CHUNK] for rs, wq in zip(hrows, wqs)]
            for h in heads:
                states[h] = states[h] * egl[h:h + 1, :] + _dot_tn(kd[hrows[h]], v_news[h])
            o = (jnp.concatenate([wq[CHUNK:] for wq in wqs], axis=0)
                 + _dot(a_ref[slot_r, ci], jnp.concatenate(v_news, axis=0)))
            for h in heads:
                oraw_ref[rows, h * GDN_DV:(h + 1) * GDN_DV] = o[hrows[h]]
            yield
        for h in heads:
            state_ref[h] = states[h]

    def prep_chunks(cis):
        lows, rhss = zip(*[prep_gram(ci, j % 2) for j, ci in enumerate(cis)])
        yield
        tinvs = []
        yield from _stacked_unit_lower_inverses(lows, tinvs)
        sols = [_dot(tinv, rhs) for tinv, rhs in zip(tinvs, rhss)]
        for ci, sol in zip(cis, sols):
            u_ref[slot_w, ci] = sol[:, :GDN_DV]
            w_ref[slot_w, ci] = sol[:, GDN_DV:]

    def prep_and_scan_body(pi, carry):
        cis = [pi * _PREP_UNROLL + j for j in range(_PREP_UNROLL)]
        streams = [scan_chunks(cis), prep_chunks(cis)]
        while streams:
            for stream in list(streams):
                if next(stream, StopIteration) is StopIteration:
                    streams.remove(stream)
        return carry

    n_chunks = tile // CHUNK
    lax.fori_loop(0, n_chunks // _PREP_UNROLL, prep_and_scan_body, 0)

    gnw = gnw_ref[...]
    for h in heads:
        lanes = slice(h * GDN_DV, (h + 1) * GDN_DV)
        o = oraw_ref[:, lanes]
        o = o * lax.rsqrt(jnp.mean(o * o, axis=-1, keepdims=True) + EPS) * gnw
        o_ref[0, :, lanes] = (o * _silu(gz_ref[0, :, lanes])).astype(_BF16)


def _gdn(conv_in, conv_w, sm, smt, a_log, dt_bias, gz, gdn_norm_w, tile):
    batch, seq, _ = conv_in.shape
    prow = jnp.zeros((2, LANES), _F32)
    prow = prow.at[0, GDN_HEADS:2 * GDN_HEADS].set(a_log.astype(_F32))
    prow = prow.at[1, GDN_HEADS:2 * GDN_HEADS].set(dt_bias.astype(_F32))
    pcol = jnp.broadcast_to(prow[:, :8, None], (2, 8, tile))
    n_chunks = tile // CHUNK
    assert n_chunks % _PREP_UNROLL == 0 and _PREP_UNROLL % 2 == 0
    tiles_per_seq = seq // tile
    n_tiles = batch * tiles_per_seq
    kern = functools.partial(_gdn_kernel, tile=tile, tiles_per_seq=tiles_per_seq)

    def prep_tile(i):
        i = jnp.minimum(i, n_tiles - 1)
        return i // tiles_per_seq, i % tiles_per_seq

    def scan_tile(i):
        i = jnp.maximum(i - 1, 0)
        return i // tiles_per_seq, i % tiles_per_seq

    rows_of = lambda pick: (lambda i: (*pick(i), 0))
    const = lambda i: (0, 0)
    return pl.pallas_call(
        kern,
        out_shape=jax.ShapeDtypeStruct((batch, seq, GDN_W), _BF16),
        grid=(n_tiles + 1,),
        in_specs=[
            pl.BlockSpec((1, tile, GDN_CONV_W), rows_of(prep_tile)),
            pl.BlockSpec((CONV_K, GDN_CONV_W), const),
            pl.BlockSpec((1, tile, LANES), rows_of(prep_tile)),
            pl.BlockSpec((1, LANES, tile), lambda i: (prep_tile(i)[0], 0, prep_tile(i)[1])),
            pl.BlockSpec((2, LANES), const),
            pl.BlockSpec((2, 8, tile), lambda i: (0, 0, 0)),
            pl.BlockSpec((1, tile, GDN_W), rows_of(scan_tile)),
            pl.BlockSpec((1, GDN_DV), const),
        ],
        out_specs=pl.BlockSpec((1, tile, GDN_W), rows_of(scan_tile)),
        scratch_shapes=[
            pltpu.VMEM((tile + 8, GDN_CONV_W), _F32),
            pltpu.VMEM((tile, GDN_QK_W), _F32),
            pltpu.VMEM((tile, GDN_QK_W), _F32),
            pltpu.VMEM((tile, GDN_W), _F32),
            pltpu.VMEM((tile, LANES), _F32),
            pltpu.VMEM((8, tile), _F32),
            pltpu.VMEM((GDN_HEADS, GDN_DK, GDN_DV), _F32),
            pltpu.VMEM((2, n_chunks, _STACK, GDN_DV), _F32),
            pltpu.VMEM((2, n_chunks, _STACK, GDN_DV), _F32),
            pltpu.VMEM((2, n_chunks, _STACK, GDN_DK), _F32),
            pltpu.VMEM((2, n_chunks, _STACK, GDN_DK), _F32),
            pltpu.VMEM((2, n_chunks, _STACK, _STACK), _F32),
            pltpu.VMEM((2, n_chunks, 8, GDN_DV), _F32),
            pltpu.VMEM((tile, GDN_W), _F32),
        ],
        compiler_params=pltpu.CompilerParams(dimension_semantics=("arbitrary",),
                                             vmem_limit_bytes=VMEM_LIMIT),
        name="gdn",
    )(conv_in, conv_w.astype(_F32), sm, smt, prow, pcol, gz, gdn_norm_w.reshape(1, GDN_DV).astype(_F32))


_GROUP_W = CMP_STRIDE * KV_W
_HID2 = NSA_KV_HEADS * CMP_HIDDEN


def _compress_weights(pe, w1, w2):
    eye = jnp.eye(NSA_KV_HEADS, dtype=_F32)
    halves = []
    pes = []
    for part in range(CMP_LEN // CMP_STRIDE):
        w1p = w1[part * CMP_STRIDE:(part + 1) * CMP_STRIDE].astype(_F32)
        big = jnp.einsum('ldf,hg->lhdgf', w1p, eye).reshape(_GROUP_W, _HID2)
        halves.append(big.astype(_BF16))
        pep = pe[part * CMP_STRIDE:(part + 1) * CMP_STRIDE].astype(_F32)
        pes.append(jnp.broadcast_to(pep[:, None, :], (CMP_STRIDE, NSA_KV_HEADS, NSA_DH)).reshape(1, _GROUP_W))
    w2big = jnp.einsum('fd,hg->hfgd', w2.astype(_F32), eye).reshape(_HID2, KV_W).astype(_BF16)
    pe2 = jnp.broadcast_to(jnp.stack(pes, 0), (2, 8, _GROUP_W)).astype(_BF16)
    return halves[0], halves[1], w2big, pe2


def _compress_kernel(kc_ref, vc_ref, kwa_ref, kwb_ref, kw2_ref, kpe_ref, vwa_ref, vwb_ref, vw2_ref, vpe_ref,
                     ko_ref, vo_ref):
    def mlp(g_ref, wa_ref, wb_ref, w2_ref, pe_ref):
        g = g_ref[0]
        ncp = g.shape[0]
        p0 = _dot(g, wa_ref[...])
        p1 = _dot(g, wb_ref[...])
        const = _dot(pe_ref[0], wa_ref[...]) + _dot(pe_ref[1], wb_ref[...])
        pre = p0 + pltpu.roll(p1, ncp - 1, 0) + const[0:1, :]
        return _dot(_silu(pre).astype(_BF16), w2_ref[...])

    ko_ref[0] = mlp(kc_ref, kwa_ref, kwb_ref, kw2_ref, kpe_ref).astype(_BF16)
    vo_ref[0] = mlp(vc_ref, vwa_ref, vwb_ref, vw2_ref, vpe_ref).T.astype(_BF16)


def _compress(kc, vc, kweights, vweights):
    batch, ncp, _ = kc.shape
    full = lambda a: pl.BlockSpec(a.shape, lambda b: (0,) * a.ndim)
    per_b = pl.BlockSpec((1, ncp, _GROUP_W), lambda b: (b, 0, 0))
    return pl.pallas_call(
        _compress_kernel,
        out_shape=(jax.ShapeDtypeStruct((batch, ncp, KV_W), _BF16),
                   jax.ShapeDtypeStruct((batch, KV_W, ncp), _BF16)),
        grid=(batch,),
        in_specs=[per_b, per_b] + [full(a) for a in kweights] + [full(a) for a in vweights],
        out_specs=(pl.BlockSpec((1, ncp, KV_W), lambda b: (b, 0, 0)),
                   pl.BlockSpec((1, KV_W, ncp), lambda b: (b, 0, 0))),
        compiler_params=pltpu.CompilerParams(dimension_semantics=("arbitrary",),
                                             vmem_limit_bytes=VMEM_LIMIT),
        name="compress",
    )(kc, vc, *kweights, *vweights)


def _overlap_t(n_sel, ncp, n_cmp):
    cs = CMP_STRIDE * np.arange(ncp)
    ss = SEL_BLOCK * np.arange(n_sel)
    ov = ((cs[None, :] < ss[:, None] + SEL_BLOCK) & (cs[None, :] + CMP_LEN > ss[:, None])
          & (np.arange(ncp)[None, :] < n_cmp))
    return jnp.asarray(ov.astype(np.float32), dtype=_BF16)


def _key_onehots(n_tiles, with_blocks):
    key = np.arange(n_tiles * TQ)[:, None]
    col = np.arange(LANES)[None, :]
    ek = np.broadcast_to((col == 8) | (col == 9), (n_tiles * TQ, LANES))
    if with_blocks:
        ek = ek | (col == key // SEL_BLOCK)
    return jnp.asarray(ek.astype(np.float32), dtype=_BF16)


def _nsa_kernel(q_ref, ks_ref, kw_ref, vst_ref, vwt_ref, kc_ref, vct_ref, tab_ref, far_ref, ovt_ref, eks_ref,
                ekw_ref, smt_ref, nz_ref, o_ref, cb_ref, sc_ref, cnt_ref, mb_ref, ss_ref, st_ref, ps_ref, ocw_ref):
    a = pl.program_id(1)
    ncp = kc_ref.shape[1]
    n_sel = ovt_ref.shape[0]
    heads = range(NSA_KV_HEADS)
    drows = [slice(h * NSA_DH, (h + 1) * NSA_DH) for h in heads]
    vrows = [slice(h * VT_ROWS, (h + 1) * VT_ROWS) for h in heads]
    qcols = [slice(h * GQ_W, (h + 1) * GQ_W) for h in heads]

    qrow = lax.broadcasted_iota(jnp.int32, (KV_W, QL), 0)

    def transposed_queries(h):
        qt = (q_ref[0, :, qcols[h]].astype(_F32) * (NSA_DH ** -0.5 * LOG2E)).T
        qt4 = jnp.concatenate([qt[g * NSA_DH:(g + 1) * NSA_DH, :] for g in range(NSA_GROUP)], axis=1)
        return jnp.where(qrow // NSA_DH == h, jnp.concatenate([qt4] * NSA_KV_HEADS, axis=0), 0.0).astype(_BF16)

    qt_b = [transposed_queries(h) for h in heads]
    far_rows = [far_ref[h] for h in heads]
    zero_rows = jnp.zeros((LANES - 16, QL), _BF16)

    def augmented_queries(h, mask_rows):
        extra = jnp.concatenate([mask_rows, far_rows[h]], axis=0).astype(_BF16)
        return jnp.concatenate([qt_b[h], extra, zero_rows], axis=0)

    def augmented_keys(k_ref, ek, key0, n_tiles):
        keys = pl.ds(pl.multiple_of(key0, TQ), n_tiles * TQ)
        return jnp.concatenate([k_ref[0, keys, :], ek], axis=1)

    def gate(h, branch):
        first = 8 + 8 * branch + NSA_GROUP * h
        return _sigmoid(jnp.concatenate([smt_ref[0, first + g:first + g + 1, :] for g in range(NSA_GROUP)], axis=1))

    def normalized(acc):
        return acc[0:NSA_DH, :] * (1.0 / acc[NSA_DH:NSA_DH + 1, :])

    win_t0 = jnp.maximum(a - (WIN_TILES - 1), 0)
    win_keys = pl.ds(pl.multiple_of(win_t0 * TQ, TQ), WIN_TILES * TQ)
    win_start = pl.multiple_of(jnp.maximum(8 * a - CMP_NEAR_BACK, 0), 8)
    tab_start = pl.multiple_of(win_start - (8 * a - CMP_NEAR_BACK), 8)

    def compressed_scores(n_rows):
        crow = lax.broadcasted_iota(jnp.int32, (n_rows, QL), 0)
        for h in heads:
            cb_ref[h, 0:n_rows, :] = jnp.where(crow < win_start, far_rows[h][2:3, :], NEG)
            cb_ref[h, pl.ds(win_start, CMP_NEAR), :] = tab_ref[h, TAB_CMP, pl.ds(tab_start, CMP_NEAR), :]
        k_cmp = kc_ref[0, 0:n_rows, :]
        return [_dot(k_cmp, qt_b[h]) + cb_ref[h, 0:n_rows, :] for h in heads]

    def compressed_branch(s_c, acc_w):
        n_rows = s_c[0].shape[0]
        m_c = [jnp.max(s, axis=0, keepdims=True) for s in s_c]
        e_c = [jnp.exp2(s - m) for s, m in zip(s_c, m_c)]
        l_c = [jnp.sum(e, axis=0, keepdims=True) for e in e_c]
        p_c = [e * jnp.where(m > 0.5 * NEG, 1.0 / l, 0.0) for e, m, l in zip(e_c, m_c, l_c)]
        o_c = [_dot(vct_ref[0, drows[h], 0:n_rows], p_c[h].astype(_BF16)) for h in heads]
        for h in heads:
            ocw_ref[h] = gate(h, 0) * o_c[h] + gate(h, 2) * normalized(acc_w[h])
        ovt = ovt_ref[:, 0:n_rows]
        jrow = lax.broadcasted_iota(jnp.int32, (n_sel, TQ), 0)
        cur = (a * TQ + lax.broadcasted_iota(jnp.int32, (n_sel, TQ), 1)) // SEL_BLOCK
        forced = (jrow == 0) | (jrow == cur) | (jrow == cur - 1)
        for h in heads:
            psum = p_c[h][:, 0:TQ]
            for g in range(1, NSA_GROUP):
                psum = psum + p_c[h][:, g * TQ:(g + 1) * TQ]
            hi = psum.astype(_BF16)
            r1 = psum - hi.astype(_F32)
            mid = r1.astype(_BF16)
            lo = (r1 - mid.astype(_F32)).astype(_BF16)
            imp = _dot(ovt, hi) + _dot(ovt, mid) + _dot(ovt, lo)
            sc_ref[h] = jnp.where(jrow <= cur, imp + jnp.where(forced, FORCE_BONUS, 0.0), NEG)

    win_entries = []
    for i in range(WIN_TILES):
        d = a - (win_t0 + i)
        win_entries.append(jnp.where(d < 0, TAB_MASKED, jnp.where(d == WIN_TILES - 1, TAB_WIN_EDGE, d)))
    def front_stage(n_rows):
        k_aug_w = augmented_keys(kw_ref, ekw_ref[...], win_t0 * TQ, WIN_TILES)
        no_mask = jnp.zeros((8, QL), _F32)
        s_w = [_dot(k_aug_w, augmented_queries(h, no_mask)) for h in heads]
        s_c = compressed_scores(n_rows)
        s_wb = [jnp.concatenate([s_w[h][i * TQ:(i + 1) * TQ, :] + tab_ref[h, win_entries[i]]
                                 for i in range(WIN_TILES)], axis=0) for h in heads]
        m_w = [jnp.max(s, axis=0, keepdims=True) for s in s_wb]
        acc_w = [_dot(vwt_ref[0, vrows[h], win_keys], jnp.exp2((s_wb[h] - m_w[h]).astype(_BF16))) for h in heads]
        compressed_branch(s_c, acc_w)

    half_rows = ncp // 2
    needs_all = 8 * a + 8 > half_rows

    @pl.when(jnp.logical_not(needs_all))
    def _():
        front_stage(half_rows)

    @pl.when(needs_all)
    def _():
        front_stage(ncp)

    cnt_ref[...] = jnp.zeros(cnt_ref.shape, _F32)
    sub = lax.broadcasted_iota(jnp.int32, (8, TQ), 0)
    last_valid = (TQ // SEL_BLOCK) * (a + 1) - 1
    for chunk in range(n_sel // RANK_CHUNK):
        for row_chunk in range(n_sel // RANK_ROWS):
            row_groups = range(row_chunk * RANK_ROWS // 8, (row_chunk + 1) * RANK_ROWS // 8)

            @pl.when((chunk * RANK_CHUNK <= last_valid) & (row_chunk * RANK_ROWS <= last_valid))
            def _():
                for h in heads:
                    rows = {r: sc_ref[h, 8 * r:8 * r + 8, :] for r in row_groups}
                    cnts = {r: cnt_ref[h, 8 * r:8 * r + 8, :] for r in row_groups}
                    for jp in range(chunk * RANK_CHUNK, (chunk + 1) * RANK_CHUNK):
                        srow = sc_ref[h, jp:jp + 1, :]
                        for r in row_groups:
                            if 8 * r > jp:
                                inc = jnp.where(srow >= rows[r], 1.0, 0.0)
                            elif 8 * r + 7 < jp:
                                inc = jnp.where(srow > rows[r], 1.0, 0.0)
                            else:
                                tie = jnp.where(sub + 8 * r > jp, 1.0, 0.0)
                                inc = jnp.where(srow > rows[r], 1.0, jnp.where(srow >= rows[r], tie, 0.0))
                            cnts[r] = cnts[r] + inc
                    for r in row_groups:
                        cnt_ref[h, 8 * r:8 * r + 8, :] = cnts[r]

    for h in heads:
        mb = jnp.where(cnt_ref[h] < SEL_TOP_N, 0.0, NEG)
        mb_ref[h] = jnp.concatenate([mb] * NSA_GROUP, axis=1)

    eks = eks_ref[...]
    blocks_per_group = GROUP_TILES * (TQ // SEL_BLOCK)
    group_keys = GROUP_TILES * TQ
    last_group = a // GROUP_TILES

    def sel_scores(gi):
        k_aug = augmented_keys(ks_ref, eks, gi * group_keys, GROUP_TILES)
        mask_rows = pl.ds(pl.multiple_of(gi * blocks_per_group, 8), blocks_per_group)
        return [_dot(k_aug, augmented_queries(h, mb_ref[h, mask_rows, :])) for h in heads]

    def sel_values(h, gi):
        return vst_ref[0, vrows[h], pl.ds(pl.multiple_of(gi * group_keys, group_keys), group_keys)]

    def sel_step(gi, carry, near, last):
        s_next = None if last else sel_scores(gi + 1)
        pv_prev = [_dot(sel_values(h, jnp.maximum(gi - 1, 0)), ps_ref[h]) for h in heads]
        entries = []
        if near:
            for i in range(GROUP_TILES):
                d = a - (gi * GROUP_TILES + i)
                entries.append(jnp.where(d < 0, TAB_MASKED, jnp.where(d >= N_NEAR, TAB_ZERO, d)))
        new_carry = []
        for h in heads:
            m, alpha_prev, acc = carry[h]
            s = ss_ref[h]
            if near:
                s = jnp.concatenate([s[i * TQ:(i + 1) * TQ, :] + tab_ref[h, entries[i]]
                                     for i in range(GROUP_TILES)], axis=0)
                st_ref[h] = s
            m_new = jnp.maximum(m, jnp.max(s, axis=0, keepdims=True))
            alpha = jnp.exp2(m - m_new)
            acc = alpha_prev * acc + pv_prev[h]
            s = st_ref[h] if near else ss_ref[h]
            ps_ref[h] = jnp.exp2((s - m_new).astype(_BF16))
            new_carry.append((m_new, alpha, acc))
        if not last:
            for h in heads:
                ss_ref[h] = s_next[h]
        return tuple(new_carry)

    s_first = sel_scores(0)
    for h in heads:
        ss_ref[h] = s_first[h]
    ps_ref[...] = jnp.zeros(ps_ref.shape, _BF16)
    carry = tuple((jnp.full((1, QL), NEG, _F32), jnp.ones((1, QL), _F32), jnp.zeros((VT_ROWS, QL), _F32))
                  for _ in heads)
    n_far = jnp.maximum(last_group - N_NEAR // GROUP_TILES, 0)

    carry = lax.fori_loop(0, n_far, functools.partial(sel_step, near=False, last=False), carry)
    carry = lax.fori_loop(n_far, last_group, functools.partial(sel_step, near=True, last=False), carry)
    carry = sel_step(last_group, carry, near=True, last=True)

    for h in heads:
        _, alpha_s, acc_s = carry[h]
        acc_s = alpha_s * acc_s + _dot(sel_values(h, last_group), ps_ref[h])
        o_t = ocw_ref[h] + gate(h, 1) * normalized(acc_s)
        stacked = jnp.concatenate([o_t[:, g * TQ:(g + 1) * TQ] for g in range(NSA_GROUP)], axis=0)
        o_ref[0, :, qcols[h]] = (stacked.T * _silu(nz_ref[0, :, qcols[h]])).astype(_BF16)


def _nsa(nq, ks, kw, vst, vwt, kcmp, vcmpt, tab, far, ovt, smt, nz):
    batch, seq, _ = nq.shape
    ncp = kcmp.shape[1]
    n_sel = seq // SEL_BLOCK
    per_b = lambda shape: pl.BlockSpec((1,) + shape, lambda b, a: (b, 0, 0))
    const = lambda shape: pl.BlockSpec(shape, lambda b, a: (0,) * len(shape), pipeline_mode=pl.Buffered(1))
    tile = lambda width: pl.BlockSpec((1, TQ, width), lambda b, a: (b, a, 0))
    vt_shape = (NSA_KV_HEADS * VT_ROWS, seq)
    per_head = lambda shape, dtype: pltpu.VMEM((NSA_KV_HEADS,) + shape, dtype)
    return pl.pallas_call(
        _nsa_kernel,
        out_shape=jax.ShapeDtypeStruct((batch, seq, NSA_W), _BF16),
        grid=(batch, seq // TQ),
        in_specs=[
            tile(NSA_W),
            per_b((seq, KV_W)), per_b((seq, KV_W)), per_b(vt_shape), per_b(vt_shape),
            per_b((ncp, KV_W)), per_b((KV_W, ncp)),
            const((NSA_KV_HEADS, N_TAB, TQ, QL)), const((NSA_KV_HEADS, 8, QL)),
            const((n_sel, ncp)), const((GROUP_TILES * TQ, LANES)), const((WIN_TILES * TQ, LANES)),
            pl.BlockSpec((1, LANES, TQ), lambda b, a: (b, 0, a)),
            tile(NSA_W),
        ],
        out_specs=tile(NSA_W),
        scratch_shapes=[per_head((ncp, QL), _F32), per_head((n_sel, TQ), _F32),
                        per_head((n_sel, TQ), _F32), per_head((n_sel, QL), _F32),
                        per_head((GROUP_TILES * TQ, QL), _F32), per_head((GROUP_TILES * TQ, QL), _F32),
                        per_head((GROUP_TILES * TQ, QL), _BF16), per_head((NSA_DH, QL), _F32)],
        compiler_params=pltpu.CompilerParams(dimension_semantics=("arbitrary", "arbitrary"),
                                             vmem_limit_bytes=VMEM_LIMIT),
        name="nsa",
    )(nq, ks, kw, vst, vwt, kcmp, vcmpt, tab, far, ovt, _key_onehots(GROUP_TILES, True),
      _key_onehots(WIN_TILES, False), smt, nz)


def _out_proj_kernel(og_ref, on_ref, x_ref, w_ref, fw_ref, o_ref):
    h = x_ref[...] + _dot(og_ref[...], w_ref[0:GDN_W, :]) + _dot(on_ref[...], w_ref[GDN_W:, :])
    o_ref[...] = h * lax.rsqrt(jnp.mean(h * h, axis=-1, keepdims=True) + EPS) * fw_ref[...]


def _out_proj(o_g, o_n, x2d, w_out, final_norm_w, tm):
    n = x2d.shape[0]
    row = lambda w: pl.BlockSpec((tm, w), lambda i: (i, 0))
    return pl.pallas_call(
        _out_proj_kernel,
        out_shape=jax.ShapeDtypeStruct((n, D_MODEL), _F32),
        grid=(n // tm,),
        in_specs=[row(GDN_W), row(NSA_W), row(D_MODEL),
                  pl.BlockSpec((GDN_W + NSA_W, D_MODEL), lambda i: (0, 0)),
                  pl.BlockSpec((1, D_MODEL), lambda i: (0, 0))],
        out_specs=row(D_MODEL),
        compiler_params=pltpu.CompilerParams(dimension_semantics=("arbitrary",),
                                             vmem_limit_bytes=VMEM_LIMIT),
        name="out_proj",
    )(o_g, o_n, x2d, w_out.astype(_BF16), final_norm_w.reshape(1, D_MODEL).astype(_F32))


def kernel(x, norm_w, w_in, conv_w, a_log, dt_bias, gdn_norm_w, cmp_pe_k, cmp_pe_v, cmp_k_w1, cmp_k_w2,
           cmp_v_w1, cmp_v_w2, w_out, rel_bias, final_norm_w):
    batch, seq, _ = x.shape
    assert w_in.shape[0] == 1, "single-layer problem"
    assert seq % (8 * TQ) == 0
    n = batch * seq
    ncp = seq // CMP_STRIDE
    n_cmp = (seq - CMP_LEN) // CMP_STRIDE + 1
    tm = min(512, seq)

    x2d = x.reshape(n, D_MODEL)
    tab, far = _bias_tables(rel_bias)
    (conv_in, gz, sm, smt, nq, kc, vc, ks, kw, vst, vwt, nz) = _in_proj(
        x2d, norm_w[0], _regroup_w_in(w_in[0]), batch, seq, tm)

    o_g = _gdn(conv_in.reshape(batch, seq, GDN_CONV_W), conv_w[0], sm.reshape(batch, seq, LANES), smt,
               a_log[0], dt_bias[0], gz.reshape(batch, seq, GDN_W), gdn_norm_w[0], tm)

    kcmp, vcmpt = _compress(kc.reshape(batch, ncp, _GROUP_W), vc.reshape(batch, ncp, _GROUP_W),
                            _compress_weights(cmp_pe_k[0], cmp_k_w1[0], cmp_k_w2[0]),
                            _compress_weights(cmp_pe_v[0], cmp_v_w1[0], cmp_v_w2[0]))

    o_n = _nsa(nq.reshape(batch, seq, NSA_W), ks.reshape(batch, seq, KV_W), kw.reshape(batch, seq, KV_W),
               vst, vwt, kcmp, vcmpt, tab, far, _overlap_t(seq // SEL_BLOCK, ncp, n_cmp), smt,
               nz.reshape(batch, seq, NSA_W))

    out = _out_proj(o_g.reshape(n, GDN_W), o_n.reshape(n, NSA_W), x2d, w_out[0], final_norm_w, tm)
    return out.reshape(batch, seq, D_MODEL)
```

```python
import functools
import math

import jax
import jax.numpy as jnp
import numpy as np
from jax import lax
from jax.experimental import pallas as pl
from jax.experimental.pallas import tpu as pltpu

D_MODEL = 1024
GDN_HEADS = 4
GDN_DK = 128
GDN_DV = 128
CONV_K = 4
CHUNK = 64
NSA_HEADS = 8
NSA_KV_HEADS = 2
NSA_GROUP = NSA_HEADS // NSA_KV_HEADS
NSA_DH = 64
CMP_LEN = 32
CMP_STRIDE = 16
CMP_HIDDEN = 128
SEL_BLOCK = 64
SEL_TOP_N = 16
WINDOW = 512
FORCE_BONUS = 1e4
N_BUCKETS = 32
MAX_DISTANCE = 1024
EPS = 1e-6
NEG = -1e30

GDN_QK_W = GDN_HEADS * GDN_DK
GDN_W = GDN_HEADS * GDN_DV
GDN_CONV_W = 2 * GDN_QK_W + GDN_W
NSA_W = NSA_HEADS * NSA_DH
KV_W = NSA_KV_HEADS * NSA_DH
PROJ_SIZES = (GDN_QK_W, GDN_QK_W, GDN_W, GDN_W, GDN_HEADS, GDN_HEADS,
              NSA_W, KV_W, KV_W, KV_W, KV_W, KV_W, KV_W, 3 * NSA_HEADS, NSA_W)

LANES = 128
TQ = 128
QL = NSA_GROUP * TQ
GQ_W = NSA_GROUP * NSA_DH
N_NEAR = 8
TAB_ZERO = N_NEAR
TAB_MASKED = N_NEAR + 1
TAB_WIN_EDGE = N_NEAR + 2
TAB_CMP = N_NEAR + 3
N_TAB = N_NEAR + 4
GROUP_TILES = 4
WIN_TILES = WINDOW // TQ + 1
RANK_CHUNK = 16
RANK_ROWS = 32
VT_ROWS = 80
LOG2E = math.log2(math.e)
CMP_NEAR = 64
CMP_NEAR_BACK = 56
VMEM_LIMIT = 56 * 1024 * 1024

_F32 = jnp.float32
_BF16 = jnp.bfloat16


def _dot(a, b):
    return jnp.dot(a, b, preferred_element_type=_F32)


def _dot_nt(a, b):
    return lax.dot_general(a, b, (((1,), (1,)), ((), ())), preferred_element_type=_F32)


def _dot_tn(a, b):
    return lax.dot_general(a, b, (((0,), (0,)), ((), ())), preferred_element_type=_F32)


def _sigmoid(x):
    return 0.5 * jnp.tanh(0.5 * x) + 0.5


def _silu(x):
    return x * _sigmoid(x)


def _bias_tab_kernel(rel_ref, o_ref, far_ref):
    hk = pl.program_id(0)
    e = pl.program_id(1)
    is_cmp = e == TAB_CMP
    is_edge = e == TAB_WIN_EDGE
    base = jnp.where(e < N_NEAR, e * TQ,
                     jnp.where(is_edge, WINDOW, CMP_STRIDE * CMP_NEAR_BACK - (CMP_LEN - 1)))
    row_scale = jnp.where(is_cmp, CMP_STRIDE, 1)
    upper = jnp.where(is_edge, WINDOW, 1 << 30)
    row = lax.broadcasted_iota(jnp.int32, (TQ, TQ), 0)
    tok = lax.broadcasted_iota(jnp.int32, (TQ, TQ), 1)
    dist = base + tok - row_scale * row
    n = jnp.maximum(dist, 0)
    max_exact = N_BUCKETS // 2
    large = max_exact + (jnp.log(jnp.maximum(n, max_exact).astype(_F32) / max_exact)
                         / math.log(MAX_DISTANCE / max_exact) * (N_BUCKETS - max_exact)).astype(jnp.int32)
    bucket = jnp.where(n < max_exact, n, jnp.minimum(large, N_BUCKETS - 1))
    ok = (dist >= 0) & (dist < upper)
    for g in range(NSA_GROUP):
        head = hk * NSA_GROUP + g
        lanes = slice(g * TQ, (g + 1) * TQ)
        far = jnp.full((8, TQ), rel_ref[N_BUCKETS - 1, head] * LOG2E, _F32)
        far_hi = far.astype(_BF16).astype(_F32)
        far_lo = (far - far_hi).astype(_BF16).astype(_F32)
        val = jnp.zeros((TQ, TQ), _F32)
        for b in range(N_BUCKETS):
            val = jnp.where(bucket == b, rel_ref[b, head], val)
        val = val * LOG2E - jnp.where(is_cmp, 0.0, far_hi[0:1, :] + far_lo[0:1, :])
        val = jnp.where(ok, val, NEG)
        val = jnp.where(e == TAB_ZERO, 0.0, jnp.where(e == TAB_MASKED, NEG, val))
        o_ref[0, 0, :, lanes] = val

        sub = lax.broadcasted_iota(jnp.int32, (8, TQ), 0)
        far_ref[0, :, lanes] = jnp.where(sub == 0, far_hi, jnp.where(sub == 1, far_lo,
                                                                     jnp.where(sub == 2, far, 0.0)))


def _bias_tables(rel_bias):
    return pl.pallas_call(
        _bias_tab_kernel,
        out_shape=(jax.ShapeDtypeStruct((NSA_KV_HEADS, N_TAB, TQ, QL), _F32),
                   jax.ShapeDtypeStruct((NSA_KV_HEADS, 8, QL), _F32)),
        grid=(NSA_KV_HEADS, N_TAB),
        in_specs=[pl.BlockSpec(memory_space=pltpu.SMEM)],
        out_specs=(pl.BlockSpec((1, 1, TQ, QL), lambda h, e: (h, e, 0, 0)),
                   pl.BlockSpec((1, 8, QL), lambda h, e: (h, 0, 0))),
        name="bias_tables",
    )(rel_bias.astype(_F32))


_C_CONV = (0, GDN_CONV_W)
_C_GZ = (_C_CONV[1], _C_CONV[1] + GDN_W)
_C_NQ = (_C_GZ[1], _C_GZ[1] + NSA_W)
_C_NZ = (_C_NQ[1], _C_NQ[1] + NSA_W)
_C_SMALL_KC = (_C_NZ[1], _C_NZ[1] + 2 * LANES)
_C_VC_KS = (_C_SMALL_KC[1], _C_SMALL_KC[1] + 2 * LANES)
_C_KW_VS = (_C_VC_KS[1], _C_VC_KS[1] + 2 * LANES)
_C_VW = (_C_KW_VS[1], _C_KW_VS[1] + LANES)
_W_COLS = _C_VW[1]
assert KV_W == LANES


def _regroup_w_in(w):
    gq, gk, gv, gz, gb, ga, nq, kc, vc, ks, vs, kw, vw, ng, nz = jnp.split(
        w, [int(p) for p in np.cumsum(PROJ_SIZES)[:-1]], axis=-1)
    small = jnp.concatenate([gb, ga, ng], axis=-1)
    small = jnp.pad(small, ((0, 0), (0, LANES - small.shape[-1])))
    return jnp.concatenate([gq, gk, gv, gz, nq, nz, small, kc, vc, ks, kw, vs, vw], axis=-1).astype(_BF16)


def _in_proj_kernel(x_ref, nw_ref, w_ref, conv_ref, gz_ref, sm_ref, smt_ref, nq_ref, kc_ref, vc_ref,
                    ks_ref, kw_ref, vst_ref, vwt_ref, nz_ref):
    x = x_ref[...]
    u = x * lax.rsqrt(jnp.mean(x * x, axis=-1, keepdims=True) + EPS) * nw_ref[...]
    u = u.astype(_BF16)

    def proj(c):
        return _dot(u, w_ref[:, c[0]:c[1]])

    def proj_pair(c):
        both = proj(c)
        return both[:, :LANES], both[:, LANES:]

    conv_ref[...] = proj(_C_CONV)
    gz_ref[...] = proj(_C_GZ)
    nq_ref[...] = proj(_C_NQ).astype(_BF16)
    nz_ref[...] = proj(_C_NZ)
    sm, kc = proj_pair(_C_SMALL_KC)
    vc, ks = proj_pair(_C_VC_KS)
    kw, vs = proj_pair(_C_KW_VS)
    sm_ref[...] = sm
    smt_ref[0] = sm.T
    kc_ref[...] = kc.astype(_BF16)
    vc_ref[...] = vc.astype(_BF16)
    ks_ref[...] = ks.astype(_BF16)
    kw_ref[...] = kw.astype(_BF16)
    tm = x.shape[0]
    ones_pad = (lax.broadcasted_iota(jnp.int32, (VT_ROWS - NSA_DH, tm), 0) == 0).astype(_F32)

    def values_t(v):
        vt = v.T
        parts = []
        for h in range(NSA_KV_HEADS):
            parts += [vt[h * NSA_DH:(h + 1) * NSA_DH, :], ones_pad]
        return jnp.concatenate(parts, axis=0).astype(_BF16)

    vst_ref[0] = values_t(vs)
    vwt_ref[0] = values_t(proj(_C_VW))


def _in_proj(x2d, norm_w, w_cat, batch, seq, tm):
    n = x2d.shape[0]
    tiles_per_seq = seq // tm
    row = lambda w: pl.BlockSpec((tm, w), lambda i: (i, 0))
    tr_map = lambda i: (i // tiles_per_seq, 0, i % tiles_per_seq)
    tr = pl.BlockSpec((1, LANES, tm), tr_map)
    trv = pl.BlockSpec((1, NSA_KV_HEADS * VT_ROWS, tm), tr_map)
    sds = jax.ShapeDtypeStruct
    return pl.pallas_call(
        _in_proj_kernel,
        out_shape=(
            sds((n, GDN_CONV_W), _F32), sds((n, GDN_W), _F32), sds((n, LANES), _F32),
            sds((batch, LANES, seq), _F32), sds((n, NSA_W), _BF16), sds((n, KV_W), _BF16),
            sds((n, KV_W), _BF16), sds((n, KV_W), _BF16), sds((n, KV_W), _BF16),
            sds((batch, NSA_KV_HEADS * VT_ROWS, seq), _BF16), sds((batch, NSA_KV_HEADS * VT_ROWS, seq), _BF16),
            sds((n, NSA_W), _F32)),
        grid=(n // tm,),
        in_specs=[row(D_MODEL), pl.BlockSpec((1, D_MODEL), lambda i: (0, 0)),
                  pl.BlockSpec((D_MODEL, _W_COLS), lambda i: (0, 0))],
        out_specs=(row(GDN_CONV_W), row(GDN_W), row(LANES), tr, row(NSA_W), row(KV_W), row(KV_W),
                   row(KV_W), row(KV_W), trv, trv, row(NSA_W)),
        compiler_params=pltpu.CompilerParams(dimension_semantics=("arbitrary",),
                                             vmem_limit_bytes=VMEM_LIMIT),
        name="in_proj",
    )(x2d, norm_w.reshape(1, D_MODEL).astype(_F32), w_cat)


_GDN_PAIR = 2 * CHUNK
_STACK = GDN_HEADS * CHUNK
_PREP_UNROLL = 4


def _segment_cumsum(g, axis):
    pos = lax.broadcasted_iota(jnp.int32, g.shape, axis) % CHUNK
    k = 1
    while k < CHUNK:
        g = g + jnp.where(pos >= k, pltpu.roll(g, k, axis), 0.0)
        k *= 2
    return g


def _stacked_unit_lower_inverses(lows, out):
    r = lax.broadcasted_iota(jnp.int32, (_STACK, _STACK), 0)
    c = lax.broadcasted_iota(jnp.int32, (_STACK, _STACK), 1)
    eye = (r == c).astype(_F32)
    ms = [-low for low in lows]
    xs = [eye + m for m in ms]
    ms = [_dot(m, m) for m in ms]
    yield
    span = 4
    while span < CHUNK:
        boths = [_dot(jnp.concatenate([x, m], axis=0), m) for x, m in zip(xs, ms)]
        xs = [x + both[:_STACK] for x, both in zip(xs, boths)]
        ms = [both[_STACK:] for both in boths]
        span *= 2
        yield
    out.extend(x + _dot(x, m) for x, m in zip(xs, ms))
    yield


def _gdn_kernel(x_ref, cw_ref, sm_ref, smt_ref, prow_ref, pcol_ref, gz_ref, gnw_ref, o_ref,
                xs_ref, q_ref, k_ref, v_ref, col_ref, grow_ref, state_ref,
                u_ref, w_ref, qd_ref, kd_ref, a_ref, egl_ref, *, tile, tiles_per_seq):
    t_idx = pl.program_id(0)
    slot_1 = t_idx % 2
    slot_2 = 1 - slot_1
    half = tile // 2

    @pl.when(t_idx == 0)
    def _():
        for ref in (q_ref, k_ref, v_ref, col_ref, grow_ref):
            ref[1] = jnp.zeros(ref.shape[1:], _F32)
        for ref in (u_ref, w_ref, qd_ref, kd_ref, a_ref, egl_ref):
            ref[0] = jnp.zeros(ref.shape[1:], _F32)

    @pl.when(t_idx % tiles_per_seq == 0)
    def _():
        xs_ref[0:8, :] = jnp.zeros((8, GDN_CONV_W), _F32)

    @pl.when(t_idx % tiles_per_seq != 0)
    def _():
        xs_ref[0:8, :] = xs_ref[tile:tile + 8, :]

    @pl.when((t_idx == 0) | ((t_idx - 2) % tiles_per_seq == 0))
    def _():
        state_ref[...] = jnp.zeros_like(state_ref)

    xs_ref[8:tile + 8, :] = x_ref[0]
    heads = range(GDN_HEADS)

    def stage1_half(pi):
        r0 = pl.multiple_of(pi * half, half)
        rows = pl.ds(r0, half)
        xe = xs_ref[pl.ds(r0, half + 8), :]
        y = jnp.zeros((half, GDN_CONV_W), _F32)
        for tap in range(CONV_K):
            off = 8 - (CONV_K - 1) + tap
            y = y + cw_ref[tap:tap + 1, :] * xe[off:off + half, :]
        y = _silu(y)
        for h in heads:
            lo = h * GDN_DK
            qh = y[:, lo:lo + GDN_DK]
            kh = y[:, GDN_QK_W + lo:GDN_QK_W + lo + GDN_DK]
            q_ref[slot_1, rows, lo:lo + GDN_DK] = (
                qh * lax.rsqrt(jnp.sum(qh * qh, axis=-1, keepdims=True) + EPS) * GDN_DK ** -0.5)
            k_ref[slot_1, rows, lo:lo + GDN_DK] = kh * lax.rsqrt(jnp.sum(kh * kh, axis=-1, keepdims=True) + EPS)
        v_ref[slot_1, rows, :] = y[:, 2 * GDN_QK_W:]
        sm = sm_ref[0, rows, :]
        lane = lax.broadcasted_iota(jnp.int32, sm.shape, 1)
        zc = sm + prow_ref[1:2, :]
        gcol = -jnp.exp(prow_ref[0:1, :]) * (jnp.maximum(zc, 0.0) + jnp.log1p(jnp.exp(-jnp.abs(zc))))
        col_ref[slot_1, rows, :] = jnp.where(lane < GDN_HEADS, _sigmoid(sm), _segment_cumsum(gcol, 0))
        zr = smt_ref[0, 0:8, rows] + pcol_ref[1]
        grow = -jnp.exp(pcol_ref[0]) * (jnp.maximum(zr, 0.0) + jnp.log1p(jnp.exp(-jnp.abs(zr))))
        grow_ref[slot_1, :, rows] = _segment_cumsum(grow, 1)

    r = lax.broadcasted_iota(jnp.int32, (_STACK, _STACK), 0)
    c = lax.broadcasted_iota(jnp.int32, (_STACK, _STACK), 1)
    same_head = (r // CHUNK) == (c // CHUNK)
    tril = same_head & (r >= c)
    strict = same_head & (r > c)

    def prep_gram(ci, odd):
        rows = pl.ds(pl.multiple_of(ci * CHUNK, CHUNK), CHUNK)
        slab = pl.ds(pl.multiple_of((ci // 2) * _GDN_PAIR, _GDN_PAIR), _GDN_PAIR)
        colv = col_ref[slot_2, rows, :]
        grow2 = grow_ref[slot_2, :, slab]

        def stack(ref, width):
            return jnp.concatenate([ref[slot_2, rows, h * width:(h + 1) * width] for h in heads], axis=0)

        q = stack(q_ref, GDN_DK)
        k = stack(k_ref, GDN_DK)
        v = stack(v_ref, GDN_DV)
        beta = jnp.concatenate([colv[:, h:h + 1] for h in heads], axis=0)
        gc = jnp.concatenate([colv[:, GDN_HEADS + h:GDN_HEADS + h + 1] for h in heads], axis=0)
        gr = jnp.concatenate([grow2[GDN_HEADS + h:GDN_HEADS + h + 1, odd * CHUNK:(odd + 1) * CHUNK]
                              for h in heads], axis=1)
        last = [colv[CHUNK - 1:CHUNK, GDN_HEADS + h:GDN_HEADS + h + 1] for h in heads]
        g_last = jnp.concatenate([jnp.broadcast_to(x, (CHUNK, 1)) for x in last], axis=0)
        decay = jnp.where(tril, jnp.exp(jnp.where(tril, gc - gr, 0.0)), 0.0)
        kb = k * beta
        eg = jnp.exp(gc)
        low = jnp.where(strict, _dot_nt(kb, k) * decay, 0.0)
        a_ref[slot_2, ci] = jnp.where(tril, _dot_nt(q, k) * decay, 0.0)
        kd_ref[slot_2, ci] = k * jnp.exp(g_last - gc)
        qd_ref[slot_2, ci] = q * eg
        egl_ref[slot_2, ci] = jnp.concatenate([jnp.broadcast_to(jnp.exp(x), (1, GDN_DV)) for x in last]
                                              + [jnp.zeros((8 - GDN_HEADS, GDN_DV), _F32)], axis=0)
        return low, jnp.concatenate([v * beta, kb * eg], axis=1)

    gnw = gnw_ref[...]

    def stage3_chunks(cis):
        states = [state_ref[h] for h in heads]
        for ci in cis:
            u, w, qd, kd = u_ref[slot_1, ci], w_ref[slot_1, ci], qd_ref[slot_1, ci], kd_ref[slot_1, ci]
            egl = egl_ref[slot_1, ci]
            rows = pl.ds(pl.multiple_of(ci * CHUNK, CHUNK), CHUNK)
            hrows = [slice(h * CHUNK, (h + 1) * CHUNK) for h in heads]
            wqs = [_dot(jnp.concatenate([w[rs], qd[rs]], axis=0), states[h]) for h, rs in zip(heads, hrows)]
            yield
            v_news = [u[rs] - wq[:CHUNK] for rs, wq in zip(hrows, wqs)]
            for h in heads:
                states[h] = states[h] * egl[h:h + 1, :] + _dot_tn(kd[hrows[h]], v_news[h])
            o = (jnp.concatenate([wq[CHUNK:] for wq in wqs], axis=0)
                 + _dot(a_ref[slot_1, ci], jnp.concatenate(v_news, axis=0)))
            for h in heads:
                lanes = slice(h * GDN_DV, (h + 1) * GDN_DV)
                oh = o[hrows[h]]
                oh = oh * lax.rsqrt(jnp.mean(oh * oh, axis=-1, keepdims=True) + EPS) * gnw
                o_ref[0, rows, lanes] = (oh * _silu(gz_ref[0, rows, lanes])).astype(_BF16)
            yield
        for h in heads:
            state_ref[h] = states[h]

    def stage2_chunks(cis):
        lows, rhss = zip(*[prep_gram(ci, j % 2) for j, ci in enumerate(cis)])
        yield
        tinvs = []
        yield from _stacked_unit_lower_inverses(lows, tinvs)
        sols = [_dot(tinv, rhs) for tinv, rhs in zip(tinvs, rhss)]
        for ci, sol in zip(cis, sols):
            u_ref[slot_2, ci] = sol[:, :GDN_DV]
            w_ref[slot_2, ci] = sol[:, GDN_DV:]

    def pipeline_body(pi, carry):
        cis = [pi * _PREP_UNROLL + j for j in range(_PREP_UNROLL)]
        streams = [stage3_chunks(cis), stage2_chunks(cis)]
        first_round = True
        while streams:
            for stream in list(streams):
                if next(stream, StopIteration) is StopIteration:
                    streams.remove(stream)
            if first_round:
                stage1_half(pi)
                first_round = False
        return carry

    assert tile // CHUNK == 2 * _PREP_UNROLL
    lax.fori_loop(0, 2, pipeline_body, 0)


def _gdn(conv_in, conv_w, sm, smt, a_log, dt_bias, gz, gdn_norm_w, tile):
    batch, seq, _ = conv_in.shape
    prow = jnp.zeros((2, LANES), _F32)
    prow = prow.at[0, GDN_HEADS:2 * GDN_HEADS].set(a_log.astype(_F32))
    prow = prow.at[1, GDN_HEADS:2 * GDN_HEADS].set(dt_bias.astype(_F32))
    pcol = jnp.broadcast_to(prow[:, :8, None], (2, 8, tile // 2))
    n_chunks = tile // CHUNK
    assert _PREP_UNROLL % 2 == 0
    tiles_per_seq = seq // tile
    n_tiles = batch * tiles_per_seq
    kern = functools.partial(_gdn_kernel, tile=tile, tiles_per_seq=tiles_per_seq)

    def stage1_tile(i):
        i = jnp.minimum(i, n_tiles - 1)
        return i // tiles_per_seq, i % tiles_per_seq

    def stage3_tile(i):
        i = jnp.maximum(i - 2, 0)
        return i // tiles_per_seq, i % tiles_per_seq

    rows_of = lambda pick: (lambda i: (*pick(i), 0))
    const = lambda i: (0, 0)
    return pl.pallas_call(
        kern,
        out_shape=jax.ShapeDtypeStruct((batch, seq, GDN_W), _BF16),
        grid=(n_tiles + 2,),
        in_specs=[
            pl.BlockSpec((1, tile, GDN_CONV_W), rows_of(stage1_tile)),
            pl.BlockSpec((CONV_K, GDN_CONV_W), const),
            pl.BlockSpec((1, tile, LANES), rows_of(stage1_tile)),
            pl.BlockSpec((1, LANES, tile), lambda i: (stage1_tile(i)[0], 0, stage1_tile(i)[1])),
            pl.BlockSpec((2, LANES), const),
            pl.BlockSpec((2, 8, tile // 2), lambda i: (0, 0, 0)),
            pl.BlockSpec((1, tile, GDN_W), rows_of(stage3_tile)),
            pl.BlockSpec((1, GDN_DV), const),
        ],
        out_specs=pl.BlockSpec((1, tile, GDN_W), rows_of(stage3_tile)),
        scratch_shapes=[
            pltpu.VMEM((tile + 8, GDN_CONV_W), _F32),
            pltpu.VMEM((2, tile, GDN_QK_W), _F32),
            pltpu.VMEM((2, tile, GDN_QK_W), _F32),
            pltpu.VMEM((2, tile, GDN_W), _F32),
            pltpu.VMEM((2, tile, LANES), _F32),
            pltpu.VMEM((2, 8, tile), _F32),
            pltpu.VMEM((GDN_HEADS, GDN_DK, GDN_DV), _F32),
            pltpu.VMEM((2, n_chunks, _STACK, GDN_DV), _F32),
            pltpu.VMEM((2, n_chunks, _STACK, GDN_DV), _F32),
            pltpu.VMEM((2, n_chunks, _STACK, GDN_DK), _F32),
            pltpu.VMEM((2, n_chunks, _STACK, GDN_DK), _F32),
            pltpu.VMEM((2, n_chunks, _STACK, _STACK), _F32),
            pltpu.VMEM((2, n_chunks, 8, GDN_DV), _F32),
        ],
        compiler_params=pltpu.CompilerParams(dimension_semantics=("arbitrary",),
                                             vmem_limit_bytes=VMEM_LIMIT),
        name="gdn",
    )(conv_in, conv_w.astype(_F32), sm, smt, prow, pcol, gz, gdn_norm_w.reshape(1, GDN_DV).astype(_F32))


_GROUP_W = CMP_STRIDE * KV_W
_HID2 = NSA_KV_HEADS * CMP_HIDDEN


def _compress_weights(pe, w1, w2):
    eye = jnp.eye(NSA_KV_HEADS, dtype=_F32)
    halves = []
    pes = []
    for part in range(CMP_LEN // CMP_STRIDE):
        w1p = w1[part * CMP_STRIDE:(part + 1) * CMP_STRIDE].astype(_F32)
        big = jnp.einsum('ldf,hg->lhdgf', w1p, eye).reshape(_GROUP_W, _HID2)
        halves.append(big.astype(_BF16))
        pep = pe[part * CMP_STRIDE:(part + 1) * CMP_STRIDE].astype(_F32)
        pes.append(jnp.broadcast_to(pep[:, None, :], (CMP_STRIDE, NSA_KV_HEADS, NSA_DH)).reshape(1, _GROUP_W))
    w2big = jnp.einsum('fd,hg->hfgd', w2.astype(_F32), eye).reshape(_HID2, KV_W).astype(_BF16)
    pe2 = jnp.broadcast_to(jnp.stack(pes, 0), (2, 8, _GROUP_W)).astype(_BF16)
    return halves[0], halves[1], w2big, pe2


def _compress_kernel(kc_ref, vc_ref, kwa_ref, kwb_ref, kw2_ref, kpe_ref, vwa_ref, vwb_ref, vw2_ref, vpe_ref,
                     ko_ref, vo_ref):
    def mlp(g_ref, wa_ref, wb_ref, w2_ref, pe_ref):
        g = g_ref[0]
        ncp = g.shape[0]
        p0 = _dot(g, wa_ref[...])
        p1 = _dot(g, wb_ref[...])
        const = _dot(pe_ref[0], wa_ref[...]) + _dot(pe_ref[1], wb_ref[...])
        pre = p0 + pltpu.roll(p1, ncp - 1, 0) + const[0:1, :]
        return _dot(_silu(pre).astype(_BF16), w2_ref[...])

    ko_ref[0] = mlp(kc_ref, kwa_ref, kwb_ref, kw2_ref, kpe_ref).astype(_BF16)
    vo_ref[0] = mlp(vc_ref, vwa_ref, vwb_ref, vw2_ref, vpe_ref).T.astype(_BF16)


def _compress(kc, vc, kweights, vweights):
    batch, ncp, _ = kc.shape
    full = lambda a: pl.BlockSpec(a.shape, lambda b: (0,) * a.ndim)
    per_b = pl.BlockSpec((1, ncp, _GROUP_W), lambda b: (b, 0, 0))
    return pl.pallas_call(
        _compress_kernel,
        out_shape=(jax.ShapeDtypeStruct((batch, ncp, KV_W), _BF16),
                   jax.ShapeDtypeStruct((batch, KV_W, ncp), _BF16)),
        grid=(batch,),
        in_specs=[per_b, per_b] + [full(a) for a in kweights] + [full(a) for a in vweights],
        out_specs=(pl.BlockSpec((1, ncp, KV_W), lambda b: (b, 0, 0)),
                   pl.BlockSpec((1, KV_W, ncp), lambda b: (b, 0, 0))),
        compiler_params=pltpu.CompilerParams(dimension_semantics=("arbitrary",),
                                             vmem_limit_bytes=VMEM_LIMIT),
        name="compress",
    )(kc, vc, *kweights, *vweights)


def _overlap_t(n_sel, ncp, n_cmp):
    cs = CMP_STRIDE * np.arange(ncp)
    ss = SEL_BLOCK * np.arange(n_sel)
    ov = ((cs[None, :] < ss[:, None] + SEL_BLOCK) & (cs[None, :] + CMP_LEN > ss[:, None])
          & (np.arange(ncp)[None, :] < n_cmp))
    return jnp.asarray(ov.astype(np.float32), dtype=_BF16)


def _key_onehots(n_tiles, with_blocks):
    key = np.arange(n_tiles * TQ)[:, None]
    col = np.arange(LANES)[None, :]
    ek = np.broadcast_to((col == 8) | (col == 9), (n_tiles * TQ, LANES))
    if with_blocks:
        ek = ek | (col == key // SEL_BLOCK)
    return jnp.asarray(ek.astype(np.float32), dtype=_BF16)


def _nsa_kernel(q_ref, ks_ref, kw_ref, vst_ref, vwt_ref, kc_ref, vct_ref, tab_ref, far_ref, ovt_ref, eks_ref,
                ekw_ref, smt_ref, nz_ref, o_ref, cb_ref, sc_ref, cnt_ref, mb_ref, ss_ref, st_ref, ps_ref, ocw_ref):
    a = pl.program_id(1)
    ncp = kc_ref.shape[1]
    n_sel = ovt_ref.shape[0]
    heads = range(NSA_KV_HEADS)
    drows = [slice(h * NSA_DH, (h + 1) * NSA_DH) for h in heads]
    vrows = [slice(h * VT_ROWS, (h + 1) * VT_ROWS) for h in heads]
    qcols = [slice(h * GQ_W, (h + 1) * GQ_W) for h in heads]

    qrow = lax.broadcasted_iota(jnp.int32, (KV_W, QL), 0)

    def transposed_queries(h):
        qt = (q_ref[0, :, qcols[h]].astype(_F32) * (NSA_DH ** -0.5 * LOG2E)).T
        qt4 = jnp.concatenate([qt[g * NSA_DH:(g + 1) * NSA_DH, :] for g in range(NSA_GROUP)], axis=1)
        return jnp.where(qrow // NSA_DH == h, jnp.concatenate([qt4] * NSA_KV_HEADS, axis=0), 0.0).astype(_BF16)

    qt_b = [transposed_queries(h) for h in heads]
    far_rows = [far_ref[h] for h in heads]
    zero_rows = jnp.zeros((LANES - 16, QL), _BF16)

    def augmented_queries(h, mask_rows):
        extra = jnp.concatenate([mask_rows, far_rows[h]], axis=0).astype(_BF16)
        return jnp.concatenate([qt_b[h], extra, zero_rows], axis=0)

    def augmented_keys(k_ref, ek, key0, n_tiles):
        keys = pl.ds(pl.multiple_of(key0, TQ), n_tiles * TQ)
        return jnp.concatenate([k_ref[0, keys, :], ek], axis=1)

    def gate(h, branch):
        first = 8 + 8 * branch + NSA_GROUP * h
        return _sigmoid(jnp.concatenate([smt_ref[0, first + g:first + g + 1, :] for g in range(NSA_GROUP)], axis=1))

    def normalized(acc):
        return acc[0:NSA_DH, :] * (1.0 / acc[NSA_DH:NSA_DH + 1, :])

    win_t0 = jnp.maximum(a - (WIN_TILES - 1), 0)
    win_keys = pl.ds(pl.multiple_of(win_t0 * TQ, TQ), WIN_TILES * TQ)
    win_start = pl.multiple_of(jnp.maximum(8 * a - CMP_NEAR_BACK, 0), 8)
    tab_start = pl.multiple_of(win_start - (8 * a - CMP_NEAR_BACK), 8)

    def compressed_scores(n_rows):
        crow = lax.broadcasted_iota(jnp.int32, (n_rows, QL), 0)
        for h in heads:
            cb_ref[h, 0:n_rows, :] = jnp.where(crow < win_start, far_rows[h][2:3, :], NEG)
            cb_ref[h, pl.ds(win_start, CMP_NEAR), :] = tab_ref[h, TAB_CMP, pl.ds(tab_start, CMP_NEAR), :]
        k_cmp = kc_ref[0, 0:n_rows, :]
        return [_dot(k_cmp, qt_b[h]) + cb_ref[h, 0:n_rows, :] for h in heads]

    def compressed_branch(s_c, acc_w):
        n_rows = s_c[0].shape[0]
        m_c = [jnp.max(s, axis=0, keepdims=True) for s in s_c]
        e_c = [jnp.exp2(s - m) for s, m in zip(s_c, m_c)]
        l_c = [jnp.sum(e, axis=0, keepdims=True) for e in e_c]
        p_c = [e * jnp.where(m > 0.5 * NEG, 1.0 / l, 0.0) for e, m, l in zip(e_c, m_c, l_c)]
        o_c = [_dot(vct_ref[0, drows[h], 0:n_rows], p_c[h].astype(_BF16)) for h in heads]
        for h in heads:
            ocw_ref[h] = gate(h, 0) * o_c[h] + gate(h, 2) * normalized(acc_w[h])
        ovt = ovt_ref[:, 0:n_rows]
        jrow = lax.broadcasted_iota(jnp.int32, (n_sel, TQ), 0)
        cur = (a * TQ + lax.broadcasted_iota(jnp.int32, (n_sel, TQ), 1)) // SEL_BLOCK
        forced = (jrow == 0) | (jrow == cur) | (jrow == cur - 1)
        for h in heads:
            psum = p_c[h][:, 0:TQ]
            for g in range(1, NSA_GROUP):
                psum = psum + p_c[h][:, g * TQ:(g + 1) * TQ]
            hi = psum.astype(_BF16)
            r1 = psum - hi.astype(_F32)
            mid = r1.astype(_BF16)
            lo = (r1 - mid.astype(_F32)).astype(_BF16)
            imp = _dot(ovt, hi) + _dot(ovt, mid) + _dot(ovt, lo)
            sc_ref[h] = jnp.where(jrow <= cur, imp + jnp.where(forced, FORCE_BONUS, 0.0), NEG)

    win_entries = []
    for i in range(WIN_TILES):
        d = a - (win_t0 + i)
        win_entries.append(jnp.where(d < 0, TAB_MASKED, jnp.where(d == WIN_TILES - 1, TAB_WIN_EDGE, d)))
    def front_stage(n_rows):
        k_aug_w = augmented_keys(kw_ref, ekw_ref[...], win_t0 * TQ, WIN_TILES)
        no_mask = jnp.zeros((8, QL), _F32)
        s_w = [_dot(k_aug_w, augmented_queries(h, no_mask)) for h in heads]
        s_c = compressed_scores(n_rows)
        s_wb = [jnp.concatenate([s_w[h][i * TQ:(i + 1) * TQ, :] + tab_ref[h, win_entries[i]]
                                 for i in range(WIN_TILES)], axis=0) for h in heads]
        m_w = [jnp.max(s, axis=0, keepdims=True) for s in s_wb]
        acc_w = [_dot(vwt_ref[0, vrows[h], win_keys], jnp.exp2((s_wb[h] - m_w[h]).astype(_BF16))) for h in heads]
        compressed_branch(s_c, acc_w)

    half_rows = ncp // 2
    needs_all = 8 * a + 8 > half_rows

    @pl.when(jnp.logical_not(needs_all))
    def _():
        front_stage(half_rows)

    @pl.when(needs_all)
    def _():
        front_stage(ncp)

    cnt_ref[...] = jnp.zeros(cnt_ref.shape, _F32)
    sub = lax.broadcasted_iota(jnp.int32, (8, TQ), 0)
    last_valid = (TQ // SEL_BLOCK) * (a + 1) - 1
    for chunk in range(n_sel // RANK_CHUNK):
        for row_chunk in range(n_sel // RANK_ROWS):
            row_groups = range(row_chunk * RANK_ROWS // 8, (row_chunk + 1) * RANK_ROWS // 8)

            @pl.when((chunk * RANK_CHUNK <= last_valid) & (row_chunk * RANK_ROWS <= last_valid))
            def _():
                for h in heads:
                    rows = {r: sc_ref[h, 8 * r:8 * r + 8, :] for r in row_groups}
                    cnts = {r: cnt_ref[h, 8 * r:8 * r + 8, :] for r in row_groups}
                    for jp in range(chunk * RANK_CHUNK, (chunk + 1) * RANK_CHUNK):
                        srow = sc_ref[h, jp:jp + 1, :]
                        for r in row_groups:
                            if 8 * r > jp:
                                inc = jnp.where(srow >= rows[r], 1.0, 0.0)
                            elif 8 * r + 7 < jp:
                                inc = jnp.where(srow > rows[r], 1.0, 0.0)
                            else:
                                tie = jnp.where(sub + 8 * r > jp, 1.0, 0.0)
                                inc = jnp.where(srow > rows[r], 1.0, jnp.where(srow >= rows[r], tie, 0.0))
                            cnts[r] = cnts[r] + inc
                    for r in row_groups:
                        cnt_ref[h, 8 * r:8 * r + 8, :] = cnts[r]

    for h in heads:
        mb = jnp.where(cnt_ref[h] < SEL_TOP_N, 0.0, NEG)
        mb_ref[h] = jnp.concatenate([mb] * NSA_GROUP, axis=1)

    eks = eks_ref[...]
    blocks_per_group = GROUP_TILES * (TQ // SEL_BLOCK)
    group_keys = GROUP_TILES * TQ
    last_group = a // GROUP_TILES

    def sel_scores(gi):
        k_aug = augmented_keys(ks_ref, eks, gi * group_keys, GROUP_TILES)
        mask_rows = pl.ds(pl.multiple_of(gi * blocks_per_group, 8), blocks_per_group)
        return [_dot(k_aug, augmented_queries(h, mb_ref[h, mask_rows, :])) for h in heads]

    def sel_values(h, gi):
        return vst_ref[0, vrows[h], pl.ds(pl.multiple_of(gi * group_keys, group_keys), group_keys)]

    def sel_step(gi, carry, near, last):
        s_next = None if last else sel_scores(gi + 1)
        pv_prev = [_dot(sel_values(h, jnp.maximum(gi - 1, 0)), ps_ref[h]) for h in heads]
        entries = []
        if near:
            for i in range(GROUP_TILES):
                d = a - (gi * GROUP_TILES + i)
                entries.append(jnp.where(d < 0, TAB_MASKED, jnp.where(d >= N_NEAR, TAB_ZERO, d)))
        new_carry = []
        for h in heads:
            m, alpha_prev, acc = carry[h]
            s = ss_ref[h]
            if near:
                s = jnp.concatenate([s[i * TQ:(i + 1) * TQ, :] + tab_ref[h, entries[i]]
                                     for i in range(GROUP_TILES)], axis=0)
                st_ref[h] = s
            m_new = jnp.maximum(m, jnp.max(s, axis=0, keepdims=True))
            alpha = jnp.exp2(m - m_new)
            acc = alpha_prev * acc + pv_prev[h]
            s = st_ref[h] if near else ss_ref[h]
            ps_ref[h] = jnp.exp2((s - m_new).astype(_BF16))
            new_carry.append((m_new, alpha, acc))
        if not last:
            for h in heads:
                ss_ref[h] = s_next[h]
        return tuple(new_carry)

    s_first = sel_scores(0)
    for h in heads:
        ss_ref[h] = s_first[h]
    ps_ref[...] = jnp.zeros(ps_ref.shape, _BF16)
    carry = tuple((jnp.full((1, QL), NEG, _F32), jnp.ones((1, QL), _F32), jnp.zeros((VT_ROWS, QL), _F32))
                  for _ in heads)
    n_far = jnp.maximum(last_group - N_NEAR // GROUP_TILES, 0)

    carry = lax.fori_loop(0, n_far, functools.partial(sel_step, near=False, last=False), carry)
    carry = lax.fori_loop(n_far, last_group, functools.partial(sel_step, near=True, last=False), carry)
    carry = sel_step(last_group, carry, near=True, last=True)

    for h in heads:
        _, alpha_s, acc_s = carry[h]
        acc_s = alpha_s * acc_s + _dot(sel_values(h, last_group), ps_ref[h])
        o_t = ocw_ref[h] + gate(h, 1) * normalized(acc_s)
        stacked = jnp.concatenate([o_t[:, g * TQ:(g + 1) * TQ] for g in range(NSA_GROUP)], axis=0)
        o_ref[0, :, qcols[h]] = (stacked.T * _silu(nz_ref[0, :, qcols[h]])).astype(_BF16)


def _nsa(nq, ks, kw, vst, vwt, kcmp, vcmpt, tab, far, ovt, smt, nz):
    batch, seq, _ = nq.shape
    ncp = kcmp.shape[1]
    n_sel = seq // SEL_BLOCK
    per_b = lambda shape: pl.BlockSpec((1,) + shape, lambda b, a: (b, 0, 0))
    const = lambda shape: pl.BlockSpec(shape, lambda b, a: (0,) * len(shape), pipeline_mode=pl.Buffered(1))
    tile = lambda width: pl.BlockSpec((1, TQ, width), lambda b, a: (b, a, 0))
    vt_shape = (NSA_KV_HEADS * VT_ROWS, seq)
    per_head = lambda shape, dtype: pltpu.VMEM((NSA_KV_HEADS,) + shape, dtype)
    return pl.pallas_call(
        _nsa_kernel,
        out_shape=jax.ShapeDtypeStruct((batch, seq, NSA_W), _BF16),
        grid=(batch, seq // TQ),
        in_specs=[
            tile(NSA_W),
            per_b((seq, KV_W)), per_b((seq, KV_W)), per_b(vt_shape), per_b(vt_shape),
            per_b((ncp, KV_W)), per_b((KV_W, ncp)),
            const((NSA_KV_HEADS, N_TAB, TQ, QL)), const((NSA_KV_HEADS, 8, QL)),
            const((n_sel, ncp)), const((GROUP_TILES * TQ, LANES)), const((WIN_TILES * TQ, LANES)),
            pl.BlockSpec((1, LANES, TQ), lambda b, a: (b, 0, a)),
            tile(NSA_W),
        ],
        out_specs=tile(NSA_W),
        scratch_shapes=[per_head((ncp, QL), _F32), per_head((n_sel, TQ), _F32),
                        per_head((n_sel, TQ), _F32), per_head((n_sel, QL), _F32),
                        per_head((GROUP_TILES * TQ, QL), _F32), per_head((GROUP_TILES * TQ, QL), _F32),
                        per_head((GROUP_TILES * TQ, QL), _BF16), per_head((NSA_DH, QL), _F32)],
        compiler_params=pltpu.CompilerParams(dimension_semantics=("arbitrary", "arbitrary"),
                                             vmem_limit_bytes=VMEM_LIMIT),
        name="nsa",
    )(nq, ks, kw, vst, vwt, kcmp, vcmpt, tab, far, ovt, _key_onehots(GROUP_TILES, True),
      _key_onehots(WIN_TILES, False), smt, nz)


def _out_proj_kernel(og_ref, on_ref, x_ref, w_ref, fw_ref, o_ref):
    h = x_ref[...] + _dot(og_ref[...], w_ref[0:GDN_W, :]) + _dot(on_ref[...], w_ref[GDN_W:, :])
    o_ref[...] = h * lax.rsqrt(jnp.mean(h * h, axis=-1, keepdims=True) + EPS) * fw_ref[...]


def _out_proj(o_g, o_n, x2d, w_out, final_norm_w, tm):
    n = x2d.shape[0]
    row = lambda w: pl.BlockSpec((tm, w), lambda i: (i, 0))
    return pl.pallas_call(
        _out_proj_kernel,
        out_shape=jax.ShapeDtypeStruct((n, D_MODEL), _F32),
        grid=(n // tm,),
        in_specs=[row(GDN_W), row(NSA_W), row(D_MODEL),
                  pl.BlockSpec((GDN_W + NSA_W, D_MODEL), lambda i: (0, 0)),
                  pl.BlockSpec((1, D_MODEL), lambda i: (0, 0))],
        out_specs=row(D_MODEL),
        compiler_params=pltpu.CompilerParams(dimension_semantics=("arbitrary",),
                                             vmem_limit_bytes=VMEM_LIMIT),
        name="out_proj",
    )(o_g, o_n, x2d, w_out.astype(_BF16), final_norm_w.reshape(1, D_MODEL).astype(_F32))


def kernel(x, norm_w, w_in, conv_w, a_log, dt_bias, gdn_norm_w, cmp_pe_k, cmp_pe_v, cmp_k_w1, cmp_k_w2,
           cmp_v_w1, cmp_v_w2, w_out, rel_bias, final_norm_w):
    batch, seq, _ = x.shape
    assert w_in.shape[0] == 1, "single-layer problem"
    assert seq % (8 * TQ) == 0
    n = batch * seq
    ncp = seq // CMP_STRIDE
    n_cmp = (seq - CMP_LEN) // CMP_STRIDE + 1
    tm = min(512, seq)

    x2d = x.reshape(n, D_MODEL)
    tab, far = _bias_tables(rel_bias)
    (conv_in, gz, sm, smt, nq, kc, vc, ks, kw, vst, vwt, nz) = _in_proj(
        x2d, norm_w[0], _regroup_w_in(w_in[0]), batch, seq, tm)

    o_g = _gdn(conv_in.reshape(batch, seq, GDN_CONV_W), conv_w[0], sm.reshape(batch, seq, LANES), smt,
               a_log[0], dt_bias[0], gz.reshape(batch, seq, GDN_W), gdn_norm_w[0], tm)

    kcmp, vcmpt = _compress(kc.reshape(batch, ncp, _GROUP_W), vc.reshape(batch, ncp, _GROUP_W),
                            _compress_weights(cmp_pe_k[0], cmp_k_w1[0], cmp_k_w2[0]),
                            _compress_weights(cmp_pe_v[0], cmp_v_w1[0], cmp_v_w2[0]))

    o_n = _nsa(nq.reshape(batch, seq, NSA_W), ks.reshape(batch, seq, KV_W), kw.reshape(batch, seq, KV_W),
               vst, vwt, kcmp, vcmpt, tab, far, _overlap_t(seq // SEL_BLOCK, ncp, n_cmp), smt,
               nz.reshape(batch, seq, NSA_W))

    out = _out_proj(o_g.reshape(n, GDN_W), o_n.reshape(n, NSA_W), x2d, w_out[0], final_norm_w, 2 * tm)
    return out.reshape(batch, seq, D_MODEL)
```

```python
import functools
import math

import jax
import jax.numpy as jnp
import numpy as np
from jax import lax
from jax.experimental import pallas as pl
from jax.experimental.pallas import tpu as pltpu

D_MODEL = 1024
GDN_HEADS = 4
GDN_DK = 128
GDN_DV = 128
CONV_K = 4
CHUNK = 64
NSA_HEADS = 8
NSA_KV_HEADS = 2
NSA_GROUP = NSA_HEADS // NSA_KV_HEADS
NSA_DH = 64
CMP_LEN = 32
CMP_STRIDE = 16
CMP_HIDDEN = 128
SEL_BLOCK = 64
SEL_TOP_N = 16
WINDOW = 512
FORCE_BONUS = 1e4
N_BUCKETS = 32
MAX_DISTANCE = 1024
EPS = 1e-6
NEG = -1e30

GDN_QK_W = GDN_HEADS * GDN_DK
GDN_W = GDN_HEADS * GDN_DV
GDN_CONV_W = 2 * GDN_QK_W + GDN_W
NSA_W = NSA_HEADS * NSA_DH
KV_W = NSA_KV_HEADS * NSA_DH
PROJ_SIZES = (GDN_QK_W, GDN_QK_W, GDN_W, GDN_W, GDN_HEADS, GDN_HEADS,
              NSA_W, KV_W, KV_W, KV_W, KV_W, KV_W, KV_W, 3 * NSA_HEADS, NSA_W)

LANES = 128
TQ = 128
QL = NSA_GROUP * TQ
GQ_W = NSA_GROUP * NSA_DH
N_NEAR = 8
TAB_ZERO = N_NEAR
TAB_MASKED = N_NEAR + 1
TAB_WIN_EDGE = N_NEAR + 2
TAB_CMP = N_NEAR + 3
N_TAB = N_NEAR + 4
GROUP_TILES = 4
WIN_TILES = WINDOW // TQ + 1
TILE_PAIR = 2
RANK_CHUNK = 16
RANK_ROWS = 32
VT_ROWS = 80
LOG2E = math.log2(math.e)
CMP_NEAR = 64
CMP_NEAR_BACK = 56
VMEM_LIMIT = 56 * 1024 * 1024

_F32 = jnp.float32
_BF16 = jnp.bfloat16


def _dot(a, b):
    return jnp.dot(a, b, preferred_element_type=_F32)


def _dot_nt(a, b):
    return lax.dot_general(a, b, (((1,), (1,)), ((), ())), preferred_element_type=_F32)


def _dot_tn(a, b):
    return lax.dot_general(a, b, (((0,), (0,)), ((), ())), preferred_element_type=_F32)


def _sigmoid(x):
    return 0.5 * jnp.tanh(0.5 * x) + 0.5


def _silu(x):
    return x * _sigmoid(x)


def _bias_tab_kernel(rel_ref, o_ref, far_ref):
    hk = pl.program_id(0)
    e = pl.program_id(1)
    is_cmp = e == TAB_CMP
    is_edge = e == TAB_WIN_EDGE
    base = jnp.where(e < N_NEAR, e * TQ,
                     jnp.where(is_edge, WINDOW, CMP_STRIDE * CMP_NEAR_BACK - (CMP_LEN - 1)))
    row_scale = jnp.where(is_cmp, CMP_STRIDE, 1)
    upper = jnp.where(is_edge, WINDOW, 1 << 30)
    row = lax.broadcasted_iota(jnp.int32, (TQ, TQ), 0)
    tok = lax.broadcasted_iota(jnp.int32, (TQ, TQ), 1)
    dist = base + tok - row_scale * row
    n = jnp.maximum(dist, 0)
    max_exact = N_BUCKETS // 2
    large = max_exact + (jnp.log(jnp.maximum(n, max_exact).astype(_F32) / max_exact)
                         / math.log(MAX_DISTANCE / max_exact) * (N_BUCKETS - max_exact)).astype(jnp.int32)
    bucket = jnp.where(n < max_exact, n, jnp.minimum(large, N_BUCKETS - 1))
    ok = (dist >= 0) & (dist < upper)
    for g in range(NSA_GROUP):
        head = hk * NSA_GROUP + g
        lanes = slice(g * TQ, (g + 1) * TQ)
        far = jnp.full((8, TQ), rel_ref[N_BUCKETS - 1, head] * LOG2E, _F32)
        far_hi = far.astype(_BF16).astype(_F32)
        far_lo = (far - far_hi).astype(_BF16).astype(_F32)
        val = jnp.zeros((TQ, TQ), _F32)
        for b in range(N_BUCKETS):
            val = jnp.where(bucket == b, rel_ref[b, head], val)
        val = val * LOG2E - jnp.where(is_cmp, 0.0, far_hi[0:1, :] + far_lo[0:1, :])
        val = jnp.where(ok, val, NEG)
        val = jnp.where(e == TAB_ZERO, 0.0, jnp.where(e == TAB_MASKED, NEG, val))
        o_ref[0, 0, :, lanes] = val

        sub = lax.broadcasted_iota(jnp.int32, (8, TQ), 0)
        far_ref[0, :, lanes] = jnp.where(sub == 0, far_hi, jnp.where(sub == 1, far_lo,
                                                                     jnp.where(sub == 2, far, 0.0)))


def _bias_tables(rel_bias):
    return pl.pallas_call(
        _bias_tab_kernel,
        out_shape=(jax.ShapeDtypeStruct((NSA_KV_HEADS, N_TAB, TQ, QL), _F32),
                   jax.ShapeDtypeStruct((NSA_KV_HEADS, 8, QL), _F32)),
        grid=(NSA_KV_HEADS, N_TAB),
        in_specs=[pl.BlockSpec(memory_space=pltpu.SMEM)],
        out_specs=(pl.BlockSpec((1, 1, TQ, QL), lambda h, e: (h, e, 0, 0)),
                   pl.BlockSpec((1, 8, QL), lambda h, e: (h, 0, 0))),
        name="bias_tables",
    )(rel_bias.astype(_F32))


_C_CONV = (0, GDN_CONV_W)
_C_GZ = (_C_CONV[1], _C_CONV[1] + GDN_W)
_C_NQ = (_C_GZ[1], _C_GZ[1] + NSA_W)
_C_NZ = (_C_NQ[1], _C_NQ[1] + NSA_W)
_C_SMALL_KC = (_C_NZ[1], _C_NZ[1] + 2 * LANES)
_C_VC_KS = (_C_SMALL_KC[1], _C_SMALL_KC[1] + 2 * LANES)
_C_KW_VS = (_C_VC_KS[1], _C_VC_KS[1] + 2 * LANES)
_C_VW = (_C_KW_VS[1], _C_KW_VS[1] + LANES)
_W_COLS = _C_VW[1]
assert KV_W == LANES


def _regroup_w_in(w):
    gq, gk, gv, gz, gb, ga, nq, kc, vc, ks, vs, kw, vw, ng, nz = jnp.split(
        w, [int(p) for p in np.cumsum(PROJ_SIZES)[:-1]], axis=-1)
    small = jnp.concatenate([gb, ga, ng], axis=-1)
    small = jnp.pad(small, ((0, 0), (0, LANES - small.shape[-1])))
    return jnp.concatenate([gq, gk, gv, gz, nq, nz, small, kc, vc, ks, kw, vs, vw], axis=-1).astype(_BF16)


def _in_proj_kernel(x_ref, nw_ref, w_ref, conv_ref, gz_ref, sm_ref, smt_ref, nq_ref, kc_ref, vc_ref,
                    ks_ref, kw_ref, vst_ref, vwt_ref, nz_ref):
    x = x_ref[...]
    u = x * lax.rsqrt(jnp.mean(x * x, axis=-1, keepdims=True) + EPS) * nw_ref[...]
    u = u.astype(_BF16)

    def proj(c):
        return _dot(u, w_ref[:, c[0]:c[1]])

    def proj_pair(c):
        both = proj(c)
        return both[:, :LANES], both[:, LANES:]

    conv_ref[...] = proj(_C_CONV)
    gz_ref[...] = proj(_C_GZ)
    nq_ref[...] = proj(_C_NQ).astype(_BF16)
    nz_ref[...] = proj(_C_NZ)
    sm, kc = proj_pair(_C_SMALL_KC)
    vc, ks = proj_pair(_C_VC_KS)
    kw, vs = proj_pair(_C_KW_VS)
    sm_ref[...] = sm
    smt_ref[0] = sm.T
    kc_ref[...] = kc.astype(_BF16)
    vc_ref[...] = vc.astype(_BF16)
    ks_ref[...] = ks.astype(_BF16)
    kw_ref[...] = kw.astype(_BF16)
    tm = x.shape[0]
    ones_pad = (lax.broadcasted_iota(jnp.int32, (VT_ROWS - NSA_DH, tm), 0) == 0).astype(_F32)

    def values_t(v):
        vt = v.T
        parts = []
        for h in range(NSA_KV_HEADS):
            parts += [vt[h * NSA_DH:(h + 1) * NSA_DH, :], ones_pad]
        return jnp.concatenate(parts, axis=0).astype(_BF16)

    vst_ref[0] = values_t(vs)
    vwt_ref[0] = values_t(proj(_C_VW))


def _in_proj(x2d, norm_w, w_cat, batch, seq, tm):
    n = x2d.shape[0]
    tiles_per_seq = seq // tm
    row = lambda w: pl.BlockSpec((tm, w), lambda i: (i, 0))
    tr_map = lambda i: (i // tiles_per_seq, 0, i % tiles_per_seq)
    tr = pl.BlockSpec((1, LANES, tm), tr_map)
    trv = pl.BlockSpec((1, NSA_KV_HEADS * VT_ROWS, tm), tr_map)
    sds = jax.ShapeDtypeStruct
    return pl.pallas_call(
        _in_proj_kernel,
        out_shape=(
            sds((n, GDN_CONV_W), _F32), sds((n, GDN_W), _F32), sds((n, LANES), _F32),
            sds((batch, LANES, seq), _F32), sds((n, NSA_W), _BF16), sds((n, KV_W), _BF16),
            sds((n, KV_W), _BF16), sds((n, KV_W), _BF16), sds((n, KV_W), _BF16),
            sds((batch, NSA_KV_HEADS * VT_ROWS, seq), _BF16), sds((batch, NSA_KV_HEADS * VT_ROWS, seq), _BF16),
            sds((n, NSA_W), _F32)),
        grid=(n // tm,),
        in_specs=[row(D_MODEL), pl.BlockSpec((1, D_MODEL), lambda i: (0, 0)),
                  pl.BlockSpec((D_MODEL, _W_COLS), lambda i: (0, 0))],
        out_specs=(row(GDN_CONV_W), row(GDN_W), row(LANES), tr, row(NSA_W), row(KV_W), row(KV_W),
                   row(KV_W), row(KV_W), trv, trv, row(NSA_W)),
        compiler_params=pltpu.CompilerParams(dimension_semantics=("arbitrary",),
                                             vmem_limit_bytes=VMEM_LIMIT),
        name="in_proj",
    )(x2d, norm_w.reshape(1, D_MODEL).astype(_F32), w_cat)


_GDN_PAIR = 2 * CHUNK
_STACK = GDN_HEADS * CHUNK
_PREP_UNROLL = 4


def _segment_cumsum(g, axis):
    pos = lax.broadcasted_iota(jnp.int32, g.shape, axis) % CHUNK
    k = 1
    while k < CHUNK:
        g = g + jnp.where(pos >= k, pltpu.roll(g, k, axis), 0.0)
        k *= 2
    return g


def _stacked_unit_lower_inverses(lows, out):
    r = lax.broadcasted_iota(jnp.int32, (_STACK, _STACK), 0)
    c = lax.broadcasted_iota(jnp.int32, (_STACK, _STACK), 1)
    eye = (r == c).astype(_F32)
    ms = [-low for low in lows]
    xs = [eye + m for m in ms]
    ms = [_dot(m, m) for m in ms]
    yield
    span = 4
    while span < CHUNK:
        boths = [_dot(jnp.concatenate([x, m], axis=0), m) for x, m in zip(xs, ms)]
        xs = [x + both[:_STACK] for x, both in zip(xs, boths)]
        ms = [both[_STACK:] for both in boths]
        span *= 2
        yield
    out.extend(x + _dot(x, m) for x, m in zip(xs, ms))
    yield


def _gdn_kernel(x_ref, cw_ref, sm_ref, smt_ref, prow_ref, pcol_ref, gz_ref, gnw_ref, o_ref,
                xs_ref, q_ref, k_ref, v_ref, col_ref, grow_ref, state_ref,
                u_ref, w_ref, qd_ref, kd_ref, a_ref, egl_ref, *, tile, tiles_per_seq):
    t_idx = pl.program_id(0)
    slot_1 = t_idx % 2
    slot_2 = 1 - slot_1
    half = tile // 2

    @pl.when(t_idx == 0)
    def _():
        for ref in (q_ref, k_ref, v_ref, col_ref, grow_ref):
            ref[1] = jnp.zeros(ref.shape[1:], _F32)
        for ref in (u_ref, w_ref, qd_ref, kd_ref, a_ref, egl_ref):
            ref[0] = jnp.zeros(ref.shape[1:], _F32)

    @pl.when(t_idx % tiles_per_seq == 0)
    def _():
        xs_ref[0:8, :] = jnp.zeros((8, GDN_CONV_W), _F32)

    @pl.when(t_idx % tiles_per_seq != 0)
    def _():
        xs_ref[0:8, :] = xs_ref[tile:tile + 8, :]

    @pl.when((t_idx == 0) | ((t_idx - 2) % tiles_per_seq == 0))
    def _():
        state_ref[...] = jnp.zeros_like(state_ref)

    xs_ref[8:tile + 8, :] = x_ref[0]
    heads = range(GDN_HEADS)

    def stage1_half(pi):
        r0 = pl.multiple_of(pi * half, half)
        rows = pl.ds(r0, half)
        xe = xs_ref[pl.ds(r0, half + 8), :]
        y = jnp.zeros((half, GDN_CONV_W), _F32)
        for tap in range(CONV_K):
            off = 8 - (CONV_K - 1) + tap
            y = y + cw_ref[tap:tap + 1, :] * xe[off:off + half, :]
        y = _silu(y)
        for h in heads:
            lo = h * GDN_DK
            qh = y[:, lo:lo + GDN_DK]
            kh = y[:, GDN_QK_W + lo:GDN_QK_W + lo + GDN_DK]
            q_ref[slot_1, rows, lo:lo + GDN_DK] = (
                qh * lax.rsqrt(jnp.sum(qh * qh, axis=-1, keepdims=True) + EPS) * GDN_DK ** -0.5)
            k_ref[slot_1, rows, lo:lo + GDN_DK] = kh * lax.rsqrt(jnp.sum(kh * kh, axis=-1, keepdims=True) + EPS)
        v_ref[slot_1, rows, :] = y[:, 2 * GDN_QK_W:]
        sm = sm_ref[0, rows, :]
        lane = lax.broadcasted_iota(jnp.int32, sm.shape, 1)
        zc = sm + prow_ref[1:2, :]
        gcol = -jnp.exp(prow_ref[0:1, :]) * (jnp.maximum(zc, 0.0) + jnp.log1p(jnp.exp(-jnp.abs(zc))))
        col_ref[slot_1, rows, :] = jnp.where(lane < GDN_HEADS, _sigmoid(sm), _segment_cumsum(gcol, 0))
        zr = smt_ref[0, 0:8, rows] + pcol_ref[1]
        grow = -jnp.exp(pcol_ref[0]) * (jnp.maximum(zr, 0.0) + jnp.log1p(jnp.exp(-jnp.abs(zr))))
        grow_ref[slot_1, :, rows] = _segment_cumsum(grow, 1)

    r = lax.broadcasted_iota(jnp.int32, (_STACK, _STACK), 0)
    c = lax.broadcasted_iota(jnp.int32, (_STACK, _STACK), 1)
    same_head = (r // CHUNK) == (c // CHUNK)
    tril = same_head & (r >= c)
    strict = same_head & (r > c)

    def prep_gram(ci, odd):
        rows = pl.ds(pl.multiple_of(ci * CHUNK, CHUNK), CHUNK)
        slab = pl.ds(pl.multiple_of((ci // 2) * _GDN_PAIR, _GDN_PAIR), _GDN_PAIR)
        colv = col_ref[slot_2, rows, :]
        grow2 = grow_ref[slot_2, :, slab]

        def stack(ref, width):
            return jnp.concatenate([ref[slot_2, rows, h * width:(h + 1) * width] for h in heads], axis=0)

        q = stack(q_ref, GDN_DK)
        k = stack(k_ref, GDN_DK)
        v = stack(v_ref, GDN_DV)
        beta = jnp.concatenate([colv[:, h:h + 1] for h in heads], axis=0)
        gc = jnp.concatenate([colv[:, GDN_HEADS + h:GDN_HEADS + h + 1] for h in heads], axis=0)
        gr = jnp.concatenate([grow2[GDN_HEADS + h:GDN_HEADS + h + 1, odd * CHUNK:(odd + 1) * CHUNK]
                              for h in heads], axis=1)
        last = [colv[CHUNK - 1:CHUNK, GDN_HEADS + h:GDN_HEADS + h + 1] for h in heads]
        g_last = jnp.concatenate([jnp.broadcast_to(x, (CHUNK, 1)) for x in last], axis=0)
        decay = jnp.where(tril, jnp.exp(jnp.where(tril, gc - gr, 0.0)), 0.0)
        kb = k * beta
        eg = jnp.exp(gc)
        low = jnp.where(strict, _dot_nt(kb, k) * decay, 0.0)
        a_ref[slot_2, ci] = jnp.where(tril, _dot_nt(q, k) * decay, 0.0)
        kd_ref[slot_2, ci] = k * jnp.exp(g_last - gc)
        qd_ref[slot_2, ci] = q * eg
        egl_ref[slot_2, ci] = jnp.concatenate([jnp.broadcast_to(jnp.exp(x), (1, GDN_DV)) for x in last]
                                              + [jnp.zeros((8 - GDN_HEADS, GDN_DV), _F32)], axis=0)
        return low, jnp.concatenate([v * beta, kb * eg], axis=1)

    gnw = gnw_ref[...]

    def stage3_chunks(cis):
        states = [state_ref[h] for h in heads]
        for ci in cis:
            u, w, qd, kd = u_ref[slot_1, ci], w_ref[slot_1, ci], qd_ref[slot_1, ci], kd_ref[slot_1, ci]
            egl = egl_ref[slot_1, ci]
            rows = pl.ds(pl.multiple_of(ci * CHUNK, CHUNK), CHUNK)
            hrows = [slice(h * CHUNK, (h + 1) * CHUNK) for h in heads]
            wqs = [_dot(jnp.concatenate([w[rs], qd[rs]], axis=0), states[h]) for h, rs in zip(heads, hrows)]
            yield
            v_news = [u[rs] - wq[:CHUNK] for rs, wq in zip(hrows, wqs)]
            for h in heads:
                states[h] = states[h] * egl[h:h + 1, :] + _dot_tn(kd[hrows[h]], v_news[h])
            o = (jnp.concatenate([wq[CHUNK:] for wq in wqs], axis=0)
                 + _dot(a_ref[slot_1, ci], jnp.concatenate(v_news, axis=0)))
            for h in heads:
                lanes = slice(h * GDN_DV, (h + 1) * GDN_DV)
                oh = o[hrows[h]]
                oh = oh * lax.rsqrt(jnp.mean(oh * oh, axis=-1, keepdims=True) + EPS) * gnw
                o_ref[0, rows, lanes] = (oh * _silu(gz_ref[0, rows, lanes])).astype(_BF16)
            yield
        for h in heads:
            state_ref[h] = states[h]

    def stage2_chunks(cis):
        lows, rhss = zip(*[prep_gram(ci, j % 2) for j, ci in enumerate(cis)])
        yield
        tinvs = []
        yield from _stacked_unit_lower_inverses(lows, tinvs)
        sols = [_dot(tinv, rhs) for tinv, rhs in zip(tinvs, rhss)]
        for ci, sol in zip(cis, sols):
            u_ref[slot_2, ci] = sol[:, :GDN_DV]
            w_ref[slot_2, ci] = sol[:, GDN_DV:]

    def pipeline_body(pi, carry):
        cis = [pi * _PREP_UNROLL + j for j in range(_PREP_UNROLL)]
        streams = [stage3_chunks(cis), stage2_chunks(cis)]
        first_round = True
        while streams:
            for stream in list(streams):
                if next(stream, StopIteration) is StopIteration:
                    streams.remove(stream)
            if first_round:
                stage1_half(pi)
                first_round = False
        return carry

    assert tile // CHUNK == 2 * _PREP_UNROLL
    lax.fori_loop(0, 2, pipeline_body, 0)


def _gdn(conv_in, conv_w, sm, smt, a_log, dt_bias, gz, gdn_norm_w, tile):
    batch, seq, _ = conv_in.shape
    prow = jnp.zeros((2, LANES), _F32)
    prow = prow.at[0, GDN_HEADS:2 * GDN_HEADS].set(a_log.astype(_F32))
    prow = prow.at[1, GDN_HEADS:2 * GDN_HEADS].set(dt_bias.astype(_F32))
    pcol = jnp.broadcast_to(prow[:, :8, None], (2, 8, tile // 2))
    n_chunks = tile // CHUNK
    assert _PREP_UNROLL % 2 == 0
    tiles_per_seq = seq // tile
    n_tiles = batch * tiles_per_seq
    kern = functools.partial(_gdn_kernel, tile=tile, tiles_per_seq=tiles_per_seq)

    def stage1_tile(i):
        i = jnp.minimum(i, n_tiles - 1)
        return i // tiles_per_seq, i % tiles_per_seq

    def stage3_tile(i):
        i = jnp.maximum(i - 2, 0)
        return i // tiles_per_seq, i % tiles_per_seq

    rows_of = lambda pick: (lambda i: (*pick(i), 0))
    const = lambda i: (0, 0)
    return pl.pallas_call(
        kern,
        out_shape=jax.ShapeDtypeStruct((batch, seq, GDN_W), _BF16),
        grid=(n_tiles + 2,),
        in_specs=[
            pl.BlockSpec((1, tile, GDN_CONV_W), rows_of(stage1_tile)),
            pl.BlockSpec((CONV_K, GDN_CONV_W), const),
            pl.BlockSpec((1, tile, LANES), rows_of(stage1_tile)),
            pl.BlockSpec((1, LANES, tile), lambda i: (stage1_tile(i)[0], 0, stage1_tile(i)[1])),
            pl.BlockSpec((2, LANES), const),
            pl.BlockSpec((2, 8, tile // 2), lambda i: (0, 0, 0)),
            pl.BlockSpec((1, tile, GDN_W), rows_of(stage3_tile)),
            pl.BlockSpec((1, GDN_DV), const),
        ],
        out_specs=pl.BlockSpec((1, tile, GDN_W), rows_of(stage3_tile)),
        scratch_shapes=[
            pltpu.VMEM((tile + 8, GDN_CONV_W), _F32),
            pltpu.VMEM((2, tile, GDN_QK_W), _F32),
            pltpu.VMEM((2, tile, GDN_QK_W), _F32),
            pltpu.VMEM((2, tile, GDN_W), _F32),
            pltpu.VMEM((2, tile, LANES), _F32),
            pltpu.VMEM((2, 8, tile), _F32),
            pltpu.VMEM((GDN_HEADS, GDN_DK, GDN_DV), _F32),
            pltpu.VMEM((2, n_chunks, _STACK, GDN_DV), _F32),
            pltpu.VMEM((2, n_chunks, _STACK, GDN_DV), _F32),
            pltpu.VMEM((2, n_chunks, _STACK, GDN_DK), _F32),
            pltpu.VMEM((2, n_chunks, _STACK, GDN_DK), _F32),
            pltpu.VMEM((2, n_chunks, _STACK, _STACK), _F32),
            pltpu.VMEM((2, n_chunks, 8, GDN_DV), _F32),
        ],
        compiler_params=pltpu.CompilerParams(dimension_semantics=("arbitrary",),
                                             vmem_limit_bytes=VMEM_LIMIT),
        name="gdn",
    )(conv_in, conv_w.astype(_F32), sm, smt, prow, pcol, gz, gdn_norm_w.reshape(1, GDN_DV).astype(_F32))


_GROUP_W = CMP_STRIDE * KV_W
_HID2 = NSA_KV_HEADS * CMP_HIDDEN


def _compress_weights(pe, w1, w2):
    eye = jnp.eye(NSA_KV_HEADS, dtype=_F32)
    halves = []
    pes = []
    for part in range(CMP_LEN // CMP_STRIDE):
        w1p = w1[part * CMP_STRIDE:(part + 1) * CMP_STRIDE].astype(_F32)
        big = jnp.einsum('ldf,hg->lhdgf', w1p, eye).reshape(_GROUP_W, _HID2)
        halves.append(big.astype(_BF16))
        pep = pe[part * CMP_STRIDE:(part + 1) * CMP_STRIDE].astype(_F32)
        pes.append(jnp.broadcast_to(pep[:, None, :], (CMP_STRIDE, NSA_KV_HEADS, NSA_DH)).reshape(1, _GROUP_W))
    w2big = jnp.einsum('fd,hg->hfgd', w2.astype(_F32), eye).reshape(_HID2, KV_W).astype(_BF16)
    pe2 = jnp.broadcast_to(jnp.stack(pes, 0), (2, 8, _GROUP_W)).astype(_BF16)
    return halves[0], halves[1], w2big, pe2


def _compress_kernel(kc_ref, vc_ref, kwa_ref, kwb_ref, kw2_ref, kpe_ref, vwa_ref, vwb_ref, vw2_ref, vpe_ref,
                     ko_ref, vo_ref):
    def mlp(g_ref, wa_ref, wb_ref, w2_ref, pe_ref):
        g = g_ref[0]
        ncp = g.shape[0]
        p0 = _dot(g, wa_ref[...])
        p1 = _dot(g, wb_ref[...])
        const = _dot(pe_ref[0], wa_ref[...]) + _dot(pe_ref[1], wb_ref[...])
        pre = p0 + pltpu.roll(p1, ncp - 1, 0) + const[0:1, :]
        return _dot(_silu(pre).astype(_BF16), w2_ref[...])

    ko_ref[0] = mlp(kc_ref, kwa_ref, kwb_ref, kw2_ref, kpe_ref).astype(_BF16)
    vo_ref[0] = mlp(vc_ref, vwa_ref, vwb_ref, vw2_ref, vpe_ref).T.astype(_BF16)


def _compress(kc, vc, kweights, vweights):
    batch, ncp, _ = kc.shape
    full = lambda a: pl.BlockSpec(a.shape, lambda b: (0,) * a.ndim)
    per_b = pl.BlockSpec((1, ncp, _GROUP_W), lambda b: (b, 0, 0))
    return pl.pallas_call(
        _compress_kernel,
        out_shape=(jax.ShapeDtypeStruct((batch, ncp, KV_W), _BF16),
                   jax.ShapeDtypeStruct((batch, KV_W, ncp), _BF16)),
        grid=(batch,),
        in_specs=[per_b, per_b] + [full(a) for a in kweights] + [full(a) for a in vweights],
        out_specs=(pl.BlockSpec((1, ncp, KV_W), lambda b: (b, 0, 0)),
                   pl.BlockSpec((1, KV_W, ncp), lambda b: (b, 0, 0))),
        compiler_params=pltpu.CompilerParams(dimension_semantics=("arbitrary",),
                                             vmem_limit_bytes=VMEM_LIMIT),
        name="compress",
    )(kc, vc, *kweights, *vweights)


def _overlap_t(n_sel, ncp, n_cmp):
    cs = CMP_STRIDE * np.arange(ncp)
    ss = SEL_BLOCK * np.arange(n_sel)
    ov = ((cs[None, :] < ss[:, None] + SEL_BLOCK) & (cs[None, :] + CMP_LEN > ss[:, None])
          & (np.arange(ncp)[None, :] < n_cmp))
    return jnp.asarray(ov.astype(np.float32), dtype=_BF16)


def _key_onehots(n_tiles, with_blocks):
    key = np.arange(n_tiles * TQ)[:, None]
    col = np.arange(LANES)[None, :]
    ek = np.broadcast_to((col == 8) | (col == 9), (n_tiles * TQ, LANES))
    if with_blocks:
        ek = ek | (col == key // SEL_BLOCK)
    return jnp.asarray(ek.astype(np.float32), dtype=_BF16)


def _nsa_kernel(q_ref, ks_ref, kw_ref, vst_ref, vwt_ref, kc_ref, vct_ref, tab_ref, far_ref, ovt_ref, eks_ref,
                ekw_ref, smt_ref, nz_ref, o_ref, cb_ref, sc_ref, cnt_ref, mb_ref, ss_ref, st_ref, ps_ref, ocw_ref):
    a_of = [TILE_PAIR * pl.program_id(1) + t for t in range(TILE_PAIR)]
    ncp = kc_ref.shape[1]
    n_sel = ovt_ref.shape[0]
    heads = range(NSA_KV_HEADS)
    tiles = range(TILE_PAIR)
    chains = [(t, h) for t in tiles for h in heads]
    cid = lambda t, h: t * NSA_KV_HEADS + h
    drows = [slice(h * NSA_DH, (h + 1) * NSA_DH) for h in heads]
    vrows = [slice(h * VT_ROWS, (h + 1) * VT_ROWS) for h in heads]
    qcols = [slice(h * GQ_W, (h + 1) * GQ_W) for h in heads]
    toks = [slice(t * TQ, (t + 1) * TQ) for t in tiles]

    qrow = lax.broadcasted_iota(jnp.int32, (KV_W, QL), 0)

    def transposed_queries(t, h):
        qt = (q_ref[0, toks[t], qcols[h]].astype(_F32) * (NSA_DH ** -0.5 * LOG2E)).T
        qt4 = jnp.concatenate([qt[g * NSA_DH:(g + 1) * NSA_DH, :] for g in range(NSA_GROUP)], axis=1)
        return jnp.where(qrow // NSA_DH == h, jnp.concatenate([qt4] * NSA_KV_HEADS, axis=0), 0.0).astype(_BF16)

    qt_b = {c: transposed_queries(*c) for c in chains}
    far_rows = [far_ref[h] for h in heads]
    zero_rows = jnp.zeros((LANES - 16, QL), _BF16)

    def augmented_queries(c, mask_rows):
        extra = jnp.concatenate([mask_rows, far_rows[c[1]]], axis=0).astype(_BF16)
        return jnp.concatenate([qt_b[c], extra, zero_rows], axis=0)

    def augmented_keys(k_ref, ek, key0, n_tiles):
        keys = pl.ds(pl.multiple_of(key0, TQ), n_tiles * TQ)
        return jnp.concatenate([k_ref[0, keys, :], ek], axis=1)

    def gate(c, branch):
        t, h = c
        first = 8 + 8 * branch + NSA_GROUP * h
        return _sigmoid(jnp.concatenate([smt_ref[0, first + g:first + g + 1, toks[t]]
                                         for g in range(NSA_GROUP)], axis=1))

    def normalized(acc):
        return acc[0:NSA_DH, :] * (1.0 / acc[NSA_DH:NSA_DH + 1, :])

    win_t0 = [jnp.maximum(a - (WIN_TILES - 1), 0) for a in a_of]
    win_keys = [pl.ds(pl.multiple_of(w0 * TQ, TQ), WIN_TILES * TQ) for w0 in win_t0]
    win_start = [pl.multiple_of(jnp.maximum(8 * a - CMP_NEAR_BACK, 0), 8) for a in a_of]
    tab_start = [pl.multiple_of(ws - (8 * a - CMP_NEAR_BACK), 8) for ws, a in zip(win_start, a_of)]

    def compressed_scores(n_rows):
        crow = lax.broadcasted_iota(jnp.int32, (n_rows, QL), 0)
        for t, h in chains:
            c = cid(t, h)
            cb_ref[c, 0:n_rows, :] = jnp.where(crow < win_start[t], far_rows[h][2:3, :], NEG)
            cb_ref[c, pl.ds(win_start[t], CMP_NEAR), :] = tab_ref[h, TAB_CMP, pl.ds(tab_start[t], CMP_NEAR), :]
        k_cmp = kc_ref[0, 0:n_rows, :]
        return {c: _dot(k_cmp, qt_b[c]) + cb_ref[cid(*c), 0:n_rows, :] for c in chains}

    def compressed_branch(s_c, acc_w):
        n_rows = s_c[chains[0]].shape[0]
        m_c = {c: jnp.max(s_c[c], axis=0, keepdims=True) for c in chains}
        e_c = {c: jnp.exp2(s_c[c] - m_c[c]) for c in chains}
        l_c = {c: jnp.sum(e_c[c], axis=0, keepdims=True) for c in chains}
        p_c = {c: e_c[c] * jnp.where(m_c[c] > 0.5 * NEG, 1.0 / l_c[c], 0.0) for c in chains}
        o_c = {c: _dot(vct_ref[0, drows[c[1]], 0:n_rows], p_c[c].astype(_BF16)) for c in chains}
        for c in chains:
            ocw_ref[cid(*c)] = gate(c, 0) * o_c[c] + gate(c, 2) * normalized(acc_w[c])
        ovt = ovt_ref[:, 0:n_rows]
        jrow = lax.broadcasted_iota(jnp.int32, (n_sel, TQ), 0)
        for c in chains:
            cur = (a_of[c[0]] * TQ + lax.broadcasted_iota(jnp.int32, (n_sel, TQ), 1)) // SEL_BLOCK
            forced = (jrow == 0) | (jrow == cur) | (jrow == cur - 1)
            psum = p_c[c][:, 0:TQ]
            for g in range(1, NSA_GROUP):
                psum = psum + p_c[c][:, g * TQ:(g + 1) * TQ]
            hi = psum.astype(_BF16)
            r1 = psum - hi.astype(_F32)
            mid = r1.astype(_BF16)
            lo = (r1 - mid.astype(_F32)).astype(_BF16)
            imp = _dot(ovt, hi) + _dot(ovt, mid) + _dot(ovt, lo)
            sc_ref[cid(*c)] = jnp.where(jrow <= cur, imp + jnp.where(forced, FORCE_BONUS, 0.0), NEG)

    def window_entries(t):
        entries = []
        for i in range(WIN_TILES):
            d = a_of[t] - (win_t0[t] + i)
            entries.append(jnp.where(d < 0, TAB_MASKED, jnp.where(d == WIN_TILES - 1, TAB_WIN_EDGE, d)))
        return entries

    win_entries = [window_entries(t) for t in tiles]

    def front_stage(n_rows):
        ekw = ekw_ref[...]
        k_aug_w = [augmented_keys(kw_ref, ekw, win_t0[t] * TQ, WIN_TILES) for t in tiles]
        no_mask = jnp.zeros((8, QL), _F32)
        s_w = {c: _dot(k_aug_w[c[0]], augmented_queries(c, no_mask)) for c in chains}
        s_c = compressed_scores(n_rows)
        s_wb = {c: jnp.concatenate([s_w[c][i * TQ:(i + 1) * TQ, :] + tab_ref[c[1], win_entries[c[0]][i]]
                                    for i in range(WIN_TILES)], axis=0) for c in chains}
        m_w = {c: jnp.max(s_wb[c], axis=0, keepdims=True) for c in chains}
        acc_w = {c: _dot(vwt_ref[0, vrows[c[1]], win_keys[c[0]]], jnp.exp2((s_wb[c] - m_w[c]).astype(_BF16)))
                 for c in chains}
        compressed_branch(s_c, acc_w)

    half_rows = ncp // 2
    needs_all = 8 * a_of[-1] + 8 > half_rows

    @pl.when(jnp.logical_not(needs_all))
    def _():
        front_stage(half_rows)

    @pl.when(needs_all)
    def _():
        front_stage(ncp)

    cnt_ref[...] = jnp.zeros(cnt_ref.shape, _F32)
    sub = lax.broadcasted_iota(jnp.int32, (8, TQ), 0)
    last_valid = (TQ // SEL_BLOCK) * (a_of[-1] + 1) - 1
    for chunk in range(n_sel // RANK_CHUNK):
        for row_chunk in range(n_sel // RANK_ROWS):
            row_groups = range(row_chunk * RANK_ROWS // 8, (row_chunk + 1) * RANK_ROWS // 8)

            @pl.when((chunk * RANK_CHUNK <= last_valid) & (row_chunk * RANK_ROWS <= last_valid))
            def _():
                for c in range(len(chains)):
                    rows = {r: sc_ref[c, 8 * r:8 * r + 8, :] for r in row_groups}
                    cnts = {r: cnt_ref[c, 8 * r:8 * r + 8, :] for r in row_groups}
                    for jp in range(chunk * RANK_CHUNK, (chunk + 1) * RANK_CHUNK):
                        srow = sc_ref[c, jp:jp + 1, :]
                        for r in row_groups:
                            if 8 * r > jp:
                                inc = jnp.where(srow >= rows[r], 1.0, 0.0)
                            elif 8 * r + 7 < jp:
                                inc = jnp.where(srow > rows[r], 1.0, 0.0)
                            else:
                                tie = jnp.where(sub + 8 * r > jp, 1.0, 0.0)
                                inc = jnp.where(srow > rows[r], 1.0, jnp.where(srow >= rows[r], tie, 0.0))
                            cnts[r] = cnts[r] + inc
                    for r in row_groups:
                        cnt_ref[c, 8 * r:8 * r + 8, :] = cnts[r]

    for c in range(len(chains)):
        mb = jnp.where(cnt_ref[c] < SEL_TOP_N, 0.0, NEG)
        mb_ref[c] = jnp.concatenate([mb] * NSA_GROUP, axis=1)

    eks = eks_ref[...]
    blocks_per_group = GROUP_TILES * (TQ // SEL_BLOCK)
    group_keys = GROUP_TILES * TQ
    last_group = [a // GROUP_TILES for a in a_of]

    def sel_scores(t, gi):
        k_aug = augmented_keys(ks_ref, eks, gi * group_keys, GROUP_TILES)
        mask_rows = pl.ds(pl.multiple_of(gi * blocks_per_group, 8), blocks_per_group)
        return [_dot(k_aug, augmented_queries((t, h), mb_ref[cid(t, h), mask_rows, :])) for h in heads]

    def sel_values(h, gi):
        return vst_ref[0, vrows[h], pl.ds(pl.multiple_of(gi * group_keys, group_keys), group_keys)]

    def sel_step(gi, carry, t, near, last):
        s_next = None if last else sel_scores(t, gi + 1)
        pv_prev = [_dot(sel_values(h, jnp.maximum(gi - 1, 0)), ps_ref[cid(t, h)]) for h in heads]
        entries = []
        if near:
            for i in range(GROUP_TILES):
                d = a_of[t] - (gi * GROUP_TILES + i)
                entries.append(jnp.where(d < 0, TAB_MASKED, jnp.where(d >= N_NEAR, TAB_ZERO, d)))
        new_carry = []
        for h in heads:
            c = cid(t, h)
            m, alpha_prev, acc = carry[h]
            s = ss_ref[c]
            if near:
                s = jnp.concatenate([s[i * TQ:(i + 1) * TQ, :] + tab_ref[h, entries[i]]
                                     for i in range(GROUP_TILES)], axis=0)
                st_ref[c] = s
            m_new = jnp.maximum(m, jnp.max(s, axis=0, keepdims=True))
            alpha = jnp.exp2(m - m_new)
            acc = alpha_prev * acc + pv_prev[h]
            s = st_ref[c] if near else ss_ref[c]
            ps_ref[c] = jnp.exp2((s - m_new).astype(_BF16))
            new_carry.append((m_new, alpha, acc))
        if not last:
            for h in heads:
                ss_ref[cid(t, h)] = s_next[h]
        return tuple(new_carry)

    ps_ref[...] = jnp.zeros(ps_ref.shape, _BF16)
    carries = []
    for t in tiles:
        s_first = sel_scores(t, 0)
        for h in heads:
            ss_ref[cid(t, h)] = s_first[h]
        carry = tuple((jnp.full((1, QL), NEG, _F32), jnp.ones((1, QL), _F32), jnp.zeros((VT_ROWS, QL), _F32))
                      for _ in heads)
        n_far = jnp.maximum(last_group[t] - N_NEAR // GROUP_TILES, 0)
        carry = lax.fori_loop(0, n_far, functools.partial(sel_step, t=t, near=False, last=False), carry)
        carry = lax.fori_loop(n_far, last_group[t], functools.partial(sel_step, t=t, near=True, last=False), carry)
        carries.append(carry)

    carries = [sel_step(last_group[t], carries[t], t=t, near=True, last=True) for t in tiles]

    acc_s = {}
    for c in chains:
        t, h = c
        _, alpha_s, acc = carries[t][h]
        acc_s[c] = alpha_s * acc + _dot(sel_values(h, last_group[t]), ps_ref[cid(t, h)])
    for c in chains:
        t, h = c
        o_t = ocw_ref[cid(t, h)] + gate(c, 1) * normalized(acc_s[c])
        stacked = jnp.concatenate([o_t[:, g * TQ:(g + 1) * TQ] for g in range(NSA_GROUP)], axis=0)
        o_ref[0, toks[t], qcols[h]] = (stacked.T * _silu(nz_ref[0, toks[t], qcols[h]])).astype(_BF16)


def _nsa(nq, ks, kw, vst, vwt, kcmp, vcmpt, tab, far, ovt, smt, nz):
    batch, seq, _ = nq.shape
    ncp = kcmp.shape[1]
    n_sel = seq // SEL_BLOCK
    per_b = lambda shape: pl.BlockSpec((1,) + shape, lambda b, a: (b, 0, 0))
    const = lambda shape: pl.BlockSpec(shape, lambda b, a: (0,) * len(shape), pipeline_mode=pl.Buffered(1))
    step_tokens = TILE_PAIR * TQ
    tile = lambda width: pl.BlockSpec((1, step_tokens, width), lambda b, a: (b, a, 0))
    vt_shape = (NSA_KV_HEADS * VT_ROWS, seq)
    per_head = lambda shape, dtype: pltpu.VMEM((TILE_PAIR * NSA_KV_HEADS,) + shape, dtype)
    return pl.pallas_call(
        _nsa_kernel,
        out_shape=jax.ShapeDtypeStruct((batch, seq, NSA_W), _BF16),
        grid=(batch, seq // step_tokens),
        in_specs=[
            tile(NSA_W),
            per_b((seq, KV_W)), per_b((seq, KV_W)), per_b(vt_shape), per_b(vt_shape),
            per_b((ncp, KV_W)), per_b((KV_W, ncp)),
            const((NSA_KV_HEADS, N_TAB, TQ, QL)), const((NSA_KV_HEADS, 8, QL)),
            const((n_sel, ncp)), const((GROUP_TILES * TQ, LANES)), const((WIN_TILES * TQ, LANES)),
            pl.BlockSpec((1, LANES, step_tokens), lambda b, a: (b, 0, a)),
            tile(NSA_W),
        ],
        out_specs=tile(NSA_W),
        scratch_shapes=[per_head((ncp, QL), _F32), per_head((n_sel, TQ), _F32),
                        per_head((n_sel, TQ), _F32), per_head((n_sel, QL), _F32),
                        per_head((GROUP_TILES * TQ, QL), _F32), per_head((GROUP_TILES * TQ, QL), _F32),
                        per_head((GROUP_TILES * TQ, QL), _BF16), per_head((NSA_DH, QL), _F32)],
        compiler_params=pltpu.CompilerParams(dimension_semantics=("arbitrary", "arbitrary"),
                                             vmem_limit_bytes=VMEM_LIMIT),
        name="nsa",
    )(nq, ks, kw, vst, vwt, kcmp, vcmpt, tab, far, ovt, _key_onehots(GROUP_TILES, True),
      _key_onehots(WIN_TILES, False), smt, nz)


def _out_proj_kernel(og_ref, on_ref, x_ref, w_ref, fw_ref, o_ref):
    h = x_ref[...] + _dot(og_ref[...], w_ref[0:GDN_W, :]) + _dot(on_ref[...], w_ref[GDN_W:, :])
    o_ref[...] = h * lax.rsqrt(jnp.mean(h * h, axis=-1, keepdims=True) + EPS) * fw_ref[...]


def _out_proj(o_g, o_n, x2d, w_out, final_norm_w, tm):
    n = x2d.shape[0]
    row = lambda w: pl.BlockSpec((tm, w), lambda i: (i, 0))
    return pl.pallas_call(
        _out_proj_kernel,
        out_shape=jax.ShapeDtypeStruct((n, D_MODEL), _F32),
        grid=(n // tm,),
        in_specs=[row(GDN_W), row(NSA_W), row(D_MODEL),
                  pl.BlockSpec((GDN_W + NSA_W, D_MODEL), lambda i: (0, 0)),
                  pl.BlockSpec((1, D_MODEL), lambda i: (0, 0))],
        out_specs=row(D_MODEL),
        compiler_params=pltpu.CompilerParams(dimension_semantics=("arbitrary",),
                                             vmem_limit_bytes=VMEM_LIMIT),
        name="out_proj",
    )(o_g, o_n, x2d, w_out.astype(_BF16), final_norm_w.reshape(1, D_MODEL).astype(_F32))


def kernel(x, norm_w, w_in, conv_w, a_log, dt_bias, gdn_norm_w, cmp_pe_k, cmp_pe_v, cmp_k_w1, cmp_k_w2,
           cmp_v_w1, cmp_v_w2, w_out, rel_bias, final_norm_w):
    batch, seq, _ = x.shape
    assert w_in.shape[0] == 1, "single-layer problem"
    assert seq % (8 * TQ) == 0
    n = batch * seq
    ncp = seq // CMP_STRIDE
    n_cmp = (seq - CMP_LEN) // CMP_STRIDE + 1
    tm = min(512, seq)

    x2d = x.reshape(n, D_MODEL)
    tab, far = _bias_tables(rel_bias)
    (conv_in, gz, sm, smt, nq, kc, vc, ks, kw, vst, vwt, nz) = _in_proj(
        x2d, norm_w[0], _regroup_w_in(w_in[0]), batch, seq, tm)

    o_g = _gdn(conv_in.reshape(batch, seq, GDN_CONV_W), conv_w[0], sm.reshape(batch, seq, LANES), smt,
               a_log[0], dt_bias[0], gz.reshape(batch, seq, GDN_W), gdn_norm_w[0], tm)

    kcmp, vcmpt = _compress(kc.reshape(batch, ncp, _GROUP_W), vc.reshape(batch, ncp, _GROUP_W),
                            _compress_weights(cmp_pe_k[0], cmp_k_w1[0], cmp_k_w2[0]),
                            _compress_weights(cmp_pe_v[0], cmp_v_w1[0], cmp_v_w2[0]))

    o_n = _nsa(nq.reshape(batch, seq, NSA_W), ks.reshape(batch, seq, KV_W), kw.reshape(batch, seq, KV_W),
               vst, vwt, kcmp, vcmpt, tab, far, _overlap_t(seq // SEL_BLOCK, ncp, n_cmp), smt,
               nz.reshape(batch, seq, NSA_W))

    out = _out_proj(o_g.reshape(n, GDN_W), o_n.reshape(n, NSA_W), x2d, w_out[0], final_norm_w, 2 * tm)
    return out.reshape(batch, seq, D_MODEL)
```

```python
import functools
import math

import jax
import jax.numpy as jnp
import numpy as np
from jax import lax
from jax.experimental import pallas as pl
from jax.experimental.pallas import tpu as pltpu

D_MODEL = 1024
GDN_HEADS = 4
GDN_DK = 128
GDN_DV = 128
CONV_K = 4
CHUNK = 64
NSA_HEADS = 8
NSA_KV_HEADS = 2
NSA_GROUP = NSA_HEADS // NSA_KV_HEADS
NSA_DH = 64
CMP_LEN = 32
CMP_STRIDE = 16
CMP_HIDDEN = 128
SEL_BLOCK = 64
SEL_TOP_N = 16
WINDOW = 512
FORCE_BONUS = 1e4
N_BUCKETS = 32
MAX_DISTANCE = 1024
EPS = 1e-6
NEG = -1e30

GDN_QK_W = GDN_HEADS * GDN_DK
GDN_W = GDN_HEADS * GDN_DV
GDN_CONV_W = 2 * GDN_QK_W + GDN_W
NSA_W = NSA_HEADS * NSA_DH
KV_W = NSA_KV_HEADS * NSA_DH
PROJ_SIZES = (GDN_QK_W, GDN_QK_W, GDN_W, GDN_W, GDN_HEADS, GDN_HEADS,
              NSA_W, KV_W, KV_W, KV_W, KV_W, KV_W, KV_W, 3 * NSA_HEADS, NSA_W)

LANES = 128
TQ = 128
QL = NSA_GROUP * TQ
GQ_W = NSA_GROUP * NSA_DH
N_NEAR = 8
TAB_ZERO = N_NEAR
TAB_MASKED = N_NEAR + 1
TAB_WIN_EDGE = N_NEAR + 2
TAB_CMP = N_NEAR + 3
N_TAB = N_NEAR + 4
GROUP_TILES = 4
WIN_TILES = WINDOW // TQ + 1
TILE_PAIR = 2
RANK_CHUNK = 16
RANK_ROWS = 32
VT_ROWS = 80
LOG2E = math.log2(math.e)
CMP_NEAR = 64
CMP_NEAR_BACK = 56
VMEM_LIMIT = 56 * 1024 * 1024

_F32 = jnp.float32
_BF16 = jnp.bfloat16


def _dot(a, b):
    return jnp.dot(a, b, preferred_element_type=_F32)


def _dot_nt(a, b):
    return lax.dot_general(a, b, (((1,), (1,)), ((), ())), preferred_element_type=_F32)


def _dot_tn(a, b):
    return lax.dot_general(a, b, (((0,), (0,)), ((), ())), preferred_element_type=_F32)


def _sigmoid(x):
    return 0.5 * jnp.tanh(0.5 * x) + 0.5


def _silu(x):
    return x * _sigmoid(x)


def _bias_tab_kernel(rel_ref, o_ref, far_ref):
    hk = pl.program_id(0)
    e = pl.program_id(1)
    is_cmp = e == TAB_CMP
    is_edge = e == TAB_WIN_EDGE
    base = jnp.where(e < N_NEAR, e * TQ,
                     jnp.where(is_edge, WINDOW, CMP_STRIDE * CMP_NEAR_BACK - (CMP_LEN - 1)))
    row_scale = jnp.where(is_cmp, CMP_STRIDE, 1)
    upper = jnp.where(is_edge, WINDOW, 1 << 30)
    row = lax.broadcasted_iota(jnp.int32, (TQ, TQ), 0)
    tok = lax.broadcasted_iota(jnp.int32, (TQ, TQ), 1)
    dist = base + tok - row_scale * row
    n = jnp.maximum(dist, 0)
    max_exact = N_BUCKETS // 2
    large = max_exact + (jnp.log(jnp.maximum(n, max_exact).astype(_F32) / max_exact)
                         / math.log(MAX_DISTANCE / max_exact) * (N_BUCKETS - max_exact)).astype(jnp.int32)
    bucket = jnp.where(n < max_exact, n, jnp.minimum(large, N_BUCKETS - 1))
    ok = (dist >= 0) & (dist < upper)
    for g in range(NSA_GROUP):
        head = hk * NSA_GROUP + g
        lanes = slice(g * TQ, (g + 1) * TQ)
        far = jnp.full((8, TQ), rel_ref[N_BUCKETS - 1, head] * LOG2E, _F32)
        far_hi = far.astype(_BF16).astype(_F32)
        far_lo = (far - far_hi).astype(_BF16).astype(_F32)
        val = jnp.zeros((TQ, TQ), _F32)
        for b in range(N_BUCKETS):
            val = jnp.where(bucket == b, rel_ref[b, head], val)
        val = val * LOG2E - jnp.where(is_cmp, 0.0, far_hi[0:1, :] + far_lo[0:1, :])
        val = jnp.where(ok, val, NEG)
        val = jnp.where(e == TAB_ZERO, 0.0, jnp.where(e == TAB_MASKED, NEG, val))
        o_ref[0, 0, :, lanes] = val

        sub = lax.broadcasted_iota(jnp.int32, (8, TQ), 0)
        far_ref[0, :, lanes] = jnp.where(sub == 0, far_hi, jnp.where(sub == 1, far_lo,
                                                                     jnp.where(sub == 2, far, 0.0)))


def _bias_tables(rel_bias):
    return pl.pallas_call(
        _bias_tab_kernel,
        out_shape=(jax.ShapeDtypeStruct((NSA_KV_HEADS, N_TAB, TQ, QL), _F32),
                   jax.ShapeDtypeStruct((NSA_KV_HEADS, 8, QL), _F32)),
        grid=(NSA_KV_HEADS, N_TAB),
        in_specs=[pl.BlockSpec(memory_space=pltpu.SMEM)],
        out_specs=(pl.BlockSpec((1, 1, TQ, QL), lambda h, e: (h, e, 0, 0)),
                   pl.BlockSpec((1, 8, QL), lambda h, e: (h, 0, 0))),
        name="bias_tables",
    )(rel_bias.astype(_F32))


_C_CONV = (0, GDN_CONV_W)
_C_GZ = (_C_CONV[1], _C_CONV[1] + GDN_W)
_C_NQ = (_C_GZ[1], _C_GZ[1] + NSA_W)
_C_NZ = (_C_NQ[1], _C_NQ[1] + NSA_W)
_C_SMALL_KC = (_C_NZ[1], _C_NZ[1] + 2 * LANES)
_C_VC_KS = (_C_SMALL_KC[1], _C_SMALL_KC[1] + 2 * LANES)
_C_KW_VS = (_C_VC_KS[1], _C_VC_KS[1] + 2 * LANES)
_C_VW = (_C_KW_VS[1], _C_KW_VS[1] + LANES)
_W_COLS = _C_VW[1]
assert KV_W == LANES


def _regroup_w_in(w):
    gq, gk, gv, gz, gb, ga, nq, kc, vc, ks, vs, kw, vw, ng, nz = jnp.split(
        w, [int(p) for p in np.cumsum(PROJ_SIZES)[:-1]], axis=-1)
    small = jnp.concatenate([gb, ga, ng], axis=-1)
    small = jnp.pad(small, ((0, 0), (0, LANES - small.shape[-1])))
    return jnp.concatenate([gq, gk, gv, gz, nq, nz, small, kc, vc, ks, kw, vs, vw], axis=-1).astype(_BF16)


def _in_proj_kernel(x_ref, nw_ref, w_ref, conv_ref, gz_ref, sm_ref, smt_ref, nq_ref, kc_ref, vc_ref,
                    ks_ref, kw_ref, vst_ref, vwt_ref, nz_ref):
    x = x_ref[...]
    u = x * lax.rsqrt(jnp.mean(x * x, axis=-1, keepdims=True) + EPS) * nw_ref[...]
    u = u.astype(_BF16)

    def proj(c):
        return _dot(u, w_ref[:, c[0]:c[1]])

    def proj_pair(c):
        both = proj(c)
        return both[:, :LANES], both[:, LANES:]

    conv_ref[...] = proj(_C_CONV)
    gz_ref[...] = proj(_C_GZ)
    nq_ref[...] = proj(_C_NQ).astype(_BF16)
    nz_ref[...] = proj(_C_NZ)
    sm, kc = proj_pair(_C_SMALL_KC)
    vc, ks = proj_pair(_C_VC_KS)
    kw, vs = proj_pair(_C_KW_VS)
    sm_ref[...] = sm
    smt_ref[0] = sm.T
    kc_ref[...] = kc.astype(_BF16)
    vc_ref[...] = vc.astype(_BF16)
    ks_ref[...] = ks.astype(_BF16)
    kw_ref[...] = kw.astype(_BF16)
    tm = x.shape[0]
    ones_pad = (lax.broadcasted_iota(jnp.int32, (VT_ROWS - NSA_DH, tm), 0) == 0).astype(_F32)

    def values_t(v):
        vt = v.T
        parts = []
        for h in range(NSA_KV_HEADS):
            parts += [vt[h * NSA_DH:(h + 1) * NSA_DH, :], ones_pad]
        return jnp.concatenate(parts, axis=0).astype(_BF16)

    vst_ref[0] = values_t(vs)
    vwt_ref[0] = values_t(proj(_C_VW))


def _in_proj(x2d, norm_w, w_cat, batch, seq, tm):
    n = x2d.shape[0]
    tiles_per_seq = seq // tm
    row = lambda w: pl.BlockSpec((tm, w), lambda i: (i, 0))
    tr_map = lambda i: (i // tiles_per_seq, 0, i % tiles_per_seq)
    tr = pl.BlockSpec((1, LANES, tm), tr_map)
    trv = pl.BlockSpec((1, NSA_KV_HEADS * VT_ROWS, tm), tr_map)
    sds = jax.ShapeDtypeStruct
    return pl.pallas_call(
        _in_proj_kernel,
        out_shape=(
            sds((n, GDN_CONV_W), _F32), sds((n, GDN_W), _F32), sds((n, LANES), _F32),
            sds((batch, LANES, seq), _F32), sds((n, NSA_W), _BF16), sds((n, KV_W), _BF16),
            sds((n, KV_W), _BF16), sds((n, KV_W), _BF16), sds((n, KV_W), _BF16),
            sds((batch, NSA_KV_HEADS * VT_ROWS, seq), _BF16), sds((batch, NSA_KV_HEADS * VT_ROWS, seq), _BF16),
            sds((n, NSA_W), _F32)),
        grid=(n // tm,),
        in_specs=[row(D_MODEL), pl.BlockSpec((1, D_MODEL), lambda i: (0, 0)),
                  pl.BlockSpec((D_MODEL, _W_COLS), lambda i: (0, 0))],
        out_specs=(row(GDN_CONV_W), row(GDN_W), row(LANES), tr, row(NSA_W), row(KV_W), row(KV_W),
                   row(KV_W), row(KV_W), trv, trv, row(NSA_W)),
        compiler_params=pltpu.CompilerParams(dimension_semantics=("arbitrary",),
                                             vmem_limit_bytes=VMEM_LIMIT),
        name="in_proj",
    )(x2d, norm_w.reshape(1, D_MODEL).astype(_F32), w_cat)


_GDN_PAIR = 2 * CHUNK
_STACK = GDN_HEADS * CHUNK
_PREP_UNROLL = 4


def _segment_cumsum(g, axis):
    pos = lax.broadcasted_iota(jnp.int32, g.shape, axis) % CHUNK
    k = 1
    while k < CHUNK:
        g = g + jnp.where(pos >= k, pltpu.roll(g, k, axis), 0.0)
        k *= 2
    return g


def _stacked_unit_lower_inverses(lows, out):
    r = lax.broadcasted_iota(jnp.int32, (_STACK, _STACK), 0)
    c = lax.broadcasted_iota(jnp.int32, (_STACK, _STACK), 1)
    eye = (r == c).astype(_F32)
    ms = [-low for low in lows]
    xs = [eye + m for m in ms]
    ms = [_dot(m, m) for m in ms]
    yield
    span = 4
    while span < CHUNK:
        boths = [_dot(jnp.concatenate([x, m], axis=0), m) for x, m in zip(xs, ms)]
        xs = [x + both[:_STACK] for x, both in zip(xs, boths)]
        ms = [both[_STACK:] for both in boths]
        span *= 2
        yield
    out.extend(x + _dot(x, m) for x, m in zip(xs, ms))
    yield


def _gdn_kernel(x_ref, cw_ref, sm_ref, smt_ref, prow_ref, pcol_ref, gz_ref, gnw_ref, o_ref,
                xs_ref, q_ref, k_ref, v_ref, col_ref, grow_ref, state_ref,
                u_ref, w_ref, qd_ref, kd_ref, a_ref, egl_ref, *, tile, tiles_per_seq):
    t_idx = pl.program_id(0)
    slot_1 = t_idx % 2
    slot_2 = 1 - slot_1
    half = tile // 2

    @pl.when(t_idx == 0)
    def _():
        for ref in (q_ref, k_ref, v_ref, col_ref, grow_ref):
            ref[1] = jnp.zeros(ref.shape[1:], _F32)
        for ref in (u_ref, w_ref, qd_ref, kd_ref, a_ref, egl_ref):
            ref[0] = jnp.zeros(ref.shape[1:], _F32)

    @pl.when(t_idx % tiles_per_seq == 0)
    def _():
        xs_ref[0:8, :] = jnp.zeros((8, GDN_CONV_W), _F32)

    @pl.when(t_idx % tiles_per_seq != 0)
    def _():
        xs_ref[0:8, :] = xs_ref[tile:tile + 8, :]

    @pl.when((t_idx == 0) | ((t_idx - 2) % tiles_per_seq == 0))
    def _():
        state_ref[...] = jnp.zeros_like(state_ref)

    xs_ref[8:tile + 8, :] = x_ref[0]
    heads = range(GDN_HEADS)

    def stage1_half(pi):
        r0 = pl.multiple_of(pi * half, half)
        rows = pl.ds(r0, half)
        xe = xs_ref[pl.ds(r0, half + 8), :]
        y = jnp.zeros((half, GDN_CONV_W), _F32)
        for tap in range(CONV_K):
            off = 8 - (CONV_K - 1) + tap
            y = y + cw_ref[tap:tap + 1, :] * xe[off:off + half, :]
        y = _silu(y)
        for h in heads:
            lo = h * GDN_DK
            qh = y[:, lo:lo + GDN_DK]
            kh = y[:, GDN_QK_W + lo:GDN_QK_W + lo + GDN_DK]
            q_ref[slot_1, rows, lo:lo + GDN_DK] = (
                qh * lax.rsqrt(jnp.sum(qh * qh, axis=-1, keepdims=True) + EPS) * GDN_DK ** -0.5)
            k_ref[slot_1, rows, lo:lo + GDN_DK] = kh * lax.rsqrt(jnp.sum(kh * kh, axis=-1, keepdims=True) + EPS)
        v_ref[slot_1, rows, :] = y[:, 2 * GDN_QK_W:]
        sm = sm_ref[0, rows, :]
        lane = lax.broadcasted_iota(jnp.int32, sm.shape, 1)
        zc = sm + prow_ref[1:2, :]
        gcol = -jnp.exp(prow_ref[0:1, :]) * (jnp.maximum(zc, 0.0) + jnp.log1p(jnp.exp(-jnp.abs(zc))))
        col_ref[slot_1, rows, :] = jnp.where(lane < GDN_HEADS, _sigmoid(sm), _segment_cumsum(gcol, 0))
        zr = smt_ref[0, 0:8, rows] + pcol_ref[1]
        grow = -jnp.exp(pcol_ref[0]) * (jnp.maximum(zr, 0.0) + jnp.log1p(jnp.exp(-jnp.abs(zr))))
        grow_ref[slot_1, :, rows] = _segment_cumsum(grow, 1)

    r = lax.broadcasted_iota(jnp.int32, (_STACK, _STACK), 0)
    c = lax.broadcasted_iota(jnp.int32, (_STACK, _STACK), 1)
    same_head = (r // CHUNK) == (c // CHUNK)
    tril = same_head & (r >= c)
    strict = same_head & (r > c)

    def prep_gram(ci, odd):
        rows = pl.ds(pl.multiple_of(ci * CHUNK, CHUNK), CHUNK)
        slab = pl.ds(pl.multiple_of((ci // 2) * _GDN_PAIR, _GDN_PAIR), _GDN_PAIR)
        colv = col_ref[slot_2, rows, :]
        grow2 = grow_ref[slot_2, :, slab]

        def stack(ref, width):
            return jnp.concatenate([ref[slot_2, rows, h * width:(h + 1) * width] for h in heads], axis=0)

        q = stack(q_ref, GDN_DK)
        k = stack(k_ref, GDN_DK)
        v = stack(v_ref, GDN_DV)
        beta = jnp.concatenate([colv[:, h:h + 1] for h in heads], axis=0)
        gc = jnp.concatenate([colv[:, GDN_HEADS + h:GDN_HEADS + h + 1] for h in heads], axis=0)
        gr = jnp.concatenate([grow2[GDN_HEADS + h:GDN_HEADS + h + 1, odd * CHUNK:(odd + 1) * CHUNK]
                              for h in heads], axis=1)
        last = [colv[CHUNK - 1:CHUNK, GDN_HEADS + h:GDN_HEADS + h + 1] for h in heads]
        g_last = jnp.concatenate([jnp.broadcast_to(x, (CHUNK, 1)) for x in last], axis=0)
        decay = jnp.where(tril, jnp.exp(jnp.where(tril, gc - gr, 0.0)), 0.0)
        kb = k * beta
        eg = jnp.exp(gc)
        low = jnp.where(strict, _dot_nt(kb, k) * decay, 0.0)
        a_ref[slot_2, ci] = jnp.where(tril, _dot_nt(q, k) * decay, 0.0)
        kd_ref[slot_2, ci] = k * jnp.exp(g_last - gc)
        qd_ref[slot_2, ci] = q * eg
        egl_ref[slot_2, ci] = jnp.concatenate([jnp.broadcast_to(jnp.exp(x), (1, GDN_DV)) for x in last]
                                              + [jnp.zeros((8 - GDN_HEADS, GDN_DV), _F32)], axis=0)
        return low, jnp.concatenate([v * beta, kb * eg], axis=1)

    gnw = gnw_ref[...]

    def stage3_chunks(cis):
        states = [state_ref[h] for h in heads]
        for ci in cis:
            u, w, qd, kd = u_ref[slot_1, ci], w_ref[slot_1, ci], qd_ref[slot_1, ci], kd_ref[slot_1, ci]
            egl = egl_ref[slot_1, ci]
            rows = pl.ds(pl.multiple_of(ci * CHUNK, CHUNK), CHUNK)
            hrows = [slice(h * CHUNK, (h + 1) * CHUNK) for h in heads]
            wqs = [_dot(jnp.concatenate([w[rs], qd[rs]], axis=0), states[h]) for h, rs in zip(heads, hrows)]
            yield
            v_news = [u[rs] - wq[:CHUNK] for rs, wq in zip(hrows, wqs)]
            for h in heads:
                states[h] = states[h] * egl[h:h + 1, :] + _dot_tn(kd[hrows[h]], v_news[h])
            o = (jnp.concatenate([wq[CHUNK:] for wq in wqs], axis=0)
                 + _dot(a_ref[slot_1, ci], jnp.concatenate(v_news, axis=0)))
            for h in heads:
                lanes = slice(h * GDN_DV, (h + 1) * GDN_DV)
                oh = o[hrows[h]]
                oh = oh * lax.rsqrt(jnp.mean(oh * oh, axis=-1, keepdims=True) + EPS) * gnw
                o_ref[0, rows, lanes] = (oh * _silu(gz_ref[0, rows, lanes])).astype(_BF16)
            yield
        for h in heads:
            state_ref[h] = states[h]

    def stage2_chunks(cis):
        lows, rhss = zip(*[prep_gram(ci, j % 2) for j, ci in enumerate(cis)])
        yield
        tinvs = []
        yield from _stacked_unit_lower_inverses(lows, tinvs)
        sols = [_dot(tinv, rhs) for tinv, rhs in zip(tinvs, rhss)]
        for ci, sol in zip(cis, sols):
            u_ref[slot_2, ci] = sol[:, :GDN_DV]
            w_ref[slot_2, ci] = sol[:, GDN_DV:]

    def pipeline_body(pi, carry):
        cis = [pi * _PREP_UNROLL + j for j in range(_PREP_UNROLL)]
        streams = [stage3_chunks(cis), stage2_chunks(cis)]
        first_round = True
        while streams:
            for stream in list(streams):
                if next(stream, StopIteration) is StopIteration:
                    streams.remove(stream)
            if first_round:
                stage1_half(pi)
                first_round = False
        return carry

    assert tile // CHUNK == 2 * _PREP_UNROLL
    lax.fori_loop(0, 2, pipeline_body, 0)


def _gdn(conv_in, conv_w, sm, smt, a_log, dt_bias, gz, gdn_norm_w, tile):
    batch, seq, _ = conv_in.shape
    prow = jnp.zeros((2, LANES), _F32)
    prow = prow.at[0, GDN_HEADS:2 * GDN_HEADS].set(a_log.astype(_F32))
    prow = prow.at[1, GDN_HEADS:2 * GDN_HEADS].set(dt_bias.astype(_F32))
    pcol = jnp.broadcast_to(prow[:, :8, None], (2, 8, tile // 2))
    n_chunks = tile // CHUNK
    assert _PREP_UNROLL % 2 == 0
    tiles_per_seq = seq // tile
    n_tiles = batch * tiles_per_seq
    kern = functools.partial(_gdn_kernel, tile=tile, tiles_per_seq=tiles_per_seq)

    def stage1_tile(i):
        i = jnp.minimum(i, n_tiles - 1)
        return i // tiles_per_seq, i % tiles_per_seq

    def stage3_tile(i):
        i = jnp.maximum(i - 2, 0)
        return i // tiles_per_seq, i % tiles_per_seq

    rows_of = lambda pick: (lambda i: (*pick(i), 0))
    const = lambda i: (0, 0)
    return pl.pallas_call(
        kern,
        out_shape=jax.ShapeDtypeStruct((batch, seq, GDN_W), _BF16),
        grid=(n_tiles + 2,),
        in_specs=[
            pl.BlockSpec((1, tile, GDN_CONV_W), rows_of(stage1_tile)),
            pl.BlockSpec((CONV_K, GDN_CONV_W), const),
            pl.BlockSpec((1, tile, LANES), rows_of(stage1_tile)),
            pl.BlockSpec((1, LANES, tile), lambda i: (stage1_tile(i)[0], 0, stage1_tile(i)[1])),
            pl.BlockSpec((2, LANES), const),
            pl.BlockSpec((2, 8, tile // 2), lambda i: (0, 0, 0)),
            pl.BlockSpec((1, tile, GDN_W), rows_of(stage3_tile)),
            pl.BlockSpec((1, GDN_DV), const),
        ],
        out_specs=pl.BlockSpec((1, tile, GDN_W), rows_of(stage3_tile)),
        scratch_shapes=[
            pltpu.VMEM((tile + 8, GDN_CONV_W), _F32),
            pltpu.VMEM((2, tile, GDN_QK_W), _F32),
            pltpu.VMEM((2, tile, GDN_QK_W), _F32),
            pltpu.VMEM((2, tile, GDN_W), _F32),
            pltpu.VMEM((2, tile, LANES), _F32),
            pltpu.VMEM((2, 8, tile), _F32),
            pltpu.VMEM((GDN_HEADS, GDN_DK, GDN_DV), _F32),
            pltpu.VMEM((2, n_chunks, _STACK, GDN_DV), _F32),
            pltpu.VMEM((2, n_chunks, _STACK, GDN_DV), _F32),
            pltpu.VMEM((2, n_chunks, _STACK, GDN_DK), _F32),
            pltpu.VMEM((2, n_chunks, _STACK, GDN_DK), _F32),
            pltpu.VMEM((2, n_chunks, _STACK, _STACK), _F32),
            pltpu.VMEM((2, n_chunks, 8, GDN_DV), _F32),
        ],
        compiler_params=pltpu.CompilerParams(dimension_semantics=("arbitrary",),
                                             vmem_limit_bytes=VMEM_LIMIT),
        name="gdn",
    )(conv_in, conv_w.astype(_F32), sm, smt, prow, pcol, gz, gdn_norm_w.reshape(1, GDN_DV).astype(_F32))


_GROUP_W = CMP_STRIDE * KV_W
_HID2 = NSA_KV_HEADS * CMP_HIDDEN


def _compress_weights(pe, w1, w2):
    eye = jnp.eye(NSA_KV_HEADS, dtype=_F32)
    halves = []
    pes = []
    for part in range(CMP_LEN // CMP_STRIDE):
        w1p = w1[part * CMP_STRIDE:(part + 1) * CMP_STRIDE].astype(_F32)
        big = jnp.einsum('ldf,hg->lhdgf', w1p, eye).reshape(_GROUP_W, _HID2)
        halves.append(big.astype(_BF16))
        pep = pe[part * CMP_STRIDE:(part + 1) * CMP_STRIDE].astype(_F32)
        pes.append(jnp.broadcast_to(pep[:, None, :], (CMP_STRIDE, NSA_KV_HEADS, NSA_DH)).reshape(1, _GROUP_W))
    w2big = jnp.einsum('fd,hg->hfgd', w2.astype(_F32), eye).reshape(_HID2, KV_W).astype(_BF16)
    pe2 = jnp.broadcast_to(jnp.stack(pes, 0), (2, 8, _GROUP_W)).astype(_BF16)
    return halves[0], halves[1], w2big, pe2


def _compress_kernel(kc_ref, vc_ref, kwa_ref, kwb_ref, kw2_ref, kpe_ref, vwa_ref, vwb_ref, vw2_ref, vpe_ref,
                     ko_ref, vo_ref):
    def mlp(g_ref, wa_ref, wb_ref, w2_ref, pe_ref):
        g = g_ref[0]
        ncp = g.shape[0]
        p0 = _dot(g, wa_ref[...])
        p1 = _dot(g, wb_ref[...])
        const = _dot(pe_ref[0], wa_ref[...]) + _dot(pe_ref[1], wb_ref[...])
        pre = p0 + pltpu.roll(p1, ncp - 1, 0) + const[0:1, :]
        return _dot(_silu(pre).astype(_BF16), w2_ref[...])

    ko_ref[0] = mlp(kc_ref, kwa_ref, kwb_ref, kw2_ref, kpe_ref).astype(_BF16)
    vo_ref[0] = mlp(vc_ref, vwa_ref, vwb_ref, vw2_ref, vpe_ref).T.astype(_BF16)


def _compress(kc, vc, kweights, vweights):
    batch, ncp, _ = kc.shape
    full = lambda a: pl.BlockSpec(a.shape, lambda b: (0,) * a.ndim)
    per_b = pl.BlockSpec((1, ncp, _GROUP_W), lambda b: (b, 0, 0))
    return pl.pallas_call(
        _compress_kernel,
        out_shape=(jax.ShapeDtypeStruct((batch, ncp, KV_W), _BF16),
                   jax.ShapeDtypeStruct((batch, KV_W, ncp), _BF16)),
        grid=(batch,),
        in_specs=[per_b, per_b] + [full(a) for a in kweights] + [full(a) for a in vweights],
        out_specs=(pl.BlockSpec((1, ncp, KV_W), lambda b: (b, 0, 0)),
                   pl.BlockSpec((1, KV_W, ncp), lambda b: (b, 0, 0))),
        compiler_params=pltpu.CompilerParams(dimension_semantics=("arbitrary",),
                                             vmem_limit_bytes=VMEM_LIMIT),
        name="compress",
    )(kc, vc, *kweights, *vweights)


def _overlap_t(n_sel, ncp, n_cmp):
    cs = CMP_STRIDE * np.arange(ncp)
    ss = SEL_BLOCK * np.arange(n_sel)
    ov = ((cs[None, :] < ss[:, None] + SEL_BLOCK) & (cs[None, :] + CMP_LEN > ss[:, None])
          & (np.arange(ncp)[None, :] < n_cmp))
    return jnp.asarray(ov.astype(np.float32), dtype=_BF16)


def _key_onehots(n_tiles, with_blocks):
    key = np.arange(n_tiles * TQ)[:, None]
    col = np.arange(LANES)[None, :]
    ek = np.broadcast_to((col == 8) | (col == 9), (n_tiles * TQ, LANES))
    if with_blocks:
        ek = ek | (col == key // SEL_BLOCK)
    return jnp.asarray(ek.astype(np.float32), dtype=_BF16)


def _nsa_kernel(*refs, select_phase):
    (q_ref, ks_ref, kw_ref, vst_ref, vwt_ref, kc_ref, vct_ref, tab_ref, far_ref, ovt_ref, eks_ref, ekw_ref,
     smt_ref, nz_ref) = refs[:14]
    if select_phase:
        mask_io_ref, ocw_ref, cb_ref, sc_ref, cnt_ref = refs[14:]
    else:
        mask_io_ref, ocw_ref, o_ref, mb_ref, ss_ref, st_ref, ps_ref = refs[14:]
    a_of = [TILE_PAIR * pl.program_id(1) + t for t in range(TILE_PAIR)]
    ncp = kc_ref.shape[1]
    n_sel = ovt_ref.shape[0]
    heads = range(NSA_KV_HEADS)
    tiles = range(TILE_PAIR)
    chains = [(t, h) for t in tiles for h in heads]
    cid = lambda t, h: t * NSA_KV_HEADS + h
    drows = [slice(h * NSA_DH, (h + 1) * NSA_DH) for h in heads]
    vrows = [slice(h * VT_ROWS, (h + 1) * VT_ROWS) for h in heads]
    qcols = [slice(h * GQ_W, (h + 1) * GQ_W) for h in heads]
    toks = [slice(t * TQ, (t + 1) * TQ) for t in tiles]

    qrow = lax.broadcasted_iota(jnp.int32, (KV_W, QL), 0)

    def transposed_queries(t, h):
        qt = (q_ref[0, toks[t], qcols[h]].astype(_F32) * (NSA_DH ** -0.5 * LOG2E)).T
        qt4 = jnp.concatenate([qt[g * NSA_DH:(g + 1) * NSA_DH, :] for g in range(NSA_GROUP)], axis=1)
        return jnp.where(qrow // NSA_DH == h, jnp.concatenate([qt4] * NSA_KV_HEADS, axis=0), 0.0).astype(_BF16)

    qt_b = {c: transposed_queries(*c) for c in chains}
    far_rows = [far_ref[h] for h in heads]
    zero_rows = jnp.zeros((LANES - 16, QL), _BF16)

    def augmented_queries(c, mask_rows):
        extra = jnp.concatenate([mask_rows, far_rows[c[1]]], axis=0).astype(_BF16)
        return jnp.concatenate([qt_b[c], extra, zero_rows], axis=0)

    def augmented_keys(k_ref, ek, key0, n_tiles):
        keys = pl.ds(pl.multiple_of(key0, TQ), n_tiles * TQ)
        return jnp.concatenate([k_ref[0, keys, :], ek], axis=1)

    def gate(c, branch):
        t, h = c
        first = 8 + 8 * branch + NSA_GROUP * h
        return _sigmoid(jnp.concatenate([smt_ref[0, first + g:first + g + 1, toks[t]]
                                         for g in range(NSA_GROUP)], axis=1))

    def normalized(acc):
        return acc[0:NSA_DH, :] * (1.0 / acc[NSA_DH:NSA_DH + 1, :])

    win_t0 = [jnp.maximum(a - (WIN_TILES - 1), 0) for a in a_of]
    win_keys = [pl.ds(pl.multiple_of(w0 * TQ, TQ), WIN_TILES * TQ) for w0 in win_t0]
    win_start = [pl.multiple_of(jnp.maximum(8 * a - CMP_NEAR_BACK, 0), 8) for a in a_of]
    tab_start = [pl.multiple_of(ws - (8 * a - CMP_NEAR_BACK), 8) for ws, a in zip(win_start, a_of)]

    def compressed_scores(n_rows):
        crow = lax.broadcasted_iota(jnp.int32, (n_rows, QL), 0)
        for t, h in chains:
            c = cid(t, h)
            cb_ref[c, 0:n_rows, :] = jnp.where(crow < win_start[t], far_rows[h][2:3, :], NEG)
            cb_ref[c, pl.ds(win_start[t], CMP_NEAR), :] = tab_ref[h, TAB_CMP, pl.ds(tab_start[t], CMP_NEAR), :]
        k_cmp = kc_ref[0, 0:n_rows, :]
        return {c: _dot(k_cmp, qt_b[c]) + cb_ref[cid(*c), 0:n_rows, :] for c in chains}

    def compressed_branch(s_c, acc_w):
        n_rows = s_c[chains[0]].shape[0]
        m_c = {c: jnp.max(s_c[c], axis=0, keepdims=True) for c in chains}
        e_c = {c: jnp.exp2(s_c[c] - m_c[c]) for c in chains}
        l_c = {c: jnp.sum(e_c[c], axis=0, keepdims=True) for c in chains}
        p_c = {c: e_c[c] * jnp.where(m_c[c] > 0.5 * NEG, 1.0 / l_c[c], 0.0) for c in chains}
        o_c = {c: _dot(vct_ref[0, drows[c[1]], 0:n_rows], p_c[c].astype(_BF16)) for c in chains}
        for c in chains:
            ocw_ref[0, c[0], c[1]] = gate(c, 0) * o_c[c] + gate(c, 2) * normalized(acc_w[c])
        ovt = ovt_ref[:, 0:n_rows]
        jrow = lax.broadcasted_iota(jnp.int32, (n_sel, TQ), 0)
        for c in chains:
            cur = (a_of[c[0]] * TQ + lax.broadcasted_iota(jnp.int32, (n_sel, TQ), 1)) // SEL_BLOCK
            forced = (jrow == 0) | (jrow == cur) | (jrow == cur - 1)
            psum = p_c[c][:, 0:TQ]
            for g in range(1, NSA_GROUP):
                psum = psum + p_c[c][:, g * TQ:(g + 1) * TQ]
            hi = psum.astype(_BF16)
            r1 = psum - hi.astype(_F32)
            mid = r1.astype(_BF16)
            lo = (r1 - mid.astype(_F32)).astype(_BF16)
            imp = _dot(ovt, hi) + _dot(ovt, mid) + _dot(ovt, lo)
            sc_ref[cid(*c)] = jnp.where(jrow <= cur, imp + jnp.where(forced, FORCE_BONUS, 0.0), NEG)

    def window_entries(t):
        entries = []
        for i in range(WIN_TILES):
            d = a_of[t] - (win_t0[t] + i)
            entries.append(jnp.where(d < 0, TAB_MASKED, jnp.where(d == WIN_TILES - 1, TAB_WIN_EDGE, d)))
        return entries

    win_entries = [window_entries(t) for t in tiles]

    def front_stage(n_rows):
        ekw = ekw_ref[...]
        k_aug_w = [augmented_keys(kw_ref, ekw, win_t0[t] * TQ, WIN_TILES) for t in tiles]
        no_mask = jnp.zeros((8, QL), _F32)
        s_w = {c: _dot(k_aug_w[c[0]], augmented_queries(c, no_mask)) for c in chains}
        s_c = compressed_scores(n_rows)
        s_wb = {c: jnp.concatenate([s_w[c][i * TQ:(i + 1) * TQ, :] + tab_ref[c[1], win_entries[c[0]][i]]
                                    for i in range(WIN_TILES)], axis=0) for c in chains}
        m_w = {c: jnp.max(s_wb[c], axis=0, keepdims=True) for c in chains}
        acc_w = {c: _dot(vwt_ref[0, vrows[c[1]], win_keys[c[0]]], jnp.exp2((s_wb[c] - m_w[c]).astype(_BF16)))
                 for c in chains}
        compressed_branch(s_c, acc_w)

    half_rows = ncp // 2
    needs_all = 8 * a_of[-1] + 8 > half_rows

    if select_phase:
        pl.when(jnp.logical_not(needs_all))(lambda: front_stage(half_rows))
        pl.when(needs_all)(lambda: front_stage(ncp))
        cnt_ref[...] = jnp.zeros(cnt_ref.shape, _F32)

    sub = lax.broadcasted_iota(jnp.int32, (8, TQ), 0)
    last_valid = (TQ // SEL_BLOCK) * (a_of[-1] + 1) - 1
    for chunk in range(n_sel // RANK_CHUNK if select_phase else 0):
        for row_chunk in range(n_sel // RANK_ROWS):
            row_groups = range(row_chunk * RANK_ROWS // 8, (row_chunk + 1) * RANK_ROWS // 8)

            @pl.when((chunk * RANK_CHUNK <= last_valid) & (row_chunk * RANK_ROWS <= last_valid))
            def _():
                for c in range(len(chains)):
                    rows = {r: sc_ref[c, 8 * r:8 * r + 8, :] for r in row_groups}
                    cnts = {r: cnt_ref[c, 8 * r:8 * r + 8, :] for r in row_groups}
                    for jp in range(chunk * RANK_CHUNK, (chunk + 1) * RANK_CHUNK):
                        srow = sc_ref[c, jp:jp + 1, :]
                        for r in row_groups:
                            if 8 * r > jp:
                                inc = jnp.where(srow >= rows[r], 1.0, 0.0)
                            elif 8 * r + 7 < jp:
                                inc = jnp.where(srow > rows[r], 1.0, 0.0)
                            else:
                                tie = jnp.where(sub + 8 * r > jp, 1.0, 0.0)
                                inc = jnp.where(srow > rows[r], 1.0, jnp.where(srow >= rows[r], tie, 0.0))
                            cnts[r] = cnts[r] + inc
                    for r in row_groups:
                        cnt_ref[c, 8 * r:8 * r + 8, :] = cnts[r]

    if select_phase:
        for t, h in chains:
            mask_io_ref[0, t, h] = jnp.where(cnt_ref[cid(t, h)] < SEL_TOP_N, 0.0, NEG)
        return
    for t, h in chains:
        mb_ref[cid(t, h)] = jnp.concatenate([mask_io_ref[0, t, h]] * NSA_GROUP, axis=1)

    eks = eks_ref[...]
    blocks_per_group = GROUP_TILES * (TQ // SEL_BLOCK)
    group_keys = GROUP_TILES * TQ
    last_group = [a // GROUP_TILES for a in a_of]

    def sel_scores(t, gi):
        k_aug = augmented_keys(ks_ref, eks, gi * group_keys, GROUP_TILES)
        mask_rows = pl.ds(pl.multiple_of(gi * blocks_per_group, 8), blocks_per_group)
        return [_dot(k_aug, augmented_queries((t, h), mb_ref[cid(t, h), mask_rows, :])) for h in heads]

    def sel_values(h, gi):
        return vst_ref[0, vrows[h], pl.ds(pl.multiple_of(gi * group_keys, group_keys), group_keys)]

    def sel_step(gi, carry, t, near, last):
        s_next = None if last else sel_scores(t, gi + 1)
        pv_prev = [_dot(sel_values(h, jnp.maximum(gi - 1, 0)), ps_ref[cid(t, h)]) for h in heads]
        entries = []
        if near:
            for i in range(GROUP_TILES):
                d = a_of[t] - (gi * GROUP_TILES + i)
                entries.append(jnp.where(d < 0, TAB_MASKED, jnp.where(d >= N_NEAR, TAB_ZERO, d)))
        new_carry = []
        for h in heads:
            c = cid(t, h)
            m, alpha_prev, acc = carry[h]
            s = ss_ref[c]
            if near:
                s = jnp.concatenate([s[i * TQ:(i + 1) * TQ, :] + tab_ref[h, entries[i]]
                                     for i in range(GROUP_TILES)], axis=0)
                st_ref[c] = s
            m_new = jnp.maximum(m, jnp.max(s, axis=0, keepdims=True))
            alpha = jnp.exp2(m - m_new)
            acc = alpha_prev * acc + pv_prev[h]
            s = st_ref[c] if near else ss_ref[c]
            ps_ref[c] = jnp.exp2((s - m_new).astype(_BF16))
            new_carry.append((m_new, alpha, acc))
        if not last:
            for h in heads:
                ss_ref[cid(t, h)] = s_next[h]
        return tuple(new_carry)

    ps_ref[...] = jnp.zeros(ps_ref.shape, _BF16)
    carries = []
    for t in tiles:
        s_first = sel_scores(t, 0)
        for h in heads:
            ss_ref[cid(t, h)] = s_first[h]
        carry = tuple((jnp.full((1, QL), NEG, _F32), jnp.ones((1, QL), _F32), jnp.zeros((VT_ROWS, QL), _F32))
                      for _ in heads)
        n_far = jnp.maximum(last_group[t] - N_NEAR // GROUP_TILES, 0)
        carry = lax.fori_loop(0, n_far, functools.partial(sel_step, t=t, near=False, last=False), carry)
        carry = lax.fori_loop(n_far, last_group[t], functools.partial(sel_step, t=t, near=True, last=False), carry)
        carries.append(carry)

    carries = [sel_step(last_group[t], carries[t], t=t, near=True, last=True) for t in tiles]

    acc_s = {}
    for c in chains:
        t, h = c
        _, alpha_s, acc = carries[t][h]
        acc_s[c] = alpha_s * acc + _dot(sel_values(h, last_group[t]), ps_ref[cid(t, h)])
    for c in chains:
        t, h = c
        o_t = ocw_ref[0, t, h] + gate(c, 1) * normalized(acc_s[c])
        stacked = jnp.concatenate([o_t[:, g * TQ:(g + 1) * TQ] for g in range(NSA_GROUP)], axis=0)
        o_ref[0, toks[t], qcols[h]] = (stacked.T * _silu(nz_ref[0, toks[t], qcols[h]])).astype(_BF16)


def _nsa(nq, ks, kw, vst, vwt, kcmp, vcmpt, tab, far, ovt, smt, nz):
    batch, seq, _ = nq.shape
    ncp = kcmp.shape[1]
    n_sel = seq // SEL_BLOCK
    per_b = lambda shape: pl.BlockSpec((1,) + shape, lambda b, a: (b, 0, 0))
    const = lambda shape: pl.BlockSpec(shape, lambda b, a: (0,) * len(shape), pipeline_mode=pl.Buffered(1))
    step_tokens = TILE_PAIR * TQ
    tile = lambda width: pl.BlockSpec((1, step_tokens, width), lambda b, a: (b, a, 0))
    vt_shape = (NSA_KV_HEADS * VT_ROWS, seq)
    per_head = lambda shape, dtype: pltpu.VMEM((TILE_PAIR * NSA_KV_HEADS,) + shape, dtype)
    n_tiles = seq // TQ
    per_chain = lambda shape: pl.BlockSpec((1, TILE_PAIR, NSA_KV_HEADS) + shape, lambda b, a: (b, a, 0, 0, 0))
    mask_shape = jax.ShapeDtypeStruct((batch, n_tiles, NSA_KV_HEADS, n_sel, TQ), _F32)
    ocw_shape = jax.ShapeDtypeStruct((batch, n_tiles, NSA_KV_HEADS, NSA_DH, QL), _F32)
    common_specs = [
        tile(NSA_W),
        per_b((seq, KV_W)), per_b((seq, KV_W)), per_b(vt_shape), per_b(vt_shape),
        per_b((ncp, KV_W)), per_b((KV_W, ncp)),
        const((NSA_KV_HEADS, N_TAB, TQ, QL)), const((NSA_KV_HEADS, 8, QL)),
        const((n_sel, ncp)), const((GROUP_TILES * TQ, LANES)), const((WIN_TILES * TQ, LANES)),
        pl.BlockSpec((1, LANES, step_tokens), lambda b, a: (b, 0, a)),
        tile(NSA_W),
    ]
    common_args = (nq, ks, kw, vst, vwt, kcmp, vcmpt, tab, far, ovt, _key_onehots(GROUP_TILES, True),
                   _key_onehots(WIN_TILES, False), smt, nz)
    params = pltpu.CompilerParams(dimension_semantics=("arbitrary", "arbitrary"), vmem_limit_bytes=VMEM_LIMIT)
    mask, ocw = pl.pallas_call(
        functools.partial(_nsa_kernel, select_phase=True),
        out_shape=(mask_shape, ocw_shape),
        grid=(batch, seq // step_tokens),
        in_specs=common_specs,
        out_specs=(per_chain((n_sel, TQ)), per_chain((NSA_DH, QL))),
        scratch_shapes=[per_head((ncp, QL), _F32), per_head((n_sel, TQ), _F32), per_head((n_sel, TQ), _F32)],
        compiler_params=params,
        name="nsa_select",
    )(*common_args)
    return pl.pallas_call(
        functools.partial(_nsa_kernel, select_phase=False),
        out_shape=jax.ShapeDtypeStruct((batch, seq, NSA_W), _BF16),
        grid=(batch, seq // step_tokens),
        in_specs=common_specs + [per_chain((n_sel, TQ)), per_chain((NSA_DH, QL))],
        out_specs=tile(NSA_W),
        scratch_shapes=[per_head((n_sel, QL), _F32), per_head((GROUP_TILES * TQ, QL), _F32),
                        per_head((GROUP_TILES * TQ, QL), _F32), per_head((GROUP_TILES * TQ, QL), _BF16)],
        compiler_params=params,
        name="nsa",
    )(*common_args, mask, ocw)


def _out_proj_kernel(og_ref, on_ref, x_ref, w_ref, fw_ref, o_ref):
    h = x_ref[...] + _dot(og_ref[...], w_ref[0:GDN_W, :]) + _dot(on_ref[...], w_ref[GDN_W:, :])
    o_ref[...] = h * lax.rsqrt(jnp.mean(h * h, axis=-1, keepdims=True) + EPS) * fw_ref[...]


def _out_proj(o_g, o_n, x2d, w_out, final_norm_w, tm):
    n = x2d.shape[0]
    row = lambda w: pl.BlockSpec((tm, w), lambda i: (i, 0))
    return pl.pallas_call(
        _out_proj_kernel,
        out_shape=jax.ShapeDtypeStruct((n, D_MODEL), _F32),
        grid=(n // tm,),
        in_specs=[row(GDN_W), row(NSA_W), row(D_MODEL),
                  pl.BlockSpec((GDN_W + NSA_W, D_MODEL), lambda i: (0, 0)),
                  pl.BlockSpec((1, D_MODEL), lambda i: (0, 0))],
        out_specs=row(D_MODEL),
        compiler_params=pltpu.CompilerParams(dimension_semantics=("arbitrary",),
                                             vmem_limit_bytes=VMEM_LIMIT),
        name="out_proj",
    )(o_g, o_n, x2d, w_out.astype(_BF16), final_norm_w.reshape(1, D_MODEL).astype(_F32))


def kernel(x, norm_w, w_in, conv_w, a_log, dt_bias, gdn_norm_w, cmp_pe_k, cmp_pe_v, cmp_k_w1, cmp_k_w2,
           cmp_v_w1, cmp_v_w2, w_out, rel_bias, final_norm_w):
    batch, seq, _ = x.shape
    assert w_in.shape[0] == 1, "single-layer problem"
    assert seq % (8 * TQ) == 0
    n = batch * seq
    ncp = seq // CMP_STRIDE
    n_cmp = (seq - CMP_LEN) // CMP_STRIDE + 1
    tm = min(512, seq)

    x2d = x.reshape(n, D_MODEL)
    tab, far = _bias_tables(rel_bias)
    (conv_in, gz, sm, smt, nq, kc, vc, ks, kw, vst, vwt, nz) = _in_proj(
        x2d, norm_w[0], _regroup_w_in(w_in[0]), batch, seq, tm)

    o_g = _gdn(conv_in.reshape(batch, seq, GDN_CONV_W), conv_w[0], sm.reshape(batch, seq, LANES), smt,
               a_log[0], dt_bias[0], gz.reshape(batch, seq, GDN_W), gdn_norm_w[0], tm)

    kcmp, vcmpt = _compress(kc.reshape(batch, ncp, _GROUP_W), vc.reshape(batch, ncp, _GROUP_W),
                            _compress_weights(cmp_pe_k[0], cmp_k_w1[0], cmp_k_w2[0]),
                            _compress_weights(cmp_pe_v[0], cmp_v_w1[0], cmp_v_w2[0]))

    o_n = _nsa(nq.reshape(batch, seq, NSA_W), ks.reshape(batch, seq, KV_W), kw.reshape(batch, seq, KV_W),
               vst, vwt, kcmp, vcmpt, tab, far, _overlap_t(seq // SEL_BLOCK, ncp, n_cmp), smt,
               nz.reshape(batch, seq, NSA_W))

    out = _out_proj(o_g.reshape(n, GDN_W), o_n.reshape(n, NSA_W), x2d, w_out[0], final_norm_w, 2 * tm)
    return out.reshape(batch, seq, D_MODEL)
```
